```python
import jax
import jax.numpy as jnp
from jax import lax
import numpy as np

D_MODEL = 1024
BATCH = 16
SEQ = 256
DEPTH = 4
DEC_BATCH = 8
DEC_SEQ = 1024
PAST_LEN = 512

GRID_W = 64
N_MIX = 3
N_A = (DEPTH + 2) // 3
N_B = (DEPTH + 1) // 3
N_C = DEPTH // 3
CHUNK = 128
A_HALF = 2 * D_MODEL
A_GROUPS = 8
N_HEADS = 16
HEAD_DIM = D_MODEL // N_HEADS
WIN_ROWS_MAX = 8
WIN_COLS = 16
CONV_W = 3
D_FF = 4 * D_MODEL
EPS = 1e-6

kernel_name = 'hybrid_diffusion_gmlp_natten_shortconv_step'


def rms_norm(x, g):
    xf = x.astype(jnp.float32)
    y = xf * lax.rsqrt(jnp.mean(xf * xf, axis=-1, keepdims=True) + EPS)
    return (y * g.astype(jnp.float32)).astype(x.dtype)


def modulation(cond, w, b):
    m = jax.nn.silu(cond) @ w + b
    return jnp.split(m[:, None, :], 6, axis=-1)


def modulate(h, shift, scale):
    return h * (1 + scale) + shift


def sq_relu_mlp(h, w1, w2):
    return jnp.square(jax.nn.relu(h @ w1)) @ w2


def chunk_gmlp(h, w_in, v_gain, ws, bs, w_out):
    b, l, _ = h.shape
    z = jax.nn.gelu(h @ w_in)
    u, v = jnp.split(z, 2, axis=-1)
    v = rms_norm(v, v_gain).reshape(b, l // CHUNK, CHUNK, A_GROUPS, A_HALF // A_GROUPS)
    s = jnp.einsum('gpq,bnqgc->bnpgc', ws, v) + bs.T[None, None, :, :, None]
    return (u * s.reshape(b, l, A_HALF)) @ w_out


def short_gated_conv(h, w_in, conv_w, conv_b, w_out):
    bg, cg, xt = jnp.split(h @ w_in, 3, axis=-1)
    z = cg * xt
    zc = lax.conv_general_dilated(z, conv_w[:, None, :], window_strides=(1,), padding=((1, 1),),
                                  dimension_numbers=('NWC', 'WIO', 'NWC'),
                                  feature_group_count=D_MODEL) + conv_b
    return (bg * zc) @ w_out


def qkv_heads(h, w_qkv, q_gain, k_gain):
    b, l, _ = h.shape
    qkv = (h @ w_qkv).reshape(b, l, 3, N_HEADS, HEAD_DIM)
    return rms_norm(qkv[:, :, 0], q_gain), rms_norm(qkv[:, :, 1], k_gain), qkv[:, :, 2]


def context_attention(q, k, v):
    b, l, _, _ = q.shape
    s = jnp.einsum('bqhd,bkhd->bhqk', q, k).astype(jnp.float32) * (HEAD_DIM ** -0.5)
    p = jax.nn.softmax(s, axis=-1).astype(v.dtype)
    return jnp.einsum('bhqk,bkhd->bqhd', p, v).reshape(b, l, D_MODEL)


def neighbourhood_attention(q, k, v, ck, cv, rpb):
    b, l, _, _ = q.shape
    rows = l // GRID_W
    wr = min(WIN_ROWS_MAX, rows)
    r = jnp.arange(rows)
    row_idx = jnp.clip(r - wr // 2, 0, rows - wr)[:, None] + jnp.arange(wr)[None, :]
    col = jnp.arange(GRID_W)
    col_start = jnp.clip(col - WIN_COLS // 2, 0, GRID_W - WIN_COLS)
    col_ok = (col[None, :] >= col_start[:, None]) & (col[None, :] < col_start[:, None] + WIN_COLS)
    rel_r = row_idx - r[:, None] + (WIN_ROWS_MAX - 1)
    rel_c = jnp.clip(col[None, :] - col[:, None] + (WIN_COLS - 1), 0, 2 * WIN_COLS - 2)
    bias = rpb[:, rel_r[:, None, :, None], rel_c[None, :, None, :]]
    bias = jnp.moveaxis(bias, 0, 1).astype(jnp.float32)
    qg = q.reshape(b, rows, GRID_W, N_HEADS, HEAD_DIM)
    k_band = k.reshape(b, rows, GRID_W, N_HEADS, HEAD_DIM)[:, row_idx]
    v_band = v.reshape(b, rows, GRID_W, N_HEADS, HEAD_DIM)[:, row_idx]
    scale = HEAD_DIM ** -0.5
    s_loc = jnp.einsum('brqhd,brikhd->brhqik', qg, k_band).astype(jnp.float32) * scale + bias
    s_loc = jnp.where(col_ok[:, None, :], s_loc, -jnp.inf)
    s_ctx = jnp.einsum('brqhd,bkhd->brhqk', qg, ck).astype(jnp.float32) * scale
    n_loc = wr * GRID_W
    s_all = jnp.concatenate([s_loc.reshape(b, rows, N_HEADS, GRID_W, n_loc), s_ctx], axis=-1)
    p = jax.nn.softmax(s_all, axis=-1).astype(v.dtype)
    p_loc = p[..., :n_loc].reshape(b, rows, N_HEADS, GRID_W, wr, GRID_W)
    o = (jnp.einsum('brhqik,brikhd->brqhd', p_loc, v_band)
         + jnp.einsum('brhqk,bkhd->brqhd', p[..., n_loc:], cv))
    return o.reshape(b, l, D_MODEL)


def setup_inputs(seed: int = 0) -> dict:
    key = jax.random.key(seed)
    ks = jax.random.split(key, 26)
    D = D_MODEL

    def nrm(k, shape, scale):
        return jax.random.normal(k, shape, jnp.float32) * scale

    return {
        'x_prompt': nrm(ks[0], (BATCH, SEQ, D), 1.0),
        'x_sample': nrm(ks[1], (DEC_BATCH, DEC_SEQ, D), 1.0),
        'cache_k': nrm(ks[2], (DEC_BATCH, N_B, PAST_LEN, N_HEADS, HEAD_DIM), 1.0),
        'cache_v': nrm(ks[3], (DEC_BATCH, N_B, PAST_LEN, N_HEADS, HEAD_DIM), 1.0),
        'c': nrm(ks[4], (DEC_BATCH, D), 1.0),
        'c_ctx': nrm(ks[5], (D,), 1.0),
        'norm_g': 1.0 + nrm(ks[6], (DEPTH, 2, D), 0.02),
        'ada_w': nrm(ks[7], (DEPTH, D, 6 * D), D ** -0.5),
        'ada_b': nrm(ks[8], (DEPTH, 6 * D), 0.02),
        'a_w_in': nrm(ks[9], (N_A, D, 2 * A_HALF), D ** -0.5),
        'a_v_gain': 1.0 + nrm(ks[10], (N_A, A_HALF), 0.02),
        'a_ws': nrm(ks[11], (N_A, A_GROUPS, CHUNK, CHUNK), CHUNK ** -0.5),
        'a_bs': 1.0 + nrm(ks[12], (N_A, A_GROUPS, CHUNK), 0.02),
        'a_w_out': nrm(ks[13], (N_A, A_HALF, D), A_HALF ** -0.5),
        'b_w_qkv': nrm(ks[14], (N_B, D, 3 * D), D ** -0.5),
        'b_q_gain': 1.0 + nrm(ks[15], (N_B, HEAD_DIM), 0.02),
        'b_k_gain': 1.0 + nrm(ks[16], (N_B, HEAD_DIM), 0.02),
        'b_rpb': nrm(ks[17], (N_B, N_HEADS, 2 * WIN_ROWS_MAX - 1, 2 * WIN_COLS - 1), 0.1),
        'b_w_o': nrm(ks[18], (N_B, D, D), D ** -0.5),
        'c_w_in': nrm(ks[19], (N_C, D, 3 * D), D ** -0.5),
        'c_conv_w': nrm(ks[20], (N_C, CONV_W, D), CONV_W ** -0.5),
        'c_conv_b': nrm(ks[21], (N_C, D), 0.02),
        'c_w_out': nrm(ks[22], (N_C, D, D), D ** -0.5),
        'ff_w1': nrm(ks[23], (DEPTH, D, D_FF), D ** -0.5),
        'ff_w2': nrm(ks[24], (DEPTH, D_FF, D), D_FF ** -0.5),
    }


def reference(x_prompt, x_sample, cache_k, cache_v, c, c_ctx, norm_g, ada_w, ada_b,
              a_w_in, a_v_gain, a_ws, a_bs, a_w_out,
              b_w_qkv, b_q_gain, b_k_gain, b_rpb, b_w_o,
              c_w_in, c_conv_w, c_conv_b, c_w_out, ff_w1, ff_w2):
    xp, xs = x_prompt, x_sample
    cp = jnp.broadcast_to(c_ctx, (xp.shape[0], c_ctx.shape[0]))
    new_k, new_v = [], []
    for i in range(DEPTH):
        kind, j = i % N_MIX, i // N_MIX
        mp = modulation(cp, ada_w[i], ada_b[i])
        ms = modulation(c, ada_w[i], ada_b[i])
        hp = modulate(rms_norm(xp, norm_g[i, 0]), mp[0], mp[1])
        hs = modulate(rms_norm(xs, norm_g[i, 0]), ms[0], ms[1])
        if kind == 0:
            op = chunk_gmlp(hp, a_w_in[j], a_v_gain[j], a_ws[j], a_bs[j], a_w_out[j])
            os_ = chunk_gmlp(hs, a_w_in[j], a_v_gain[j], a_ws[j], a_bs[j], a_w_out[j])
        elif kind == 1:
            qp, kp, vp = qkv_heads(hp, b_w_qkv[j], b_q_gain[j], b_k_gain[j])
            new_k.append(kp)
            new_v.append(vp)
            op = context_attention(qp, kp, vp) @ b_w_o[j]
            qs, ks_, vs = qkv_heads(hs, b_w_qkv[j], b_q_gain[j], b_k_gain[j])
            os_ = neighbourhood_attention(qs, ks_, vs, cache_k[:, j], cache_v[:, j], b_rpb[j]) @ b_w_o[j]
        else:
            op = short_gated_conv(hp, c_w_in[j], c_conv_w[j], c_conv_b[j], c_w_out[j])
            os_ = short_gated_conv(hs, c_w_in[j], c_conv_w[j], c_conv_b[j], c_w_out[j])
        xp = xp + mp[2] * op
        xs = xs + ms[2] * os_
        hp = modulate(rms_norm(xp, norm_g[i, 1]), mp[3], mp[4])
        hs = modulate(rms_norm(xs, norm_g[i, 1]), ms[3], ms[4])
        xp = xp + mp[5] * sq_relu_mlp(hp, ff_w1[i], ff_w2[i])
        xs = xs + ms[5] * sq_relu_mlp(hs, ff_w1[i], ff_w2[i])
    new_cache_k = jnp.stack(new_k, axis=1)
    new_cache_v = jnp.stack(new_v, axis=1)
    return (xp, xs, new_cache_k, new_cache_v)
```

```python
import functools

import jax
import jax.numpy as jnp
import numpy as np
from jax import lax
from jax.experimental import pallas as pl
from jax.experimental.pallas import tpu as pltpu

D_MODEL = 1024
BATCH = 16
SEQ = 256
DEPTH = 4
DEC_BATCH = 8
DEC_SEQ = 1024
PAST_LEN = 512
GRID_W = 64
GRID_ROWS = DEC_SEQ // GRID_W
CHUNK = 128
A_HALF = 2 * D_MODEL
A_GROUPS = 8
A_GROUP_W = A_HALF // A_GROUPS
N_HEADS = 16
HEAD_DIM = D_MODEL // N_HEADS
WIN_ROWS = 8
WIN_COLS = 16
RPB_ROWS = 2 * WIN_ROWS - 1
RPB_COLS = 2 * WIN_COLS - 1
D_FF = 4 * D_MODEL
EPS = 1e-6
ATT_SCALE = HEAD_DIM ** -0.5
MASK_VALUE = -1e30

P_TOK = BATCH * SEQ
S_TOK = DEC_BATCH * DEC_SEQ
N_TOK = P_TOK + S_TOK
COND_ROWS = 16
CTX_ROW = DEC_BATCH
LOAD_STEPS = 8
HEAD_PAIR = 2 * HEAD_DIM
N_PAIRS = N_HEADS // 2
BAND = WIN_ROWS * GRID_W

F32 = jnp.float32
BF16 = jnp.bfloat16
VMEM_LIMIT = 56 * 1024 * 1024


def _params(n_axes=1, vmem=VMEM_LIMIT):
    return pltpu.CompilerParams(dimension_semantics=("arbitrary",) * n_axes, vmem_limit_bytes=vmem)


def _dot(a, b):
    return jnp.dot(a, b, preferred_element_type=F32)


def _dot_nt(a, b):
    return lax.dot_general(a, b, (((1,), (1,)), ((), ())), preferred_element_type=F32)


def _rms(x, g):
    return x * lax.rsqrt(jnp.mean(x * x, axis=-1, keepdims=True) + EPS) * g


def _gelu(x):
    return 0.5 * x * (1.0 + jnp.tanh(0.7978845608028654 * (x + 0.044715 * (x * x * x))))


def _tok_spec(tm, width=D_MODEL):
    return pl.BlockSpec((tm, width), lambda s: (jnp.maximum(s - LOAD_STEPS, 0), 0))


def _mod_spec(tm):
    n_prompt_tiles = P_TOK // tm
    tiles_per_seq = DEC_SEQ // tm

    def index(s):
        t = jnp.maximum(s - LOAD_STEPS, 0)
        row = jnp.where(t < n_prompt_tiles, CTX_ROW, jnp.maximum(t - n_prompt_tiles, 0) // tiles_per_seq)
        return (row, 0, 0)

    return pl.BlockSpec((1, 6, D_MODEL), index)


def _chunk_spec(rows, cols):
    return pl.BlockSpec((rows // LOAD_STEPS, cols), lambda s: (jnp.minimum(s, LOAD_STEPS - 1), 0))


def _const_spec(shape):
    zeros = (0,) * len(shape)
    return pl.BlockSpec(shape, lambda s: zeros)


def _load_chunk(step, w_ref, w_scr):
    rows = w_ref.shape[0]
    off = pl.multiple_of(step * rows, rows)
    w_scr[pl.ds(off, rows), :] = w_ref[...].astype(BF16)


ADA_TN = 1024


def _ada_kernel(c_ref, w_ref, b_ref, o_ref):
    c = c_ref[...]
    a = c * (1.0 / (1.0 + jnp.exp(-c)))
    a_hi = a.astype(BF16)
    a_lo = (a - a_hi.astype(F32)).astype(BF16)
    w = w_ref[0]
    w_hi = w.astype(BF16)
    w_lo = (w - w_hi.astype(F32)).astype(BF16)
    o_ref[0] = _dot(a_hi, w_hi) + _dot(a_hi, w_lo) + _dot(a_lo, w_hi) + b_ref[0]


def _adaln(cond, ada_w, ada_b):
    n_out = 6 * D_MODEL
    return pl.pallas_call(
        _ada_kernel,
        grid=(DEPTH, n_out // ADA_TN),
        in_specs=[
            pl.BlockSpec((COND_ROWS, D_MODEL), lambda i, j: (0, 0)),
            pl.BlockSpec((1, D_MODEL, ADA_TN), lambda i, j: (i, 0, j)),
            pl.BlockSpec((1, 1, ADA_TN), lambda i, j: (i, 0, j)),
        ],
        out_specs=pl.BlockSpec((1, COND_ROWS, ADA_TN), lambda i, j: (i, 0, j)),
        out_shape=jax.ShapeDtypeStruct((DEPTH, COND_ROWS, n_out), F32),
        compiler_params=_params(2),
        name="adaln",
    )(cond, ada_w, ada_b.reshape(DEPTH, 1, n_out))


GMLP_TM = 256


def _gmlp_kernel(x_ref, mod_ref, g_ref, win_ref, vg_ref, ws_ref, bs_ref, wout_ref, o_ref,
                 win_s, wout_s, us_s):
    step = pl.program_id(0)

    @pl.when(step < LOAD_STEPS)
    def _():
        _load_chunk(step, win_ref, win_s)
        _load_chunk(step, wout_ref, wout_s)

    @pl.when(step >= LOAD_STEPS)
    def _():
        x = x_ref[...]
        mod = mod_ref[0]
        h = (_rms(x, g_ref[...]) * (1.0 + mod[1:2]) + mod[0:1]).astype(BF16)
        u = _gelu(_dot(h, win_s[:, :A_HALF]))
        v = _gelu(_dot(h, win_s[:, A_HALF:]))
        v = _rms(v, vg_ref[...]).astype(BF16)
        for g in range(A_GROUPS):
            cols = slice(g * A_GROUP_W, (g + 1) * A_GROUP_W)
            w_g = ws_ref[g].astype(BF16)
            b_g = bs_ref[g]
            for n in range(GMLP_TM // CHUNK):
                rows = slice(n * CHUNK, (n + 1) * CHUNK)
                s = _dot(w_g, v[rows, cols]) + b_g
                us_s[rows, cols] = (u[rows, cols] * s).astype(BF16)
        o_ref[...] = x + mod[2:3] * _dot(us_s[...], wout_s[...])


def _gmlp(x, mod, g, w_in, v_gain, ws, bs, w_out):
    tm = GMLP_TM
    return pl.pallas_call(
        _gmlp_kernel,
        grid=(LOAD_STEPS + N_TOK // tm,),
        in_specs=[
            _tok_spec(tm),
            _mod_spec(tm),
            _const_spec((1, D_MODEL)),
            _chunk_spec(D_MODEL, 2 * A_HALF),
            _const_spec((1, A_HALF)),
            _const_spec((A_GROUPS, CHUNK, CHUNK)),
            _const_spec((A_GROUPS, CHUNK, 1)),
            _chunk_spec(A_HALF, D_MODEL),
        ],
        out_specs=_tok_spec(tm),
        out_shape=jax.ShapeDtypeStruct((N_TOK, D_MODEL), F32),
        scratch_shapes=[
            pltpu.VMEM((D_MODEL, 2 * A_HALF), BF16),
            pltpu.VMEM((A_HALF, D_MODEL), BF16),
            pltpu.VMEM((tm, A_HALF), BF16),
        ],
        compiler_params=_params(),
        name="gmlp",
    )(x, mod, g.reshape(1, D_MODEL), w_in, v_gain.reshape(1, A_HALF), ws,
      bs.reshape(A_GROUPS, CHUNK, 1), w_out)


FFN_TM = 512
FFN_TC = 1024


def _ffn_kernel(x_ref, mod_ref, g_ref, w1_ref, w2_ref, o_ref, w1_s, w2_s):
    step = pl.program_id(0)

    @pl.when(step < LOAD_STEPS)
    def _():
        _load_chunk(step, w1_ref, w1_s)
        _load_chunk(step, w2_ref, w2_s)

    @pl.when(step >= LOAD_STEPS)
    def _():
        x = x_ref[...]
        mod = mod_ref[0]
        h = (_rms(x, g_ref[...]) * (1.0 + mod[4:5]) + mod[3:4]).astype(BF16)
        acc = jnp.zeros((FFN_TM, D_MODEL), F32)
        for c in range(D_FF // FFN_TC):
            cols = slice(c * FFN_TC, (c + 1) * FFN_TC)
            hid = jnp.square(jnp.maximum(_dot(h, w1_s[:, cols]), 0.0)).astype(BF16)
            acc = acc + _dot(hid, w2_s[cols, :])
        o_ref[...] = x + mod[5:6] * acc


def _ffn(x, mod, g, w1, w2):
    tm = FFN_TM
    return pl.pallas_call(
        _ffn_kernel,
        grid=(LOAD_STEPS + N_TOK // tm,),
        in_specs=[
            _tok_spec(tm),
            _mod_spec(tm),
            _const_spec((1, D_MODEL)),
            _chunk_spec(D_MODEL, D_FF),
            _chunk_spec(D_FF, D_MODEL),
        ],
        out_specs=_tok_spec(tm),
        out_shape=jax.ShapeDtypeStruct((N_TOK, D_MODEL), F32),
        scratch_shapes=[
            pltpu.VMEM((D_MODEL, D_FF), BF16),
            pltpu.VMEM((D_FF, D_MODEL), BF16),
        ],
        compiler_params=_params(),
        name="ffn",
    )(x, mod, g.reshape(1, D_MODEL), w1, w2)


CONV_TM = 1024
CONV_TC = 512


def _conv_kernel(x_ref, mod_ref, g_ref, win_ref, cw_ref, cb_ref, wout_ref, o_ref, win_s, wout_s):
    step = pl.program_id(0)

    @pl.when(step < LOAD_STEPS)
    def _():
        _load_chunk(step, win_ref, win_s)
        _load_chunk(step, wout_ref, wout_s)

    @pl.when(step >= LOAD_STEPS)
    def _():
        tile = step - LOAD_STEPS
        seq_len = jnp.where(tile < P_TOK // CONV_TM, SEQ, DEC_SEQ)
        pos = lax.broadcasted_iota(jnp.int32, (CONV_TM, 1), 0) & (seq_len - 1)
        has_prev = pos != 0
        has_next = pos != seq_len - 1
        x = x_ref[...]
        mod = mod_ref[0]
        h = (_rms(x, g_ref[...]) * (1.0 + mod[1:2]) + mod[0:1]).astype(BF16)
        cw = cw_ref[...]
        cb = cb_ref[...]
        acc = jnp.zeros((CONV_TM, D_MODEL), F32)
        for c in range(D_MODEL // CONV_TC):
            lo = c * CONV_TC
            cols = slice(lo, lo + CONV_TC)
            bg = _dot(h, win_s[:, lo:lo + CONV_TC])
            cg = _dot(h, win_s[:, D_MODEL + lo:D_MODEL + lo + CONV_TC])
            xt = _dot(h, win_s[:, 2 * D_MODEL + lo:2 * D_MODEL + lo + CONV_TC])
            z = cg * xt
            z_prev = jnp.where(has_prev, pltpu.roll(z, 1, axis=0), 0.0)
            z_next = jnp.where(has_next, pltpu.roll(z, CONV_TM - 1, axis=0), 0.0)
            zc = cw[0:1, cols] * z_prev + cw[1:2, cols] * z + cw[2:3, cols] * z_next + cb[:, cols]
            acc = acc + _dot((bg * zc).astype(BF16), wout_s[cols, :])
        o_ref[...] = x + mod[2:3] * acc


def _conv(x, mod, g, w_in, conv_w, conv_b, w_out):
    tm = CONV_TM
    return pl.pallas_call(
        _conv_kernel,
        grid=(LOAD_STEPS + N_TOK // tm,),
        in_specs=[
            _tok_spec(tm),
            _mod_spec(tm),
            _const_spec((1, D_MODEL)),
            _chunk_spec(D_MODEL, 3 * D_MODEL),
            _const_spec((3, D_MODEL)),
            _const_spec((1, D_MODEL)),
            _chunk_spec(D_MODEL, D_MODEL),
        ],
        out_specs=_tok_spec(tm),
        out_shape=jax.ShapeDtypeStruct((N_TOK, D_MODEL), F32),
        scratch_shapes=[
            pltpu.VMEM((D_MODEL, 3 * D_MODEL), BF16),
            pltpu.VMEM((D_MODEL, D_MODEL), BF16),
        ],
        compiler_params=_params(),
        name="sconv",
    )(x, mod, g.reshape(1, D_MODEL), w_in, conv_w, conv_b.reshape(1, D_MODEL), w_out)


QKV_TM = 512


def _qkv_kernel(x_ref, mod_ref, g_ref, w_ref, gq_ref, gk_ref, hm_ref,
                q_ref, k_ref, v_ref, kc_ref, vc_ref, w_s):
    step = pl.program_id(0)

    @pl.when(step < LOAD_STEPS)
    def _():
        _load_chunk(step, w_ref, w_s)

    @pl.when(step >= LOAD_STEPS)
    def _():
        x = x_ref[...]
        mod = mod_ref[0]
        h = (_rms(x, g_ref[...]) * (1.0 + mod[1:2]) + mod[0:1]).astype(BF16)
        head_mean = hm_ref[...]

        def head_norm(y, gain):
            ms = _dot((y * y).astype(BF16), head_mean)
            return y * lax.rsqrt(ms + EPS) * gain

        q = head_norm(_dot(h, w_s[:, :D_MODEL]), gq_ref[...])
        k = head_norm(_dot(h, w_s[:, D_MODEL:2 * D_MODEL]), gk_ref[...])
        v = _dot(h, w_s[:, 2 * D_MODEL:])
        q_ref[...] = q.astype(BF16)
        k_ref[...] = k.astype(BF16)
        v_ref[...] = v.astype(BF16)

        @pl.when(step - LOAD_STEPS < P_TOK // QKV_TM)
        def _():
            kc_ref[...] = k
            vc_ref[...] = v


def _qkv(x, mod, g, w_qkv, q_gain, k_gain):
    tm = QKV_TM
    n_prompt_tiles = P_TOK // tm
    head_mean = jnp.asarray(np.kron(np.eye(N_HEADS), np.full((HEAD_DIM, HEAD_DIM), 1.0 / HEAD_DIM)), BF16)
    cache_spec = pl.BlockSpec(
        (tm, D_MODEL), lambda s: (jnp.clip(s - LOAD_STEPS, 0, n_prompt_tiles - 1), 0))
    return pl.pallas_call(
        _qkv_kernel,
        grid=(LOAD_STEPS + N_TOK // tm,),
        in_specs=[
            _tok_spec(tm),
            _mod_spec(tm),
            _const_spec((1, D_MODEL)),
            _chunk_spec(D_MODEL, 3 * D_MODEL),
            _const_spec((1, D_MODEL)),
            _const_spec((1, D_MODEL)),
            _const_spec((D_MODEL, D_MODEL)),
        ],
        out_specs=[_tok_spec(tm), _tok_spec(tm), _tok_spec(tm), cache_spec, cache_spec],
        out_shape=[
            jax.ShapeDtypeStruct((N_TOK, D_MODEL), BF16),
            jax.ShapeDtypeStruct((N_TOK, D_MODEL), BF16),
            jax.ShapeDtypeStruct((N_TOK, D_MODEL), BF16),
            jax.ShapeDtypeStruct((P_TOK, D_MODEL), F32),
            jax.ShapeDtypeStruct((P_TOK, D_MODEL), F32),
        ],
        scratch_shapes=[pltpu.VMEM((D_MODEL, 3 * D_MODEL), BF16)],
        compiler_params=_params(),
        name="qkv",
    )(x, mod, g.reshape(1, D_MODEL), w_qkv,
      jnp.tile(q_gain, N_HEADS).reshape(1, D_MODEL), jnp.tile(k_gain, N_HEADS).reshape(1, D_MODEL),
      head_mean)


def _bias_kernel(rpb_ref, o_ref):
    h = pl.program_id(0)
    qc = lax.broadcasted_iota(jnp.int32, (GRID_W, GRID_W), 0)
    kc = lax.broadcasted_iota(jnp.int32, (GRID_W, GRID_W), 1)
    rel_c = jnp.clip(kc - qc + (WIN_COLS - 1), 0, RPB_COLS - 1)
    col_start = jnp.clip(qc - WIN_COLS // 2, 0, GRID_W - WIN_COLS)
    col_ok = (kc >= col_start) & (kc < col_start + WIN_COLS)
    base = h * (RPB_ROWS * RPB_COLS)
    row_tiles = []
    for rel_r in range(RPB_ROWS):
        tile = jnp.zeros((GRID_W, GRID_W), F32)
        for c in range(RPB_COLS):
            tile = jnp.where(rel_c == c, rpb_ref[base + rel_r * RPB_COLS + c], tile)
        row_tiles.append(jnp.where(col_ok, tile, MASK_VALUE))
    for shift in range(WIN_ROWS):
        for i in range(WIN_ROWS):
            o_ref[0, shift, :, i * GRID_W:(i + 1) * GRID_W] = row_tiles[i + WIN_ROWS - 1 - shift]


def _window_bias(rpb):
    return pl.pallas_call(
        _bias_kernel,
        grid=(N_HEADS,),
        in_specs=[pl.BlockSpec(memory_space=pltpu.SMEM)],
        out_specs=pl.BlockSpec((1, WIN_ROWS, GRID_W, BAND), lambda h: (h, 0, 0, 0)),
        out_shape=jax.ShapeDtypeStruct((N_HEADS, WIN_ROWS, GRID_W, BAND), F32),
        compiler_params=_params(),
        name="window_bias",
    )(rpb.reshape(N_HEADS * RPB_ROWS * RPB_COLS))


def _head_lane_mask(half):
    lane = lax.broadcasted_iota(jnp.int32, (1, HEAD_PAIR), 1)
    return (lane >= HEAD_DIM) if half else (lane < HEAD_DIM)


def _ctx_attn_kernel(q_ref, k_ref, v_ref, o_ref):
    q2 = q_ref[...]
    k2 = k_ref[...]
    v2 = v_ref[...]
    out = None
    for half in range(2):
        sel = _head_lane_mask(half)
        qm = jnp.where(sel, q2, jnp.zeros_like(q2))
        s = _dot_nt(qm, k2) * ATT_SCALE
        e = jnp.exp(s - jnp.max(s, axis=-1, keepdims=True))
        den = jnp.sum(e, axis=-1, keepdims=True)
        o = _dot(e.astype(BF16), v2) / den
        out = o if out is None else jnp.where(sel, o, out)
    o_ref[...] = out.astype(BF16)


def _ctx_attention(q, k, v):
    spec = pl.BlockSpec((SEQ, HEAD_PAIR), lambda p, b: (b, p))
    return pl.pallas_call(
        _ctx_attn_kernel,
        grid=(N_PAIRS, BATCH),
        in_specs=[spec, spec, spec],
        out_specs=spec,
        out_shape=jax.ShapeDtypeStruct((P_TOK, D_MODEL), BF16),
        compiler_params=_params(2),
        name="ctx_attention",
    )(q, k, v)


def _band_start(r):
    return min(max(r - WIN_ROWS // 2, 0), GRID_ROWS - WIN_ROWS)


def _nbr_attn_kernel(q_ref, k_ref, v_ref, ck_ref, cv_ref, bias_ref, o_ref):
    q2 = q_ref[...]
    ck2 = ck_ref[0].astype(BF16)
    cv2 = cv_ref[0].astype(BF16)
    for r in range(GRID_ROWS):
        rows = slice(r * GRID_W, (r + 1) * GRID_W)
        rs = _band_start(r)
        band = slice(rs * GRID_W, rs * GRID_W + BAND)
        k_band = k_ref[band, :]
        v_band = v_ref[band, :]
        out = None
        for half in range(2):
            sel = _head_lane_mask(half)
            qm = jnp.where(sel, q2[rows], jnp.zeros_like(q2[rows]))
            s_loc = _dot_nt(qm, k_band) * ATT_SCALE + bias_ref[half, r - rs]
            s_ctx = _dot_nt(qm, ck2) * ATT_SCALE
            m = jnp.maximum(jnp.max(s_loc, axis=-1, keepdims=True), jnp.max(s_ctx, axis=-1, keepdims=True))
            e_loc = jnp.exp(s_loc - m)
            e_ctx = jnp.exp(s_ctx - m)
            den = jnp.sum(e_loc, axis=-1, keepdims=True) + jnp.sum(e_ctx, axis=-1, keepdims=True)
            o = (_dot(e_loc.astype(BF16), v_band) + _dot(e_ctx.astype(BF16), cv2)) / den
            out = o if out is None else jnp.where(sel, o, out)
        o_ref[rows, :] = out.astype(BF16)


def _nbr_attention(q, k, v, ck, cv, bias):
    first = P_TOK // DEC_SEQ
    tok = pl.BlockSpec((DEC_SEQ, HEAD_PAIR), lambda p, b: (first + b, p))
    ctx = pl.BlockSpec((1, PAST_LEN, HEAD_PAIR), lambda p, b: (b, 0, p))
    return pl.pallas_call(
        _nbr_attn_kernel,
        grid=(N_PAIRS, DEC_BATCH),
        in_specs=[tok, tok, tok, ctx, ctx,
                  pl.BlockSpec((2, WIN_ROWS, GRID_W, BAND), lambda p, b: (p, 0, 0, 0))],
        out_specs=pl.BlockSpec((DEC_SEQ, HEAD_PAIR), lambda p, b: (b, p)),
        out_shape=jax.ShapeDtypeStruct((S_TOK, D_MODEL), BF16),
        compiler_params=_params(2),
        name="nbr_attention",
    )(q, k, v, ck, cv, bias)


PROJ_TM = 512


def _proj_kernel(x_ref, mod_ref, op_ref, os_ref, w_ref, o_ref, w_s):
    step = pl.program_id(0)

    @pl.when(step < LOAD_STEPS)
    def _():
        _load_chunk(step, w_ref, w_s)

    @pl.when(step >= LOAD_STEPS)
    def _():
        is_prompt = step - LOAD_STEPS < P_TOK // PROJ_TM
        a = jnp.where(is_prompt, op_ref[...], os_ref[...])
        o_ref[...] = x_ref[...] + mod_ref[0][2:3] * _dot(a, w_s[...])


def _attn_proj(x, mod, o_prompt, o_sample, w_o):
    tm = PROJ_TM
    n_prompt_tiles = P_TOK // tm
    n_sample_tiles = S_TOK // tm
    return pl.pallas_call(
        _proj_kernel,
        grid=(LOAD_STEPS + N_TOK // tm,),
        in_specs=[
            _tok_spec(tm),
            _mod_spec(tm),
            pl.BlockSpec((tm, D_MODEL), lambda s: (jnp.clip(s - LOAD_STEPS, 0, n_prompt_tiles - 1), 0)),
            pl.BlockSpec((tm, D_MODEL),
                         lambda s: (jnp.clip(s - LOAD_STEPS - n_prompt_tiles, 0, n_sample_tiles - 1), 0)),
            _chunk_spec(D_MODEL, D_MODEL),
        ],
        out_specs=_tok_spec(tm),
        out_shape=jax.ShapeDtypeStruct((N_TOK, D_MODEL), F32),
        scratch_shapes=[pltpu.VMEM((D_MODEL, D_MODEL), BF16)],
        compiler_params=_params(),
        name="attn_proj",
    )(x, mod, o_prompt, o_sample, w_o)


def kernel(x_prompt, x_sample, cache_k, cache_v, c, c_ctx, norm_g, ada_w, ada_b, a_w_in, a_v_gain, a_ws, a_bs, a_w_out, b_w_qkv, b_q_gain, b_k_gain, b_rpb, b_w_o, c_w_in, c_conv_w, c_conv_b, c_w_out, ff_w1, ff_w2):
    x = jnp.concatenate([x_prompt.reshape(P_TOK, D_MODEL), x_sample.reshape(S_TOK, D_MODEL)], axis=0)
    cond = jnp.concatenate(
        [c, c_ctx[None, :], jnp.zeros((COND_ROWS - DEC_BATCH - 1, D_MODEL), F32)], axis=0)
    mods = _adaln(cond, ada_w, ada_b).reshape(DEPTH, COND_ROWS, 6, D_MODEL)
    new_k, new_v = [], []
    for i in range(DEPTH):
        kind, j = i % 3, i // 3
        mod = mods[i]
        if kind == 0:
            x = _gmlp(x, mod, norm_g[i, 0], a_w_in[j], a_v_gain[j], a_ws[j], a_bs[j], a_w_out[j])
        elif kind == 1:
            q, k, v, k_new, v_new = _qkv(x, mod, norm_g[i, 0], b_w_qkv[j], b_q_gain[j], b_k_gain[j])
            new_k.append(k_new.reshape(BATCH, SEQ, N_HEADS, HEAD_DIM))
            new_v.append(v_new.reshape(BATCH, SEQ, N_HEADS, HEAD_DIM))
            o_prompt = _ctx_attention(q, k, v)
            bias = _window_bias(b_rpb[j])
            o_sample = _nbr_attention(
                q, k, v,
                cache_k[:, j].reshape(DEC_BATCH, PAST_LEN, D_MODEL),
                cache_v[:, j].reshape(DEC_BATCH, PAST_LEN, D_MODEL), bias)
            x = _attn_proj(x, mod, o_prompt, o_sample, b_w_o[j])
        else:
            x = _conv(x, mod, norm_g[i, 0], c_w_in[j], c_conv_w[j], c_conv_b[j], c_w_out[j])
        x = _ffn(x, mod, norm_g[i, 1], ff_w1[i], ff_w2[i])
    y_prompt = x[:P_TOK].reshape(BATCH, SEQ, D_MODEL)
    y_sample = x[P_TOK:].reshape(DEC_BATCH, DEC_SEQ, D_MODEL)
    return (y_prompt, y_sample, jnp.stack(new_k, axis=1), jnp.stack(new_v, axis=1))
```

```python
import jax
import jax.numpy as jnp
import numpy as np
from jax import lax
from jax.experimental import pallas as pl
from jax.experimental.pallas import tpu as pltpu

D_MODEL = 1024
BATCH = 16
SEQ = 256
DEPTH = 4
DEC_BATCH = 8
DEC_SEQ = 1024
PAST_LEN = 512
GRID_W = 64
GRID_ROWS = DEC_SEQ // GRID_W
CHUNK = 128
A_HALF = 2 * D_MODEL
A_GROUPS = 8
A_GROUP_W = A_HALF // A_GROUPS
N_HEADS = 16
HEAD_DIM = D_MODEL // N_HEADS
WIN_ROWS = 8
WIN_COLS = 16
RPB_ROWS = 2 * WIN_ROWS - 1
RPB_COLS = 2 * WIN_COLS - 1
D_FF = 4 * D_MODEL
EPS = 1e-6
ATT_SCALE = HEAD_DIM ** -0.5
MASK_VALUE = -1e30

P_TOK = BATCH * SEQ
S_TOK = DEC_BATCH * DEC_SEQ
N_TOK = P_TOK + S_TOK
COND_ROWS = 16
CTX_ROW = DEC_BATCH
LOAD_STEPS = 8
HEAD_PAIR = 2 * HEAD_DIM
N_PAIRS = N_HEADS // 2
BAND = WIN_ROWS * GRID_W

F32 = jnp.float32
BF16 = jnp.bfloat16
VMEM_LIMIT = 56 * 1024 * 1024


def _params(n_axes=1, vmem=VMEM_LIMIT):
    return pltpu.CompilerParams(dimension_semantics=("arbitrary",) * n_axes, vmem_limit_bytes=vmem)


def _dot(a, b):
    return jnp.dot(a, b, preferred_element_type=F32)


def _dot_nt(a, b):
    return lax.dot_general(a, b, (((1,), (1,)), ((), ())), preferred_element_type=F32)


def _rms(x, g):
    return x * lax.rsqrt(jnp.mean(x * x, axis=-1, keepdims=True) + EPS) * g


def _gelu(x):
    return 0.5 * x * (1.0 + jnp.tanh(0.7978845608028654 * (x + 0.044715 * (x * x * x))))


def _tok_spec(tm, width=D_MODEL):
    return pl.BlockSpec((tm, width), lambda s: (jnp.maximum(s - LOAD_STEPS, 0), 0))


def _prompt_tok_spec(tm):
    last = P_TOK // tm - 1
    return pl.BlockSpec((tm, D_MODEL), lambda s: (jnp.clip(s - LOAD_STEPS, 0, last), 0))


def _sample_tok_spec(tm):
    first, last = P_TOK // tm, S_TOK // tm - 1
    return pl.BlockSpec((tm, D_MODEL), lambda s: (jnp.clip(s - LOAD_STEPS - first, 0, last), 0))


def _is_prompt_tile(step, tm):
    return step - LOAD_STEPS < P_TOK // tm


def _mod_spec(layer, tm):
    n_prompt_tiles = P_TOK // tm
    tiles_per_seq = DEC_SEQ // tm

    def index(s):
        t = jnp.maximum(s - LOAD_STEPS, 0)
        row = jnp.where(t < n_prompt_tiles, CTX_ROW, jnp.maximum(t - n_prompt_tiles, 0) // tiles_per_seq)
        return (layer, row, 0, 0)

    return pl.BlockSpec((None, 1, 6, D_MODEL), index)


def _chunk_spec(layer, rows, cols):
    return pl.BlockSpec((None, rows // LOAD_STEPS, cols),
                        lambda s: (layer, jnp.minimum(s, LOAD_STEPS - 1), 0))


def _layer_spec(layer, shape):
    zeros = (0,) * len(shape)
    return pl.BlockSpec((None,) + tuple(shape), lambda s: (layer,) + zeros)


def _const_spec(shape):
    zeros = (0,) * len(shape)
    return pl.BlockSpec(shape, lambda s: zeros)


def _load_chunk(step, w_ref, w_scr):
    rows = w_ref.shape[0]
    off = pl.multiple_of(step * rows, rows)
    w_scr[pl.ds(off, rows), :] = w_ref[...].astype(BF16)


ADA_TN = 1024


def _ada_kernel(c_ref, w_ref, b_ref, o_ref):
    c = c_ref[...]
    a = c * (1.0 / (1.0 + jnp.exp(-c)))
    a_hi = a.astype(BF16)
    a_lo = (a - a_hi.astype(F32)).astype(BF16)
    w = w_ref[0]
    w_hi = w.astype(BF16)
    w_lo = (w - w_hi.astype(F32)).astype(BF16)
    o_ref[0] = _dot(a_hi, w_hi) + _dot(a_hi, w_lo) + _dot(a_lo, w_hi) + b_ref[0]


def _adaln(cond, ada_w, ada_b):
    n_out = 6 * D_MODEL
    return pl.pallas_call(
        _ada_kernel,
        grid=(DEPTH, n_out // ADA_TN),
        in_specs=[
            pl.BlockSpec((COND_ROWS, D_MODEL), lambda i, j: (0, 0)),
            pl.BlockSpec((1, D_MODEL, ADA_TN), lambda i, j: (i, 0, j)),
            pl.BlockSpec((1, 1, ADA_TN), lambda i, j: (i, 0, j)),
        ],
        out_specs=pl.BlockSpec((1, COND_ROWS, ADA_TN), lambda i, j: (i, 0, j)),
        out_shape=jax.ShapeDtypeStruct((DEPTH, COND_ROWS, n_out), F32),
        compiler_params=_params(2),
        name="adaln",
    )(cond, ada_w, ada_b.reshape(DEPTH, 1, n_out))


GMLP_TM = 256


def _gmlp_body(x, mod_ref, g_ref, vg_ref, ws_ref, bs_ref, o_ref, win_s, wout_s, us_s):
    mod = mod_ref[0]
    h = (_rms(x, g_ref[...]) * (1.0 + mod[1:2]) + mod[0:1]).astype(BF16)
    u = _gelu(_dot(h, win_s[:, :A_HALF]))
    v = _gelu(_dot(h, win_s[:, A_HALF:]))
    v = _rms(v, vg_ref[...]).astype(BF16)
    for g in range(A_GROUPS):
        cols = slice(g * A_GROUP_W, (g + 1) * A_GROUP_W)
        w_g = ws_ref[g].astype(BF16)
        b_g = bs_ref[g]
        for n in range(GMLP_TM // CHUNK):
            rows = slice(n * CHUNK, (n + 1) * CHUNK)
            s = _dot(w_g, v[rows, cols]) + b_g
            us_s[rows, cols] = (u[rows, cols] * s).astype(BF16)
    o_ref[...] = x + mod[2:3] * _dot(us_s[...], wout_s[...])


def _gmlp_kernel(x_ref, mod_ref, g_ref, win_ref, vg_ref, ws_ref, bs_ref, wout_ref, o_ref,
                 win_s, wout_s, us_s):
    step = pl.program_id(0)

    @pl.when(step < LOAD_STEPS)
    def _():
        _load_chunk(step, win_ref, win_s)
        _load_chunk(step, wout_ref, wout_s)

    @pl.when(step >= LOAD_STEPS)
    def _():
        _gmlp_body(x_ref[...], mod_ref, g_ref, vg_ref, ws_ref, bs_ref, o_ref, win_s, wout_s, us_s)


def _gmlp_split_in_kernel(xp_ref, xs_ref, mod_ref, g_ref, win_ref, vg_ref, ws_ref, bs_ref, wout_ref,
                          o_ref, win_s, wout_s, us_s):
    step = pl.program_id(0)

    @pl.when(step < LOAD_STEPS)
    def _():
        _load_chunk(step, win_ref, win_s)
        _load_chunk(step, wout_ref, wout_s)

    @pl.when(step >= LOAD_STEPS)
    def _():
        x = jnp.where(_is_prompt_tile(step, GMLP_TM), xp_ref[...], xs_ref[...])
        _gmlp_body(x, mod_ref, g_ref, vg_ref, ws_ref, bs_ref, o_ref, win_s, wout_s, us_s)


def _gmlp(xs, mods, layer, norm_g, j, w_in, v_gain, ws, bs, w_out):
    tm = GMLP_TM
    split_in = isinstance(xs, tuple)
    x_specs = [_prompt_tok_spec(tm), _sample_tok_spec(tm)] if split_in else [_tok_spec(tm)]
    x_args = list(xs) if split_in else [xs]
    return pl.pallas_call(
        _gmlp_split_in_kernel if split_in else _gmlp_kernel,
        grid=(LOAD_STEPS + N_TOK // tm,),
        in_specs=x_specs + [
            _mod_spec(layer, tm),
            _layer_spec(2 * layer, (1, D_MODEL)),
            _chunk_spec(j, D_MODEL, 2 * A_HALF),
            _layer_spec(j, (1, A_HALF)),
            _layer_spec(j, (A_GROUPS, CHUNK, CHUNK)),
            _layer_spec(j, (A_GROUPS, CHUNK, 1)),
            _chunk_spec(j, A_HALF, D_MODEL),
        ],
        out_specs=_tok_spec(tm),
        out_shape=jax.ShapeDtypeStruct((N_TOK, D_MODEL), F32),
        scratch_shapes=[
            pltpu.VMEM((D_MODEL, 2 * A_HALF), BF16),
            pltpu.VMEM((A_HALF, D_MODEL), BF16),
            pltpu.VMEM((tm, A_HALF), BF16),
        ],
        compiler_params=_params(),
        name="gmlp",
    )(*x_args, mods, norm_g, w_in, v_gain, ws, bs, w_out)


FFN_TM = 512
FFN_TC = 1024


def _ffn_body(x_ref, mod_ref, g_ref, w1_s, w2_s):
    x = x_ref[...]
    mod = mod_ref[0]
    h = (_rms(x, g_ref[...]) * (1.0 + mod[4:5]) + mod[3:4]).astype(BF16)
    acc = jnp.zeros((FFN_TM, D_MODEL), F32)
    for c in range(D_FF // FFN_TC):
        cols = slice(c * FFN_TC, (c + 1) * FFN_TC)
        hid = jnp.square(jnp.maximum(_dot(h, w1_s[:, cols]), 0.0)).astype(BF16)
        acc = acc + _dot(hid, w2_s[cols, :])
    return x + mod[5:6] * acc


def _ffn_kernel(x_ref, mod_ref, g_ref, w1_ref, w2_ref, o_ref, w1_s, w2_s):
    step = pl.program_id(0)

    @pl.when(step < LOAD_STEPS)
    def _():
        _load_chunk(step, w1_ref, w1_s)
        _load_chunk(step, w2_ref, w2_s)

    @pl.when(step >= LOAD_STEPS)
    def _():
        o_ref[...] = _ffn_body(x_ref, mod_ref, g_ref, w1_s, w2_s)


def _ffn_split_out_kernel(x_ref, mod_ref, g_ref, w1_ref, w2_ref, op_ref, os_ref, w1_s, w2_s):
    step = pl.program_id(0)

    @pl.when(step < LOAD_STEPS)
    def _():
        _load_chunk(step, w1_ref, w1_s)
        _load_chunk(step, w2_ref, w2_s)

    is_prompt = _is_prompt_tile(step, FFN_TM)

    @pl.when((step >= LOAD_STEPS) & is_prompt)
    def _():
        op_ref[...] = _ffn_body(x_ref, mod_ref, g_ref, w1_s, w2_s)

    @pl.when(jnp.logical_not(is_prompt))
    def _():
        os_ref[...] = _ffn_body(x_ref, mod_ref, g_ref, w1_s, w2_s)


def _ffn(x, mods, layer, norm_g, w1, w2, split_out=False):
    tm = FFN_TM
    if split_out:
        out_specs = [_prompt_tok_spec(tm), _sample_tok_spec(tm)]
        out_shape = [jax.ShapeDtypeStruct((P_TOK, D_MODEL), F32), jax.ShapeDtypeStruct((S_TOK, D_MODEL), F32)]
    else:
        out_specs = _tok_spec(tm)
        out_shape = jax.ShapeDtypeStruct((N_TOK, D_MODEL), F32)
    return pl.pallas_call(
        _ffn_split_out_kernel if split_out else _ffn_kernel,
        grid=(LOAD_STEPS + N_TOK // tm,),
        in_specs=[
            _tok_spec(tm),
            _mod_spec(layer, tm),
            _layer_spec(2 * layer + 1, (1, D_MODEL)),
            _chunk_spec(layer, D_MODEL, D_FF),
            _chunk_spec(layer, D_FF, D_MODEL),
        ],
        out_specs=out_specs,
        out_shape=out_shape,
        scratch_shapes=[
            pltpu.VMEM((D_MODEL, D_FF), BF16),
            pltpu.VMEM((D_FF, D_MODEL), BF16),
        ],
        compiler_params=_params(),
        name="ffn",
    )(x, mods, norm_g, w1, w2)


CONV_TM = 1024
CONV_TC = 512


def _conv_kernel(x_ref, mod_ref, g_ref, win_ref, cw_ref, cb_ref, wout_ref, o_ref, win_s, wout_s):
    step = pl.program_id(0)

    @pl.when(step < LOAD_STEPS)
    def _():
        _load_chunk(step, win_ref, win_s)
        _load_chunk(step, wout_ref, wout_s)

    @pl.when(step >= LOAD_STEPS)
    def _():
        seq_len = jnp.where(_is_prompt_tile(step, CONV_TM), SEQ, DEC_SEQ)
        pos = lax.broadcasted_iota(jnp.int32, (CONV_TM, 1), 0) & (seq_len - 1)
        has_prev = pos != 0
        has_next = pos != seq_len - 1
        x = x_ref[...]
        mod = mod_ref[0]
        h = (_rms(x, g_ref[...]) * (1.0 + mod[1:2]) + mod[0:1]).astype(BF16)
        cw = cw_ref[...]
        cb = cb_ref[...]
        acc = jnp.zeros((CONV_TM, D_MODEL), F32)
        for c in range(D_MODEL // CONV_TC):
            lo = c * CONV_TC
            cols = slice(lo, lo + CONV_TC)
            bg = _dot(h, win_s[:, lo:lo + CONV_TC])
            cg = _dot(h, win_s[:, D_MODEL + lo:D_MODEL + lo + CONV_TC])
            xt = _dot(h, win_s[:, 2 * D_MODEL + lo:2 * D_MODEL + lo + CONV_TC])
            z = cg * xt
            z_prev = jnp.where(has_prev, pltpu.roll(z, 1, axis=0), 0.0)
            z_next = jnp.where(has_next, pltpu.roll(z, CONV_TM - 1, axis=0), 0.0)
            zc = cw[0:1, cols] * z_prev + cw[1:2, cols] * z + cw[2:3, cols] * z_next + cb[:, cols]
            acc = acc + _dot((bg * zc).astype(BF16), wout_s[cols, :])
        o_ref[...] = x + mod[2:3] * acc


def _conv(x, mods, layer, norm_g, j, w_in, conv_w, conv_b, w_out):
    tm = CONV_TM
    return pl.pallas_call(
        _conv_kernel,
        grid=(LOAD_STEPS + N_TOK // tm,),
        in_specs=[
            _tok_spec(tm),
            _mod_spec(layer, tm),
            _layer_spec(2 * layer, (1, D_MODEL)),
            _chunk_spec(j, D_MODEL, 3 * D_MODEL),
            _layer_spec(j, (3, D_MODEL)),
            _layer_spec(j, (1, D_MODEL)),
            _chunk_spec(j, D_MODEL, D_MODEL),
        ],
        out_specs=_tok_spec(tm),
        out_shape=jax.ShapeDtypeStruct((N_TOK, D_MODEL), F32),
        scratch_shapes=[
            pltpu.VMEM((D_MODEL, 3 * D_MODEL), BF16),
            pltpu.VMEM((D_MODEL, D_MODEL), BF16),
        ],
        compiler_params=_params(),
        name="sconv",
    )(x, mods, norm_g, w_in, conv_w, conv_b, w_out)


QKV_TM = 512


def _qkv_kernel(x_ref, mod_ref, g_ref, w_ref, gq_ref, gk_ref, hm_ref,
                q_ref, k_ref, v_ref, kc_ref, vc_ref, w_s):
    step = pl.program_id(0)

    @pl.when(step < LOAD_STEPS)
    def _():
        _load_chunk(step, w_ref, w_s)

    @pl.when(step >= LOAD_STEPS)
    def _():
        x = x_ref[...]
        mod = mod_ref[0]
        h = (_rms(x, g_ref[...]) * (1.0 + mod[1:2]) + mod[0:1]).astype(BF16)
        head_mean = hm_ref[...]

        def head_norm(y, gain):
            ms = _dot((y * y).astype(BF16), head_mean)
            return y * lax.rsqrt(ms + EPS) * gain

        q = head_norm(_dot(h, w_s[:, :D_MODEL]), gq_ref[...])
        k = head_norm(_dot(h, w_s[:, D_MODEL:2 * D_MODEL]), gk_ref[...])
        v = _dot(h, w_s[:, 2 * D_MODEL:])
        q_ref[...] = q.astype(BF16)
        k_ref[...] = k.astype(BF16)
        v_ref[...] = v.astype(BF16)

        @pl.when(_is_prompt_tile(step, QKV_TM))
        def _():
            kc_ref[...] = k
            vc_ref[...] = v


def _qkv(x, mods, layer, norm_g, j, w_qkv, q_gain, k_gain):
    tm = QKV_TM
    head_mean = jnp.asarray(np.kron(np.eye(N_HEADS), np.full((HEAD_DIM, HEAD_DIM), 1.0 / HEAD_DIM)), BF16)
    return pl.pallas_call(
        _qkv_kernel,
        grid=(LOAD_STEPS + N_TOK // tm,),
        in_specs=[
            _tok_spec(tm),
            _mod_spec(layer, tm),
            _layer_spec(2 * layer, (1, D_MODEL)),
            _chunk_spec(j, D_MODEL, 3 * D_MODEL),
            _const_spec((1, D_MODEL)),
            _const_spec((1, D_MODEL)),
            _const_spec((D_MODEL, D_MODEL)),
        ],
        out_specs=[_tok_spec(tm), _tok_spec(tm), _tok_spec(tm), _prompt_tok_spec(tm), _prompt_tok_spec(tm)],
        out_shape=[
            jax.ShapeDtypeStruct((N_TOK, D_MODEL), BF16),
            jax.ShapeDtypeStruct((N_TOK, D_MODEL), BF16),
            jax.ShapeDtypeStruct((N_TOK, D_MODEL), BF16),
            jax.ShapeDtypeStruct((P_TOK, D_MODEL), F32),
            jax.ShapeDtypeStruct((P_TOK, D_MODEL), F32),
        ],
        scratch_shapes=[pltpu.VMEM((D_MODEL, 3 * D_MODEL), BF16)],
        compiler_params=_params(),
        name="qkv",
    )(x, mods, norm_g, w_qkv,
      jnp.tile(q_gain[j], N_HEADS).reshape(1, D_MODEL), jnp.tile(k_gain[j], N_HEADS).reshape(1, D_MODEL),
      head_mean)


def _bias_kernel(rpb_ref, o_ref):
    h = pl.program_id(0)
    qc = lax.broadcasted_iota(jnp.int32, (GRID_W, GRID_W), 0)
    kc = lax.broadcasted_iota(jnp.int32, (GRID_W, GRID_W), 1)
    rel_c = jnp.clip(kc - qc + (WIN_COLS - 1), 0, RPB_COLS - 1)
    col_start = jnp.clip(qc - WIN_COLS // 2, 0, GRID_W - WIN_COLS)
    col_ok = (kc >= col_start) & (kc < col_start + WIN_COLS)
    base = h * (RPB_ROWS * RPB_COLS)
    row_tiles = []
    for rel_r in range(RPB_ROWS):
        tile = jnp.zeros((GRID_W, GRID_W), F32)
        for c in range(RPB_COLS):
            tile = jnp.where(rel_c == c, rpb_ref[base + rel_r * RPB_COLS + c], tile)
        row_tiles.append(jnp.where(col_ok, tile, MASK_VALUE))
    for shift in range(WIN_ROWS):
        for i in range(WIN_ROWS):
            o_ref[0, shift, :, i * GRID_W:(i + 1) * GRID_W] = row_tiles[i + WIN_ROWS - 1 - shift]


def _window_bias(rpb):
    return pl.pallas_call(
        _bias_kernel,
        grid=(N_HEADS,),
        in_specs=[pl.BlockSpec(memory_space=pltpu.SMEM)],
        out_specs=pl.BlockSpec((1, WIN_ROWS, GRID_W, BAND), lambda h: (h, 0, 0, 0)),
        out_shape=jax.ShapeDtypeStruct((N_HEADS, WIN_ROWS, GRID_W, BAND), F32),
        compiler_params=_params(),
        name="window_bias",
    )(rpb.reshape(N_HEADS * RPB_ROWS * RPB_COLS))


def _head_lane_mask(half):
    lane = lax.broadcasted_iota(jnp.int32, (1, HEAD_PAIR), 1)
    return (lane >= HEAD_DIM) if half else (lane < HEAD_DIM)


def _ctx_attn_kernel(q_ref, k_ref, v_ref, o_ref):
    q2 = q_ref[...]
    k2 = k_ref[...]
    v2 = v_ref[...]
    out = None
    for half in range(2):
        sel = _head_lane_mask(half)
        qm = jnp.where(sel, q2, jnp.zeros_like(q2))
        s = _dot_nt(qm, k2) * ATT_SCALE
        e = jnp.exp(s - jnp.max(s, axis=-1, keepdims=True))
        den = jnp.sum(e, axis=-1, keepdims=True)
        o = _dot(e.astype(BF16), v2) / den
        out = o if out is None else jnp.where(sel, o, out)
    o_ref[...] = out.astype(BF16)


def _ctx_attention(q, k, v):
    spec = pl.BlockSpec((SEQ, HEAD_PAIR), lambda p, b: (b, p))
    return pl.pallas_call(
        _ctx_attn_kernel,
        grid=(N_PAIRS, BATCH),
        in_specs=[spec, spec, spec],
        out_specs=spec,
        out_shape=jax.ShapeDtypeStruct((P_TOK, D_MODEL), BF16),
        compiler_params=_params(2),
        name="ctx_attention",
    )(q, k, v)


def _band_start(r):
    return min(max(r - WIN_ROWS // 2, 0), GRID_ROWS - WIN_ROWS)


def _nbr_attn_kernel(q_ref, k_ref, v_ref, ck_ref, cv_ref, bias_ref, o_ref):
    q2 = q_ref[...]
    ck2 = ck_ref[0].astype(BF16)
    cv2 = cv_ref[0].astype(BF16)
    for r in range(GRID_ROWS):
        rows = slice(r * GRID_W, (r + 1) * GRID_W)
        rs = _band_start(r)
        band = slice(rs * GRID_W, rs * GRID_W + BAND)
        k_band = k_ref[band, :]
        v_band = v_ref[band, :]
        out = None
        for half in range(2):
            sel = _head_lane_mask(half)
            qm = jnp.where(sel, q2[rows], jnp.zeros_like(q2[rows]))
            s_loc = _dot_nt(qm, k_band) * ATT_SCALE + bias_ref[half, r - rs]
            s_ctx = _dot_nt(qm, ck2) * ATT_SCALE
            m = jnp.maximum(jnp.max(s_loc, axis=-1, keepdims=True), jnp.max(s_ctx, axis=-1, keepdims=True))
            e_loc = jnp.exp(s_loc - m)
            e_ctx = jnp.exp(s_ctx - m)
            den = jnp.sum(e_loc, axis=-1, keepdims=True) + jnp.sum(e_ctx, axis=-1, keepdims=True)
            o = (_dot(e_loc.astype(BF16), v_band) + _dot(e_ctx.astype(BF16), cv2)) / den
            out = o if out is None else jnp.where(sel, o, out)
        o_ref[rows, :] = out.astype(BF16)


def _nbr_attention(q, k, v, ck, cv, bias):
    first = P_TOK // DEC_SEQ
    tok = pl.BlockSpec((DEC_SEQ, HEAD_PAIR), lambda p, b: (first + b, p))
    ctx = pl.BlockSpec((1, PAST_LEN, HEAD_PAIR), lambda p, b: (b, 0, p))
    return pl.pallas_call(
        _nbr_attn_kernel,
        grid=(N_PAIRS, DEC_BATCH),
        in_specs=[tok, tok, tok, ctx, ctx,
                  pl.BlockSpec((2, WIN_ROWS, GRID_W, BAND), lambda p, b: (p, 0, 0, 0))],
        out_specs=pl.BlockSpec((DEC_SEQ, HEAD_PAIR), lambda p, b: (b, p)),
        out_shape=jax.ShapeDtypeStruct((S_TOK, D_MODEL), BF16),
        compiler_params=_params(2),
        name="nbr_attention",
    )(q, k, v, ck, cv, bias)


PROJ_TM = 512


def _proj_kernel(x_ref, mod_ref, op_ref, os_ref, w_ref, o_ref, w_s):
    step = pl.program_id(0)

    @pl.when(step < LOAD_STEPS)
    def _():
        _load_chunk(step, w_ref, w_s)

    @pl.when(step >= LOAD_STEPS)
    def _():
        a = jnp.where(_is_prompt_tile(step, PROJ_TM), op_ref[...], os_ref[...])
        o_ref[...] = x_ref[...] + mod_ref[0][2:3] * _dot(a, w_s[...])


def _attn_proj(x, mods, layer, o_prompt, o_sample, j, w_o):
    tm = PROJ_TM
    return pl.pallas_call(
        _proj_kernel,
        grid=(LOAD_STEPS + N_TOK // tm,),
        in_specs=[
            _tok_spec(tm),
            _mod_spec(layer, tm),
            _prompt_tok_spec(tm),
            _sample_tok_spec(tm),
            _chunk_spec(j, D_MODEL, D_MODEL),
        ],
        out_specs=_tok_spec(tm),
        out_shape=jax.ShapeDtypeStruct((N_TOK, D_MODEL), F32),
        scratch_shapes=[pltpu.VMEM((D_MODEL, D_MODEL), BF16)],
        compiler_params=_params(),
        name="attn_proj",
    )(x, mods, o_prompt, o_sample, w_o)


def kernel(x_prompt, x_sample, cache_k, cache_v, c, c_ctx, norm_g, ada_w, ada_b, a_w_in, a_v_gain, a_ws, a_bs, a_w_out, b_w_qkv, b_q_gain, b_k_gain, b_rpb, b_w_o, c_w_in, c_conv_w, c_conv_b, c_w_out, ff_w1, ff_w2):
    n_a = a_w_in.shape[0]
    cond = jnp.concatenate(
        [c, c_ctx[None, :], jnp.zeros((COND_ROWS - DEC_BATCH - 1, D_MODEL), F32)], axis=0)
    mods = _adaln(cond, ada_w, ada_b).reshape(DEPTH, COND_ROWS, 6, D_MODEL)
    norm_g = norm_g.reshape(2 * DEPTH, 1, D_MODEL)
    a_v_gain = a_v_gain.reshape(n_a, 1, A_HALF)
    a_bs = a_bs.reshape(n_a, A_GROUPS, CHUNK, 1)
    c_conv_b = c_conv_b.reshape(-1, 1, D_MODEL)
    x = (x_prompt.reshape(P_TOK, D_MODEL), x_sample.reshape(S_TOK, D_MODEL))
    new_k, new_v = [], []
    for i in range(DEPTH):
        kind, j = i % 3, i // 3
        if kind == 0:
            x = _gmlp(x, mods, i, norm_g, j, a_w_in, a_v_gain, a_ws, a_bs, a_w_out)
        elif kind == 1:
            q, k, v, k_new, v_new = _qkv(x, mods, i, norm_g, j, b_w_qkv, b_q_gain, b_k_gain)
            new_k.append(k_new.reshape(BATCH, SEQ, N_HEADS, HEAD_DIM))
            new_v.append(v_new.reshape(BATCH, SEQ, N_HEADS, HEAD_DIM))
            o_prompt = _ctx_attention(q, k, v)
            bias = _window_bias(b_rpb[j])
            o_sample = _nbr_attention(
                q, k, v,
                cache_k[:, j].reshape(DEC_BATCH, PAST_LEN, D_MODEL),
                cache_v[:, j].reshape(DEC_BATCH, PAST_LEN, D_MODEL), bias)
            x = _attn_proj(x, mods, i, o_prompt, o_sample, j, b_w_o)
        else:
            x = _conv(x, mods, i, norm_g, j, c_w_in, c_conv_w, c_conv_b, c_w_out)
        x = _ffn(x, mods, i, norm_g, ff_w1, ff_w2, split_out=(i == DEPTH - 1))
    y_prompt = x[0].reshape(BATCH, SEQ, D_MODEL)
    y_sample = x[1].reshape(DEC_BATCH, DEC_SEQ, D_MODEL)
    return (y_prompt, y_sample, jnp.stack(new_k, axis=1), jnp.stack(new_v, axis=1))
```

```python
import jax
import jax.numpy as jnp
import numpy as np
from jax import lax
from jax.experimental import pallas as pl
from jax.experimental.pallas import tpu as pltpu

D_MODEL = 1024
BATCH = 16
SEQ = 256
DEPTH = 4
DEC_BATCH = 8
DEC_SEQ = 1024
PAST_LEN = 512
GRID_W = 64
GRID_ROWS = DEC_SEQ // GRID_W
CHUNK = 128
A_HALF = 2 * D_MODEL
A_GROUPS = 8
A_GROUP_W = A_HALF // A_GROUPS
N_HEADS = 16
HEAD_DIM = D_MODEL // N_HEADS
WIN_ROWS = 8
WIN_COLS = 16
RPB_ROWS = 2 * WIN_ROWS - 1
RPB_COLS = 2 * WIN_COLS - 1
D_FF = 4 * D_MODEL
EPS = 1e-6
ATT_SCALE = HEAD_DIM ** -0.5
MASK_VALUE = -1e30

P_TOK = BATCH * SEQ
S_TOK = DEC_BATCH * DEC_SEQ
N_TOK = P_TOK + S_TOK
COND_ROWS = 16
CTX_ROW = DEC_BATCH
LOAD_STEPS = 8
HEAD_PAIR = 2 * HEAD_DIM
N_PAIRS = N_HEADS // 2
BAND = WIN_ROWS * GRID_W

F32 = jnp.float32
BF16 = jnp.bfloat16
VMEM_LIMIT = 56 * 1024 * 1024


def _params(n_axes=1, vmem=VMEM_LIMIT):
    return pltpu.CompilerParams(dimension_semantics=("arbitrary",) * n_axes, vmem_limit_bytes=vmem)


def _dot(a, b):
    return jnp.dot(a, b, preferred_element_type=F32)


def _dot_nt(a, b):
    return lax.dot_general(a, b, (((1,), (1,)), ((), ())), preferred_element_type=F32)


def _rms(x, g):
    return x * lax.rsqrt(jnp.mean(x * x, axis=-1, keepdims=True) + EPS) * g


def _gelu(x):
    return 0.5 * x * (1.0 + jnp.tanh(0.7978845608028654 * (x + 0.044715 * (x * x * x))))


def _tok_spec(tm, width=D_MODEL):
    return pl.BlockSpec((tm, width), lambda s: (jnp.maximum(s - LOAD_STEPS, 0), 0))


def _prompt_tok_spec(tm):
    last = P_TOK // tm - 1
    return pl.BlockSpec((tm, D_MODEL), lambda s: (jnp.clip(s - LOAD_STEPS, 0, last), 0))


def _sample_tok_spec(tm):
    first, last = P_TOK // tm, S_TOK // tm - 1
    return pl.BlockSpec((tm, D_MODEL), lambda s: (jnp.clip(s - LOAD_STEPS - first, 0, last), 0))


def _is_prompt_tile(step, tm):
    return step - LOAD_STEPS < P_TOK // tm


def _mod_spec(layer, tm):
    n_prompt_tiles = P_TOK // tm
    tiles_per_seq = DEC_SEQ // tm

    def index(s):
        t = jnp.maximum(s - LOAD_STEPS, 0)
        row = jnp.where(t < n_prompt_tiles, CTX_ROW, jnp.maximum(t - n_prompt_tiles, 0) // tiles_per_seq)
        return (layer, row, 0, 0)

    return pl.BlockSpec((None, 1, 6, D_MODEL), index)


def _chunk_spec(layer, rows, cols):
    return pl.BlockSpec((None, rows // LOAD_STEPS, cols),
                        lambda s: (layer, jnp.minimum(s, LOAD_STEPS - 1), 0))


def _layer_spec(layer, shape):
    zeros = (0,) * len(shape)
    return pl.BlockSpec((None,) + tuple(shape), lambda s: (layer,) + zeros)


def _const_spec(shape):
    zeros = (0,) * len(shape)
    return pl.BlockSpec(shape, lambda s: zeros)


def _load_chunk(step, w_ref, w_scr):
    rows = w_ref.shape[0]
    off = pl.multiple_of(step * rows, rows)
    w_scr[pl.ds(off, rows), :] = w_ref[...].astype(BF16)


ADA_TN = 1024


def _ada_kernel(c_ref, w_ref, b_ref, o_ref):
    c = c_ref[...]
    a = c * (1.0 / (1.0 + jnp.exp(-c)))
    a_hi = a.astype(BF16)
    a_lo = (a - a_hi.astype(F32)).astype(BF16)
    w = w_ref[0]
    w_hi = w.astype(BF16)
    w_lo = (w - w_hi.astype(F32)).astype(BF16)
    o_ref[0] = _dot(a_hi, w_hi) + _dot(a_hi, w_lo) + _dot(a_lo, w_hi) + b_ref[0]


def _adaln(cond, ada_w, ada_b):
    n_out = 6 * D_MODEL
    return pl.pallas_call(
        _ada_kernel,
        grid=(DEPTH, n_out // ADA_TN),
        in_specs=[
            pl.BlockSpec((COND_ROWS, D_MODEL), lambda i, j: (0, 0)),
            pl.BlockSpec((1, D_MODEL, ADA_TN), lambda i, j: (i, 0, j)),
            pl.BlockSpec((1, 1, ADA_TN), lambda i, j: (i, 0, j)),
        ],
        out_specs=pl.BlockSpec((1, COND_ROWS, ADA_TN), lambda i, j: (i, 0, j)),
        out_shape=jax.ShapeDtypeStruct((DEPTH, COND_ROWS, n_out), F32),
        compiler_params=_params(2),
        name="adaln",
    )(cond, ada_w, ada_b.reshape(DEPTH, 1, n_out))


GMLP_TM = 256


def _gmlp_body(x, mod_ref, g_ref, vg_ref, ws_ref, bs_ref, o_ref, win_s, wout_s, us_s):
    mod = mod_ref[0]
    h = (_rms(x, g_ref[...]) * (1.0 + mod[1:2]) + mod[0:1]).astype(BF16)
    u = _gelu(_dot(h, win_s[:, :A_HALF]))
    v = _gelu(_dot(h, win_s[:, A_HALF:]))
    v = _rms(v, vg_ref[...]).astype(BF16)
    for g in range(A_GROUPS):
        cols = slice(g * A_GROUP_W, (g + 1) * A_GROUP_W)
        w_g = ws_ref[g].astype(BF16)
        b_g = bs_ref[g]
        for n in range(GMLP_TM // CHUNK):
            rows = slice(n * CHUNK, (n + 1) * CHUNK)
            s = _dot(w_g, v[rows, cols]) + b_g
            us_s[rows, cols] = (u[rows, cols] * s).astype(BF16)
    o_ref[...] = x + mod[2:3] * _dot(us_s[...], wout_s[...])


def _gmlp_kernel(x_ref, mod_ref, g_ref, win_ref, vg_ref, ws_ref, bs_ref, wout_ref, o_ref,
                 win_s, wout_s, us_s):
    step = pl.program_id(0)

    @pl.when(step < LOAD_STEPS)
    def _():
        _load_chunk(step, win_ref, win_s)
        _load_chunk(step, wout_ref, wout_s)

    @pl.when(step >= LOAD_STEPS)
    def _():
        _gmlp_body(x_ref[...], mod_ref, g_ref, vg_ref, ws_ref, bs_ref, o_ref, win_s, wout_s, us_s)


def _gmlp_split_in_kernel(xp_ref, xs_ref, mod_ref, g_ref, win_ref, vg_ref, ws_ref, bs_ref, wout_ref,
                          o_ref, win_s, wout_s, us_s):
    step = pl.program_id(0)

    @pl.when(step < LOAD_STEPS)
    def _():
        _load_chunk(step, win_ref, win_s)
        _load_chunk(step, wout_ref, wout_s)

    @pl.when(step >= LOAD_STEPS)
    def _():
        x = jnp.where(_is_prompt_tile(step, GMLP_TM), xp_ref[...], xs_ref[...])
        _gmlp_body(x, mod_ref, g_ref, vg_ref, ws_ref, bs_ref, o_ref, win_s, wout_s, us_s)


def _gmlp(xs, mods, layer, norm_g, j, w_in, v_gain, ws, bs, w_out):
    tm = GMLP_TM
    split_in = isinstance(xs, tuple)
    x_specs = [_prompt_tok_spec(tm), _sample_tok_spec(tm)] if split_in else [_tok_spec(tm)]
    x_args = list(xs) if split_in else [xs]
    return pl.pallas_call(
        _gmlp_split_in_kernel if split_in else _gmlp_kernel,
        grid=(LOAD_STEPS + N_TOK // tm,),
        in_specs=x_specs + [
            _mod_spec(layer, tm),
            _layer_spec(2 * layer, (1, D_MODEL)),
            _chunk_spec(j, D_MODEL, 2 * A_HALF),
            _layer_spec(j, (1, A_HALF)),
            _layer_spec(j, (A_GROUPS, CHUNK, CHUNK)),
            _layer_spec(j, (A_GROUPS, CHUNK, 1)),
            _chunk_spec(j, A_HALF, D_MODEL),
        ],
        out_specs=_tok_spec(tm),
        out_shape=jax.ShapeDtypeStruct((N_TOK, D_MODEL), F32),
        scratch_shapes=[
            pltpu.VMEM((D_MODEL, 2 * A_HALF), BF16),
            pltpu.VMEM((A_HALF, D_MODEL), BF16),
            pltpu.VMEM((tm, A_HALF), BF16),
        ],
        compiler_params=_params(),
        name="gmlp",
    )(*x_args, mods, norm_g, w_in, v_gain, ws, bs, w_out)


FFN_TM = 512
FFN_TC = 1024


def _ffn_body(x_ref, mod_ref, g_ref, w1_s, w2_s):
    x = x_ref[...]
    mod = mod_ref[0]
    h = (_rms(x, g_ref[...]) * (1.0 + mod[4:5]) + mod[3:4]).astype(BF16)
    acc = jnp.zeros((FFN_TM, D_MODEL), F32)
    for c in range(D_FF // FFN_TC):
        cols = slice(c * FFN_TC, (c + 1) * FFN_TC)
        hid = jnp.square(jnp.maximum(_dot(h, w1_s[:, cols]), 0.0)).astype(BF16)
        acc = acc + _dot(hid, w2_s[cols, :])
    return x + mod[5:6] * acc


def _ffn_kernel(x_ref, mod_ref, g_ref, w1_ref, w2_ref, o_ref, w1_s, w2_s):
    step = pl.program_id(0)

    @pl.when(step < LOAD_STEPS)
    def _():
        _load_chunk(step, w1_ref, w1_s)
        _load_chunk(step, w2_ref, w2_s)

    @pl.when(step >= LOAD_STEPS)
    def _():
        o_ref[...] = _ffn_body(x_ref, mod_ref, g_ref, w1_s, w2_s)


def _ffn_split_out_kernel(x_ref, mod_ref, g_ref, w1_ref, w2_ref, op_ref, os_ref, w1_s, w2_s):
    step = pl.program_id(0)

    @pl.when(step < LOAD_STEPS)
    def _():
        _load_chunk(step, w1_ref, w1_s)
        _load_chunk(step, w2_ref, w2_s)

    is_prompt = _is_prompt_tile(step, FFN_TM)

    @pl.when((step >= LOAD_STEPS) & is_prompt)
    def _():
        op_ref[...] = _ffn_body(x_ref, mod_ref, g_ref, w1_s, w2_s)

    @pl.when(jnp.logical_not(is_prompt))
    def _():
        os_ref[...] = _ffn_body(x_ref, mod_ref, g_ref, w1_s, w2_s)


def _ffn(x, mods, layer, norm_g, w1, w2, split_out=False):
    tm = FFN_TM
    if split_out:
        out_specs = [_prompt_tok_spec(tm), _sample_tok_spec(tm)]
        out_shape = [jax.ShapeDtypeStruct((P_TOK, D_MODEL), F32), jax.ShapeDtypeStruct((S_TOK, D_MODEL), F32)]
    else:
        out_specs = _tok_spec(tm)
        out_shape = jax.ShapeDtypeStruct((N_TOK, D_MODEL), F32)
    return pl.pallas_call(
        _ffn_split_out_kernel if split_out else _ffn_kernel,
        grid=(LOAD_STEPS + N_TOK // tm,),
        in_specs=[
            _tok_spec(tm),
            _mod_spec(layer, tm),
            _layer_spec(2 * layer + 1, (1, D_MODEL)),
            _chunk_spec(layer, D_MODEL, D_FF),
            _chunk_spec(layer, D_FF, D_MODEL),
        ],
        out_specs=out_specs,
        out_shape=out_shape,
        scratch_shapes=[
            pltpu.VMEM((D_MODEL, D_FF), BF16),
            pltpu.VMEM((D_FF, D_MODEL), BF16),
        ],
        compiler_params=_params(),
        name="ffn",
    )(x, mods, norm_g, w1, w2)


CONV_TM = 1024
CONV_TC = 512


def _conv_kernel(x_ref, mod_ref, g_ref, win_ref, cw_ref, cb_ref, wout_ref, o_ref, win_s, wout_s):
    step = pl.program_id(0)

    @pl.when(step < LOAD_STEPS)
    def _():
        _load_chunk(step, win_ref, win_s)
        _load_chunk(step, wout_ref, wout_s)

    @pl.when(step >= LOAD_STEPS)
    def _():
        seq_len = jnp.where(_is_prompt_tile(step, CONV_TM), SEQ, DEC_SEQ)
        pos = lax.broadcasted_iota(jnp.int32, (CONV_TM, 1), 0) & (seq_len - 1)
        has_prev = pos != 0
        has_next = pos != seq_len - 1
        x = x_ref[...]
        mod = mod_ref[0]
        h = (_rms(x, g_ref[...]) * (1.0 + mod[1:2]) + mod[0:1]).astype(BF16)
        cw = cw_ref[...]
        cb = cb_ref[...]
        acc = jnp.zeros((CONV_TM, D_MODEL), F32)
        for c in range(D_MODEL // CONV_TC):
            lo = c * CONV_TC
            cols = slice(lo, lo + CONV_TC)
            bg = _dot(h, win_s[:, lo:lo + CONV_TC])
            cg = _dot(h, win_s[:, D_MODEL + lo:D_MODEL + lo + CONV_TC])
            xt = _dot(h, win_s[:, 2 * D_MODEL + lo:2 * D_MODEL + lo + CONV_TC])
            z = cg * xt
            z_prev = jnp.where(has_prev, pltpu.roll(z, 1, axis=0), 0.0)
            z_next = jnp.where(has_next, pltpu.roll(z, CONV_TM - 1, axis=0), 0.0)
            zc = cw[0:1, cols] * z_prev + cw[1:2, cols] * z + cw[2:3, cols] * z_next + cb[:, cols]
            acc = acc + _dot((bg * zc).astype(BF16), wout_s[cols, :])
        o_ref[...] = x + mod[2:3] * acc


def _conv(x, mods, layer, norm_g, j, w_in, conv_w, conv_b, w_out):
    tm = CONV_TM
    return pl.pallas_call(
        _conv_kernel,
        grid=(LOAD_STEPS + N_TOK // tm,),
        in_specs=[
            _tok_spec(tm),
            _mod_spec(layer, tm),
            _layer_spec(2 * layer, (1, D_MODEL)),
            _chunk_spec(j, D_MODEL, 3 * D_MODEL),
            _layer_spec(j, (3, D_MODEL)),
            _layer_spec(j, (1, D_MODEL)),
            _chunk_spec(j, D_MODEL, D_MODEL),
        ],
        out_specs=_tok_spec(tm),
        out_shape=jax.ShapeDtypeStruct((N_TOK, D_MODEL), F32),
        scratch_shapes=[
            pltpu.VMEM((D_MODEL, 3 * D_MODEL), BF16),
            pltpu.VMEM((D_MODEL, D_MODEL), BF16),
        ],
        compiler_params=_params(),
        name="sconv",
    )(x, mods, norm_g, w_in, conv_w, conv_b, w_out)


QKV_TM = 512


def _qkv_kernel(x_ref, mod_ref, g_ref, w_ref, gq_ref, gk_ref, hm_ref,
                q_ref, k_ref, v_ref, kc_ref, vc_ref, w_s):
    step = pl.program_id(0)

    @pl.when(step < LOAD_STEPS)
    def _():
        _load_chunk(step, w_ref, w_s)

    @pl.when(step >= LOAD_STEPS)
    def _():
        x = x_ref[...]
        mod = mod_ref[0]
        h = (_rms(x, g_ref[...]) * (1.0 + mod[1:2]) + mod[0:1]).astype(BF16)
        head_mean = hm_ref[...]

        def head_norm(y, gain):
            ms = _dot((y * y).astype(BF16), head_mean)
            return y * lax.rsqrt(ms + EPS) * gain

        q = head_norm(_dot(h, w_s[:, :D_MODEL]), gq_ref[...])
        k = head_norm(_dot(h, w_s[:, D_MODEL:2 * D_MODEL]), gk_ref[...])
        v = _dot(h, w_s[:, 2 * D_MODEL:])
        q_ref[...] = q.astype(BF16)
        k_ref[...] = k.astype(BF16)
        v_ref[...] = v.astype(BF16)

        @pl.when(_is_prompt_tile(step, QKV_TM))
        def _():
            kc_ref[...] = k
            vc_ref[...] = v


def _qkv(x, mods, layer, norm_g, j, w_qkv, q_gain, k_gain):
    tm = QKV_TM
    head_mean = jnp.asarray(np.kron(np.eye(N_HEADS), np.full((HEAD_DIM, HEAD_DIM), 1.0 / HEAD_DIM)), BF16)
    return pl.pallas_call(
        _qkv_kernel,
        grid=(LOAD_STEPS + N_TOK // tm,),
        in_specs=[
            _tok_spec(tm),
            _mod_spec(layer, tm),
            _layer_spec(2 * layer, (1, D_MODEL)),
            _chunk_spec(j, D_MODEL, 3 * D_MODEL),
            _const_spec((1, D_MODEL)),
            _const_spec((1, D_MODEL)),
            _const_spec((D_MODEL, D_MODEL)),
        ],
        out_specs=[_tok_spec(tm), _tok_spec(tm), _tok_spec(tm), _prompt_tok_spec(tm), _prompt_tok_spec(tm)],
        out_shape=[
            jax.ShapeDtypeStruct((N_TOK, D_MODEL), BF16),
            jax.ShapeDtypeStruct((N_TOK, D_MODEL), BF16),
            jax.ShapeDtypeStruct((N_TOK, D_MODEL), BF16),
            jax.ShapeDtypeStruct((P_TOK, D_MODEL), F32),
            jax.ShapeDtypeStruct((P_TOK, D_MODEL), F32),
        ],
        scratch_shapes=[pltpu.VMEM((D_MODEL, 3 * D_MODEL), BF16)],
        compiler_params=_params(),
        name="qkv",
    )(x, mods, norm_g, w_qkv,
      jnp.tile(q_gain[j], N_HEADS).reshape(1, D_MODEL), jnp.tile(k_gain[j], N_HEADS).reshape(1, D_MODEL),
      head_mean)


NBR_GROUP_ROWS = GRID_ROWS // 2
NBR_KEY_ROWS = NBR_GROUP_ROWS + WIN_ROWS // 2
NBR_Q = NBR_GROUP_ROWS * GRID_W
NBR_KEYS = NBR_KEY_ROWS * GRID_W


def _band_start(r):
    return min(max(r - WIN_ROWS // 2, 0), GRID_ROWS - WIN_ROWS)


def _key_row_start(r):
    return 0 if r < NBR_GROUP_ROWS else GRID_ROWS - NBR_KEY_ROWS


def _bias_kernel(rpb_ref, o_ref):
    h = pl.program_id(0)
    qc = lax.broadcasted_iota(jnp.int32, (GRID_W, GRID_W), 0)
    kc = lax.broadcasted_iota(jnp.int32, (GRID_W, GRID_W), 1)
    rel_c = jnp.clip(kc - qc + (WIN_COLS - 1), 0, RPB_COLS - 1)
    col_start = jnp.clip(qc - WIN_COLS // 2, 0, GRID_W - WIN_COLS)
    col_ok = (kc >= col_start) & (kc < col_start + WIN_COLS)
    base = h * (RPB_ROWS * RPB_COLS)
    row_tiles = []
    for rel_r in range(RPB_ROWS):
        tile = jnp.zeros((GRID_W, GRID_W), F32)
        for c in range(RPB_COLS):
            tile = jnp.where(rel_c == c, rpb_ref[base + rel_r * RPB_COLS + c], tile)
        row_tiles.append(jnp.where(col_ok, tile, MASK_VALUE))
    masked = jnp.full((GRID_W, GRID_W), MASK_VALUE, F32)
    for r in range(GRID_ROWS):
        for i in range(NBR_KEY_ROWS):
            key_row = _key_row_start(r) + i
            in_window = _band_start(r) <= key_row < _band_start(r) + WIN_ROWS
            tile = row_tiles[key_row - r + WIN_ROWS - 1] if in_window else masked
            o_ref[0, r, :, i * GRID_W:(i + 1) * GRID_W] = tile


def _window_bias(rpb):
    return pl.pallas_call(
        _bias_kernel,
        grid=(N_HEADS,),
        in_specs=[pl.BlockSpec(memory_space=pltpu.SMEM)],
        out_specs=pl.BlockSpec((1, GRID_ROWS, GRID_W, NBR_KEYS), lambda h: (h, 0, 0, 0)),
        out_shape=jax.ShapeDtypeStruct((N_HEADS, GRID_ROWS, GRID_W, NBR_KEYS), F32),
        compiler_params=_params(),
        name="window_bias",
    )(rpb.reshape(N_HEADS * RPB_ROWS * RPB_COLS))


def _head_lane_mask(half):
    lane = lax.broadcasted_iota(jnp.int32, (1, HEAD_PAIR), 1)
    return (lane >= HEAD_DIM) if half else (lane < HEAD_DIM)


def _scaled_head_queries(q2, sel):
    return jnp.where(sel, q2 * ATT_SCALE, jnp.zeros_like(q2))


def _ctx_attn_kernel(q_ref, k_ref, v_ref, o_ref):
    for p in range(N_PAIRS):
        cols = slice(p * HEAD_PAIR, (p + 1) * HEAD_PAIR)
        q2 = q_ref[:, cols]
        k2 = k_ref[:, cols]
        v2 = v_ref[:, cols]
        out = None
        for half in range(2):
            sel = _head_lane_mask(half)
            s = _dot_nt(_scaled_head_queries(q2, sel), k2)
            e = jnp.exp(s - jnp.max(s, axis=-1, keepdims=True))
            den = jnp.sum(e, axis=-1, keepdims=True)
            o = _dot(e.astype(BF16), v2) / den
            out = o if out is None else jnp.where(sel, o, out)
        o_ref[:, cols] = out.astype(BF16)


def _ctx_attention(q, k, v):
    spec = pl.BlockSpec((SEQ, D_MODEL), lambda b: (b, 0))
    return pl.pallas_call(
        _ctx_attn_kernel,
        grid=(BATCH,),
        in_specs=[spec, spec, spec],
        out_specs=spec,
        out_shape=jax.ShapeDtypeStruct((P_TOK, D_MODEL), BF16),
        compiler_params=_params(),
        name="ctx_attention",
    )(q, k, v)


def _nbr_attn_kernel(q_ref, k_ref, v_ref, ck_ref, cv_ref, bias_ref, o_ref):
    ck2 = ck_ref[0].astype(BF16)
    cv2 = cv_ref[0].astype(BF16)
    for g in range(GRID_ROWS // NBR_GROUP_ROWS):
        r0 = g * NBR_GROUP_ROWS
        rows = slice(r0 * GRID_W, r0 * GRID_W + NBR_Q)
        key_lo = _key_row_start(r0) * GRID_W
        q2 = q_ref[rows, :]
        k_band = k_ref[key_lo:key_lo + NBR_KEYS, :]
        v_band = v_ref[key_lo:key_lo + NBR_KEYS, :]
        out = None
        for half in range(2):
            sel = _head_lane_mask(half)
            qm = _scaled_head_queries(q2, sel)
            bias = bias_ref[half, r0:r0 + NBR_GROUP_ROWS].reshape(NBR_Q, NBR_KEYS)
            s_loc = _dot_nt(qm, k_band) + bias
            s_ctx = _dot_nt(qm, ck2)
            m = jnp.maximum(jnp.max(s_loc, axis=-1, keepdims=True), jnp.max(s_ctx, axis=-1, keepdims=True))
            e_loc = jnp.exp(s_loc - m)
            e_ctx = jnp.exp(s_ctx - m)
            den = jnp.sum(e_loc, axis=-1, keepdims=True) + jnp.sum(e_ctx, axis=-1, keepdims=True)
            o = (_dot(e_loc.astype(BF16), v_band) + _dot(e_ctx.astype(BF16), cv2)) / den
            out = o if out is None else jnp.where(sel, o, out)
        o_ref[rows, :] = out.astype(BF16)


def _nbr_attention(q, k, v, ck, cv, bias):
    first = P_TOK // DEC_SEQ
    tok = pl.BlockSpec((DEC_SEQ, HEAD_PAIR), lambda p, b: (first + b, p))
    ctx = pl.BlockSpec((1, PAST_LEN, HEAD_PAIR), lambda p, b: (b, 0, p))
    return pl.pallas_call(
        _nbr_attn_kernel,
        grid=(N_PAIRS, DEC_BATCH),
        in_specs=[tok, tok, tok, ctx, ctx,
                  pl.BlockSpec((2, GRID_ROWS, GRID_W, NBR_KEYS), lambda p, b: (p, 0, 0, 0))],
        out_specs=pl.BlockSpec((DEC_SEQ, HEAD_PAIR), lambda p, b: (b, p)),
        out_shape=jax.ShapeDtypeStruct((S_TOK, D_MODEL), BF16),
        compiler_params=_params(2),
        name="nbr_attention",
    )(q, k, v, ck, cv, bias)


PROJ_TM = 512


def _proj_kernel(x_ref, mod_ref, op_ref, os_ref, w_ref, o_ref, w_s):
    step = pl.program_id(0)

    @pl.when(step < LOAD_STEPS)
    def _():
        _load_chunk(step, w_ref, w_s)

    @pl.when(step >= LOAD_STEPS)
    def _():
        a = jnp.where(_is_prompt_tile(step, PROJ_TM), op_ref[...], os_ref[...])
        o_ref[...] = x_ref[...] + mod_ref[0][2:3] * _dot(a, w_s[...])


def _attn_proj(x, mods, layer, o_prompt, o_sample, j, w_o):
    tm = PROJ_TM
    return pl.pallas_call(
        _proj_kernel,
        grid=(LOAD_STEPS + N_TOK // tm,),
        in_specs=[
            _tok_spec(tm),
            _mod_spec(layer, tm),
            _prompt_tok_spec(tm),
            _sample_tok_spec(tm),
            _chunk_spec(j, D_MODEL, D_MODEL),
        ],
        out_specs=_tok_spec(tm),
        out_shape=jax.ShapeDtypeStruct((N_TOK, D_MODEL), F32),
        scratch_shapes=[pltpu.VMEM((D_MODEL, D_MODEL), BF16)],
        compiler_params=_params(),
        name="attn_proj",
    )(x, mods, o_prompt, o_sample, w_o)


def kernel(x_prompt, x_sample, cache_k, cache_v, c, c_ctx, norm_g, ada_w, ada_b, a_w_in, a_v_gain, a_ws, a_bs, a_w_out, b_w_qkv, b_q_gain, b_k_gain, b_rpb, b_w_o, c_w_in, c_conv_w, c_conv_b, c_w_out, ff_w1, ff_w2):
    n_a = a_w_in.shape[0]
    cond = jnp.concatenate(
        [c, c_ctx[None, :], jnp.zeros((COND_ROWS - DEC_BATCH - 1, D_MODEL), F32)], axis=0)
    mods = _adaln(cond, ada_w, ada_b).reshape(DEPTH, COND_ROWS, 6, D_MODEL)
    norm_g = norm_g.reshape(2 * DEPTH, 1, D_MODEL)
    a_v_gain = a_v_gain.reshape(n_a, 1, A_HALF)
    a_bs = a_bs.reshape(n_a, A_GROUPS, CHUNK, 1)
    c_conv_b = c_conv_b.reshape(-1, 1, D_MODEL)
    x = (x_prompt.reshape(P_TOK, D_MODEL), x_sample.reshape(S_TOK, D_MODEL))
    new_k, new_v = [], []
    for i in range(DEPTH):
        kind, j = i % 3, i // 3
        if kind == 0:
            x = _gmlp(x, mods, i, norm_g, j, a_w_in, a_v_gain, a_ws, a_bs, a_w_out)
        elif kind == 1:
            q, k, v, k_new, v_new = _qkv(x, mods, i, norm_g, j, b_w_qkv, b_q_gain, b_k_gain)
            new_k.append(k_new.reshape(BATCH, SEQ, N_HEADS, HEAD_DIM))
            new_v.append(v_new.reshape(BATCH, SEQ, N_HEADS, HEAD_DIM))
            o_prompt = _ctx_attention(q, k, v)
            bias = _window_bias(b_rpb[j])
            o_sample = _nbr_attention(
                q, k, v,
                cache_k[:, j].reshape(DEC_BATCH, PAST_LEN, D_MODEL),
                cache_v[:, j].reshape(DEC_BATCH, PAST_LEN, D_MODEL), bias)
            x = _attn_proj(x, mods, i, o_prompt, o_sample, j, b_w_o)
        else:
            x = _conv(x, mods, i, norm_g, j, c_w_in, c_conv_w, c_conv_b, c_w_out)
        x = _ffn(x, mods, i, norm_g, ff_w1, ff_w2, split_out=(i == DEPTH - 1))
    y_prompt = x[0].reshape(BATCH, SEQ, D_MODEL)
    y_sample = x[1].reshape(DEC_BATCH, DEC_SEQ, D_MODEL)
    return (y_prompt, y_sample, jnp.stack(new_k, axis=1), jnp.stack(new_v, axis=1))
```

```python
import jax
import jax.numpy as jnp
import numpy as np
from jax import lax
from jax.experimental import pallas as pl
from jax.experimental.pallas import tpu as pltpu

D_MODEL = 1024
BATCH = 16
SEQ = 256
DEPTH = 4
DEC_BATCH = 8
DEC_SEQ = 1024
PAST_LEN = 512
GRID_W = 64
GRID_ROWS = DEC_SEQ // GRID_W
CHUNK = 128
A_HALF = 2 * D_MODEL
A_GROUPS = 8
A_GROUP_W = A_HALF // A_GROUPS
N_HEADS = 16
HEAD_DIM = D_MODEL // N_HEADS
WIN_ROWS = 8
WIN_COLS = 16
RPB_ROWS = 2 * WIN_ROWS - 1
RPB_COLS = 2 * WIN_COLS - 1
D_FF = 4 * D_MODEL
EPS = 1e-6
ATT_SCALE = HEAD_DIM ** -0.5
MASK_VALUE = -1e30

P_TOK = BATCH * SEQ
S_TOK = DEC_BATCH * DEC_SEQ
N_TOK = P_TOK + S_TOK
COND_ROWS = 16
CTX_ROW = DEC_BATCH
LOAD_STEPS = 8
HEAD_PAIR = 2 * HEAD_DIM
N_PAIRS = N_HEADS // 2
MXU_TILE = 256

F32 = jnp.float32
BF16 = jnp.bfloat16
VMEM_LIMIT = 56 * 1024 * 1024


def _params(n_axes=1, vmem=VMEM_LIMIT):
    return pltpu.CompilerParams(dimension_semantics=("arbitrary",) * n_axes, vmem_limit_bytes=vmem)


def _dot(a, b):
    return jnp.dot(a, b, preferred_element_type=F32)


def _dot_nt(a, b):
    return lax.dot_general(a, b, (((1,), (1,)), ((), ())), preferred_element_type=F32)


def _rms(x, g):
    return x * lax.rsqrt(jnp.mean(x * x, axis=-1, keepdims=True) + EPS) * g


def _gelu(x):
    return 0.5 * x * (1.0 + jnp.tanh(0.7978845608028654 * (x + 0.044715 * (x * x * x))))


def _tok_spec(tm, width=D_MODEL):
    return pl.BlockSpec((tm, width), lambda s: (jnp.maximum(s - LOAD_STEPS, 0), 0))


def _prompt_tok_spec(tm):
    last = P_TOK // tm - 1
    return pl.BlockSpec((tm, D_MODEL), lambda s: (jnp.clip(s - LOAD_STEPS, 0, last), 0))


def _sample_tok_spec(tm):
    first, last = P_TOK // tm, S_TOK // tm - 1
    return pl.BlockSpec((tm, D_MODEL), lambda s: (jnp.clip(s - LOAD_STEPS - first, 0, last), 0))


def _is_prompt_tile(step, tm):
    return step - LOAD_STEPS < P_TOK // tm


def _mod_spec(layer, tm):
    n_prompt_tiles = P_TOK // tm
    tiles_per_seq = DEC_SEQ // tm

    def index(s):
        t = jnp.maximum(s - LOAD_STEPS, 0)
        row = jnp.where(t < n_prompt_tiles, CTX_ROW, jnp.maximum(t - n_prompt_tiles, 0) // tiles_per_seq)
        return (layer, row, 0, 0)

    return pl.BlockSpec((None, 1, 6, D_MODEL), index)


def _chunk_spec(layer, rows, cols):
    return pl.BlockSpec((None, rows // LOAD_STEPS, cols),
                        lambda s: (layer, jnp.minimum(s, LOAD_STEPS - 1), 0))


def _layer_spec(layer, shape):
    zeros = (0,) * len(shape)
    return pl.BlockSpec((None,) + tuple(shape), lambda s: (layer,) + zeros)


def _const_spec(shape):
    zeros = (0,) * len(shape)
    return pl.BlockSpec(shape, lambda s: zeros)


def _load_chunk(step, w_ref, w_scr):
    rows = w_ref.shape[0]
    off = pl.multiple_of(step * rows, rows)
    w_scr[pl.ds(off, rows), :] = w_ref[...].astype(BF16)


ADA_TN = 1024


def _ada_kernel(c_ref, w_ref, b_ref, o_ref):
    c = c_ref[...]
    a = c * (1.0 / (1.0 + jnp.exp(-c)))
    a_hi = a.astype(BF16)
    a_lo = (a - a_hi.astype(F32)).astype(BF16)
    w = w_ref[0]
    w_hi = w.astype(BF16)
    w_lo = (w - w_hi.astype(F32)).astype(BF16)
    o_ref[0] = _dot(a_hi, w_hi) + _dot(a_hi, w_lo) + _dot(a_lo, w_hi) + b_ref[0]


def _adaln(cond, ada_w, ada_b):
    n_out = 6 * D_MODEL
    return pl.pallas_call(
        _ada_kernel,
        grid=(DEPTH, n_out // ADA_TN),
        in_specs=[
            pl.BlockSpec((COND_ROWS, D_MODEL), lambda i, j: (0, 0)),
            pl.BlockSpec((1, D_MODEL, ADA_TN), lambda i, j: (i, 0, j)),
            pl.BlockSpec((1, 1, ADA_TN), lambda i, j: (i, 0, j)),
        ],
        out_specs=pl.BlockSpec((1, COND_ROWS, ADA_TN), lambda i, j: (i, 0, j)),
        out_shape=jax.ShapeDtypeStruct((DEPTH, COND_ROWS, n_out), F32),
        compiler_params=_params(2),
        name="adaln",
    )(cond, ada_w, ada_b.reshape(DEPTH, 1, n_out))


GMLP_TM = 512


def _gmlp_body(x, mod_ref, g_ref, vg_ref, ws_ref, bs_ref, o_ref, win_s, wout_s, us_s):
    mod = mod_ref[0]
    h = (_rms(x, g_ref[...]) * (1.0 + mod[1:2]) + mod[0:1]).astype(BF16)
    u = _gelu(_dot(h, win_s[:, :A_HALF]))
    v = _gelu(_dot(h, win_s[:, A_HALF:]))
    v = _rms(v, vg_ref[...]).astype(BF16)
    for g in range(A_GROUPS):
        cols = slice(g * A_GROUP_W, (g + 1) * A_GROUP_W)
        w_g = ws_ref[g].astype(BF16)
        b_g = bs_ref[g]
        for n in range(GMLP_TM // CHUNK):
            rows = slice(n * CHUNK, (n + 1) * CHUNK)
            s = _dot(w_g, v[rows, cols]) + b_g
            us_s[rows, cols] = (u[rows, cols] * s).astype(BF16)
    o_ref[...] = x + mod[2:3] * _dot(us_s[...], wout_s[...])


def _gmlp_kernel(x_ref, mod_ref, g_ref, win_ref, vg_ref, ws_ref, bs_ref, wout_ref, o_ref,
                 win_s, wout_s, us_s):
    step = pl.program_id(0)

    @pl.when(step < LOAD_STEPS)
    def _():
        _load_chunk(step, win_ref, win_s)
        _load_chunk(step, wout_ref, wout_s)

    @pl.when(step >= LOAD_STEPS)
    def _():
        _gmlp_body(x_ref[...], mod_ref, g_ref, vg_ref, ws_ref, bs_ref, o_ref, win_s, wout_s, us_s)


def _gmlp_split_in_kernel(xp_ref, xs_ref, mod_ref, g_ref, win_ref, vg_ref, ws_ref, bs_ref, wout_ref,
                          o_ref, win_s, wout_s, us_s):
    step = pl.program_id(0)

    @pl.when(step < LOAD_STEPS)
    def _():
        _load_chunk(step, win_ref, win_s)
        _load_chunk(step, wout_ref, wout_s)

    @pl.when(step >= LOAD_STEPS)
    def _():
        x = jnp.where(_is_prompt_tile(step, GMLP_TM), xp_ref[...], xs_ref[...])
        _gmlp_body(x, mod_ref, g_ref, vg_ref, ws_ref, bs_ref, o_ref, win_s, wout_s, us_s)


def _gmlp(xs, mods, layer, norm_g, j, w_in, v_gain, ws, bs, w_out):
    tm = GMLP_TM
    split_in = isinstance(xs, tuple)
    x_specs = [_prompt_tok_spec(tm), _sample_tok_spec(tm)] if split_in else [_tok_spec(tm)]
    x_args = list(xs) if split_in else [xs]
    return pl.pallas_call(
        _gmlp_split_in_kernel if split_in else _gmlp_kernel,
        grid=(LOAD_STEPS + N_TOK // tm,),
        in_specs=x_specs + [
            _mod_spec(layer, tm),
            _layer_spec(2 * layer, (1, D_MODEL)),
            _chunk_spec(j, D_MODEL, 2 * A_HALF),
            _layer_spec(j, (1, A_HALF)),
            _layer_spec(j, (A_GROUPS, CHUNK, CHUNK)),
            _layer_spec(j, (A_GROUPS, CHUNK, 1)),
            _chunk_spec(j, A_HALF, D_MODEL),
        ],
        out_specs=_tok_spec(tm),
        out_shape=jax.ShapeDtypeStruct((N_TOK, D_MODEL), F32),
        scratch_shapes=[
            pltpu.VMEM((D_MODEL, 2 * A_HALF), BF16),
            pltpu.VMEM((A_HALF, D_MODEL), BF16),
            pltpu.VMEM((tm, A_HALF), BF16),
        ],
        compiler_params=_params(),
        name="gmlp",
    )(*x_args, mods, norm_g, w_in, v_gain, ws, bs, w_out)


FFN_TM = 512
FFN_TC = 1024


def _ffn_body(x, mod_ref, g_ref, w1_s, w2_s):
    mod = mod_ref[0]
    h = (_rms(x, g_ref[...]) * (1.0 + mod[4:5]) + mod[3:4]).astype(BF16)
    acc = jnp.zeros((FFN_TM, D_MODEL), F32)
    for c in range(D_FF // FFN_TC):
        cols = slice(c * FFN_TC, (c + 1) * FFN_TC)
        hid = jnp.square(jnp.maximum(_dot(h, w1_s[:, cols]), 0.0)).astype(BF16)
        acc = acc + _dot(hid, w2_s[cols, :])
    return x + mod[5:6] * acc


def _ffn_kernel(x_ref, mod_ref, g_ref, w1_ref, w2_ref, o_ref, w1_s, w2_s):
    step = pl.program_id(0)

    @pl.when(step < LOAD_STEPS)
    def _():
        _load_chunk(step, w1_ref, w1_s)
        _load_chunk(step, w2_ref, w2_s)

    @pl.when(step >= LOAD_STEPS)
    def _():
        o_ref[...] = _ffn_body(x_ref[...], mod_ref, g_ref, w1_s, w2_s)


def _ffn_split_out_kernel(x_ref, mod_ref, g_ref, w1_ref, w2_ref, op_ref, os_ref, w1_s, w2_s):
    step = pl.program_id(0)

    @pl.when(step < LOAD_STEPS)
    def _():
        _load_chunk(step, w1_ref, w1_s)
        _load_chunk(step, w2_ref, w2_s)

    is_prompt = _is_prompt_tile(step, FFN_TM)

    @pl.when((step >= LOAD_STEPS) & is_prompt)
    def _():
        op_ref[...] = _ffn_body(x_ref[...], mod_ref, g_ref, w1_s, w2_s)

    @pl.when(jnp.logical_not(is_prompt))
    def _():
        os_ref[...] = _ffn_body(x_ref[...], mod_ref, g_ref, w1_s, w2_s)


def _ffn(x, mods, layer, norm_g, w1, w2, split_out=False):
    tm = FFN_TM
    if split_out:
        out_specs = [_prompt_tok_spec(tm), _sample_tok_spec(tm)]
        out_shape = [jax.ShapeDtypeStruct((P_TOK, D_MODEL), F32), jax.ShapeDtypeStruct((S_TOK, D_MODEL), F32)]
    else:
        out_specs = _tok_spec(tm)
        out_shape = jax.ShapeDtypeStruct((N_TOK, D_MODEL), F32)
    return pl.pallas_call(
        _ffn_split_out_kernel if split_out else _ffn_kernel,
        grid=(LOAD_STEPS + N_TOK // tm,),
        in_specs=[
            _tok_spec(tm),
            _mod_spec(layer, tm),
            _layer_spec(2 * layer + 1, (1, D_MODEL)),
            _chunk_spec(layer, D_MODEL, D_FF),
            _chunk_spec(layer, D_FF, D_MODEL),
        ],
        out_specs=out_specs,
        out_shape=out_shape,
        scratch_shapes=[
            pltpu.VMEM((D_MODEL, D_FF), BF16),
            pltpu.VMEM((D_FF, D_MODEL), BF16),
        ],
        compiler_params=_params(),
        name="ffn",
    )(x, mods, norm_g, w1, w2)


CONV_TM = 1024
CONV_TC = 512


def _conv_kernel(x_ref, mod_ref, g_ref, win_ref, cw_ref, cb_ref, wout_ref, o_ref, win_s, wout_s):
    step = pl.program_id(0)

    @pl.when(step < LOAD_STEPS)
    def _():
        _load_chunk(step, win_ref, win_s)
        _load_chunk(step, wout_ref, wout_s)

    @pl.when(step >= LOAD_STEPS)
    def _():
        seq_len = jnp.where(_is_prompt_tile(step, CONV_TM), SEQ, DEC_SEQ)
        pos = lax.broadcasted_iota(jnp.int32, (CONV_TM, 1), 0) & (seq_len - 1)
        has_prev = pos != 0
        has_next = pos != seq_len - 1
        x = x_ref[...]
        mod = mod_ref[0]
        h = (_rms(x, g_ref[...]) * (1.0 + mod[1:2]) + mod[0:1]).astype(BF16)
        cw = cw_ref[...]
        cb = cb_ref[...]
        acc = jnp.zeros((CONV_TM, D_MODEL), F32)
        for c in range(D_MODEL // CONV_TC):
            lo = c * CONV_TC
            cols = slice(lo, lo + CONV_TC)
            bg = _dot(h, win_s[:, lo:lo + CONV_TC])
            cg = _dot(h, win_s[:, D_MODEL + lo:D_MODEL + lo + CONV_TC])
            xt = _dot(h, win_s[:, 2 * D_MODEL + lo:2 * D_MODEL + lo + CONV_TC])
            z = cg * xt
            z_prev = jnp.where(has_prev, pltpu.roll(z, 1, axis=0), 0.0)
            z_next = jnp.where(has_next, pltpu.roll(z, CONV_TM - 1, axis=0), 0.0)
            zc = cw[0:1, cols] * z_prev + cw[1:2, cols] * z + cw[2:3, cols] * z_next + cb[:, cols]
            acc = acc + _dot((bg * zc).astype(BF16), wout_s[cols, :])
        o_ref[...] = x + mod[2:3] * acc


def _conv(x, mods, layer, norm_g, j, w_in, conv_w, conv_b, w_out):
    tm = CONV_TM
    return pl.pallas_call(
        _conv_kernel,
        grid=(LOAD_STEPS + N_TOK // tm,),
        in_specs=[
            _tok_spec(tm),
            _mod_spec(layer, tm),
            _layer_spec(2 * layer, (1, D_MODEL)),
            _chunk_spec(j, D_MODEL, 3 * D_MODEL),
            _layer_spec(j, (3, D_MODEL)),
            _layer_spec(j, (1, D_MODEL)),
            _chunk_spec(j, D_MODEL, D_MODEL),
        ],
        out_specs=_tok_spec(tm),
        out_shape=jax.ShapeDtypeStruct((N_TOK, D_MODEL), F32),
        scratch_shapes=[
            pltpu.VMEM((D_MODEL, 3 * D_MODEL), BF16),
            pltpu.VMEM((D_MODEL, D_MODEL), BF16),
        ],
        compiler_params=_params(),
        name="sconv",
    )(x, mods, norm_g, w_in, conv_w, conv_b, w_out)


QKV_TM = 512


def _qkv_kernel(x_ref, mod_ref, g_ref, w_ref, gq_ref, gk_ref, hm_ref,
                q_ref, k_ref, v_ref, kc_ref, vc_ref, w_s):
    step = pl.program_id(0)

    @pl.when(step < LOAD_STEPS)
    def _():
        _load_chunk(step, w_ref, w_s)

    @pl.when(step >= LOAD_STEPS)
    def _():
        x = x_ref[...]
        mod = mod_ref[0]
        h = (_rms(x, g_ref[...]) * (1.0 + mod[1:2]) + mod[0:1]).astype(BF16)
        head_mean = hm_ref[...]

        def head_norm(y, gain):
            sq = (y * y).astype(BF16)
            ms = jnp.concatenate(
                [_dot(sq[:, c:c + MXU_TILE], head_mean) for c in range(0, D_MODEL, MXU_TILE)], axis=1)
            return y * lax.rsqrt(ms + EPS) * gain

        q = head_norm(_dot(h, w_s[:, :D_MODEL]), gq_ref[...])
        k = head_norm(_dot(h, w_s[:, D_MODEL:2 * D_MODEL]), gk_ref[...])
        v = _dot(h, w_s[:, 2 * D_MODEL:])
        q_ref[...] = q.astype(BF16)
        k_ref[...] = k.astype(BF16)
        v_ref[...] = v.astype(BF16)

        @pl.when(_is_prompt_tile(step, QKV_TM))
        def _():
            kc_ref[...] = k
            vc_ref[...] = v


def _qkv(x, mods, layer, norm_g, j, w_qkv, q_gain, k_gain):
    tm = QKV_TM
    head_mean = jnp.asarray(
        np.kron(np.eye(MXU_TILE // HEAD_DIM), np.full((HEAD_DIM, HEAD_DIM), 1.0 / HEAD_DIM)), BF16)
    return pl.pallas_call(
        _qkv_kernel,
        grid=(LOAD_STEPS + N_TOK // tm,),
        in_specs=[
            _tok_spec(tm),
            _mod_spec(layer, tm),
            _layer_spec(2 * layer, (1, D_MODEL)),
            _chunk_spec(j, D_MODEL, 3 * D_MODEL),
            _const_spec((1, D_MODEL)),
            _const_spec((1, D_MODEL)),
            _const_spec((MXU_TILE, MXU_TILE)),
        ],
        out_specs=[_tok_spec(tm), _tok_spec(tm), _tok_spec(tm), _prompt_tok_spec(tm), _prompt_tok_spec(tm)],
        out_shape=[
            jax.ShapeDtypeStruct((N_TOK, D_MODEL), BF16),
            jax.ShapeDtypeStruct((N_TOK, D_MODEL), BF16),
            jax.ShapeDtypeStruct((N_TOK, D_MODEL), BF16),
            jax.ShapeDtypeStruct((P_TOK, D_MODEL), F32),
            jax.ShapeDtypeStruct((P_TOK, D_MODEL), F32),
        ],
        scratch_shapes=[pltpu.VMEM((D_MODEL, 3 * D_MODEL), BF16)],
        compiler_params=_params(),
        name="qkv",
    )(x, mods, norm_g, w_qkv,
      jnp.tile(q_gain[j], N_HEADS).reshape(1, D_MODEL), jnp.tile(k_gain[j], N_HEADS).reshape(1, D_MODEL),
      head_mean)


NBR_GROUP_ROWS = GRID_ROWS // 2
NBR_KEY_ROWS = NBR_GROUP_ROWS + WIN_ROWS // 2
NBR_Q = NBR_GROUP_ROWS * GRID_W
NBR_KEYS = NBR_KEY_ROWS * GRID_W


def _band_start(r):
    return min(max(r - WIN_ROWS // 2, 0), GRID_ROWS - WIN_ROWS)


def _key_row_start(r):
    return 0 if r < NBR_GROUP_ROWS else GRID_ROWS - NBR_KEY_ROWS


def _bias_kernel(rpb_ref, o_ref):
    h = pl.program_id(0)
    qc = lax.broadcasted_iota(jnp.int32, (GRID_W, GRID_W), 0)
    kc = lax.broadcasted_iota(jnp.int32, (GRID_W, GRID_W), 1)
    rel_c = jnp.clip(kc - qc + (WIN_COLS - 1), 0, RPB_COLS - 1)
    col_start = jnp.clip(qc - WIN_COLS // 2, 0, GRID_W - WIN_COLS)
    col_ok = (kc >= col_start) & (kc < col_start + WIN_COLS)
    base = h * (RPB_ROWS * RPB_COLS)
    row_tiles = []
    for rel_r in range(RPB_ROWS):
        tile = jnp.zeros((GRID_W, GRID_W), F32)
        for c in range(RPB_COLS):
            tile = jnp.where(rel_c == c, rpb_ref[base + rel_r * RPB_COLS + c], tile)
        row_tiles.append(jnp.where(col_ok, tile, MASK_VALUE))
    masked = jnp.full((GRID_W, GRID_W), MASK_VALUE, F32)
    for r in range(GRID_ROWS):
        for i in range(NBR_KEY_ROWS):
            key_row = _key_row_start(r) + i
            in_window = _band_start(r) <= key_row < _band_start(r) + WIN_ROWS
            tile = row_tiles[key_row - r + WIN_ROWS - 1] if in_window else masked
            o_ref[0, r, :, i * GRID_W:(i + 1) * GRID_W] = tile


def _window_bias(rpb):
    return pl.pallas_call(
        _bias_kernel,
        grid=(N_HEADS,),
        in_specs=[pl.BlockSpec(memory_space=pltpu.SMEM)],
        out_specs=pl.BlockSpec((1, GRID_ROWS, GRID_W, NBR_KEYS), lambda h: (h, 0, 0, 0)),
        out_shape=jax.ShapeDtypeStruct((N_HEADS, GRID_ROWS, GRID_W, NBR_KEYS), F32),
        compiler_params=_params(),
        name="window_bias",
    )(rpb.reshape(N_HEADS * RPB_ROWS * RPB_COLS))


def _head_lane_mask(half):
    lane = lax.broadcasted_iota(jnp.int32, (1, HEAD_PAIR), 1)
    return (lane >= HEAD_DIM) if half else (lane < HEAD_DIM)


def _scaled_head_queries(q2, sel):
    return jnp.where(sel, q2 * ATT_SCALE, jnp.zeros_like(q2))


def _ctx_attn_kernel(q_ref, k_ref, v_ref, o_ref):
    for p in range(N_PAIRS):
        cols = slice(p * HEAD_PAIR, (p + 1) * HEAD_PAIR)
        q2 = q_ref[:, cols]
        k2 = k_ref[:, cols]
        v2 = v_ref[:, cols]
        out = None
        for half in range(2):
            sel = _head_lane_mask(half)
            s = _dot_nt(_scaled_head_queries(q2, sel), k2)
            e = jnp.exp(s - jnp.max(s, axis=-1, keepdims=True))
            den = jnp.sum(e, axis=-1, keepdims=True)
            o = _dot(e.astype(BF16), v2) / den
            out = o if out is None else jnp.where(sel, o, out)
        o_ref[:, cols] = out.astype(BF16)


def _ctx_attention(q, k, v):
    spec = pl.BlockSpec((SEQ, D_MODEL), lambda b: (b, 0))
    return pl.pallas_call(
        _ctx_attn_kernel,
        grid=(BATCH,),
        in_specs=[spec, spec, spec],
        out_specs=spec,
        out_shape=jax.ShapeDtypeStruct((P_TOK, D_MODEL), BF16),
        compiler_params=_params(),
        name="ctx_attention",
    )(q, k, v)


def _nbr_attn_kernel(q_ref, k_ref, v_ref, ck_ref, cv_ref, bias_ref, o_ref):
    ck2 = ck_ref[0].astype(BF16)
    cv2 = cv_ref[0].astype(BF16)
    for g in range(GRID_ROWS // NBR_GROUP_ROWS):
        r0 = g * NBR_GROUP_ROWS
        rows = slice(r0 * GRID_W, r0 * GRID_W + NBR_Q)
        key_lo = _key_row_start(r0) * GRID_W
        q2 = q_ref[rows, :]
        k_band = k_ref[key_lo:key_lo + NBR_KEYS, :]
        v_band = v_ref[key_lo:key_lo + NBR_KEYS, :]
        out = None
        for half in range(2):
            sel = _head_lane_mask(half)
            qm = _scaled_head_queries(q2, sel)
            bias = bias_ref[half, r0:r0 + NBR_GROUP_ROWS].reshape(NBR_Q, NBR_KEYS)
            s_loc = _dot_nt(qm, k_band) + bias
            s_ctx = _dot_nt(qm, ck2)
            m = jnp.maximum(jnp.max(s_loc, axis=-1, keepdims=True), jnp.max(s_ctx, axis=-1, keepdims=True))
            e_loc = jnp.exp(s_loc - m)
            e_ctx = jnp.exp(s_ctx - m)
            den = jnp.sum(e_loc, axis=-1, keepdims=True) + jnp.sum(e_ctx, axis=-1, keepdims=True)
            o = (_dot(e_loc.astype(BF16), v_band) + _dot(e_ctx.astype(BF16), cv2)) / den
            out = o if out is None else jnp.where(sel, o, out)
        o_ref[rows, :] = out.astype(BF16)


def _nbr_attention(q, k, v, ck, cv, bias):
    first = P_TOK // DEC_SEQ
    tok = pl.BlockSpec((DEC_SEQ, HEAD_PAIR), lambda p, b: (first + b, p))
    ctx = pl.BlockSpec((1, PAST_LEN, HEAD_PAIR), lambda p, b: (b, 0, p))
    return pl.pallas_call(
        _nbr_attn_kernel,
        grid=(N_PAIRS, DEC_BATCH),
        in_specs=[tok, tok, tok, ctx, ctx,
                  pl.BlockSpec((2, GRID_ROWS, GRID_W, NBR_KEYS), lambda p, b: (p, 0, 0, 0))],
        out_specs=pl.BlockSpec((DEC_SEQ, HEAD_PAIR), lambda p, b: (b, p)),
        out_shape=jax.ShapeDtypeStruct((S_TOK, D_MODEL), BF16),
        compiler_params=_params(2),
        name="nbr_attention",
    )(q, k, v, ck, cv, bias)


def _proj_ffn_kernel(x_ref, mod_ref, ap_ref, as_ref, wo_ref, g_ref, w1_ref, w2_ref, o_ref,
                     wo_s, w1_s, w2_s):
    step = pl.program_id(0)

    @pl.when(step < LOAD_STEPS)
    def _():
        _load_chunk(step, wo_ref, wo_s)
        _load_chunk(step, w1_ref, w1_s)
        _load_chunk(step, w2_ref, w2_s)

    @pl.when(step >= LOAD_STEPS)
    def _():
        a = jnp.where(_is_prompt_tile(step, FFN_TM), ap_ref[...], as_ref[...])
        x = x_ref[...] + mod_ref[0][2:3] * _dot(a, wo_s[...])
        o_ref[...] = _ffn_body(x, mod_ref, g_ref, w1_s, w2_s)


def _attn_proj_ffn(x, mods, layer, norm_g, attn_prompt, attn_sample, j, w_o, w1, w2):
    tm = FFN_TM
    return pl.pallas_call(
        _proj_ffn_kernel,
        grid=(LOAD_STEPS + N_TOK // tm,),
        in_specs=[
            _tok_spec(tm),
            _mod_spec(layer, tm),
            _prompt_tok_spec(tm),
            _sample_tok_spec(tm),
            _chunk_spec(j, D_MODEL, D_MODEL),
            _layer_spec(2 * layer + 1, (1, D_MODEL)),
            _chunk_spec(layer, D_MODEL, D_FF),
            _chunk_spec(layer, D_FF, D_MODEL),
        ],
        out_specs=_tok_spec(tm),
        out_shape=jax.ShapeDtypeStruct((N_TOK, D_MODEL), F32),
        scratch_shapes=[
            pltpu.VMEM((D_MODEL, D_MODEL), BF16),
            pltpu.VMEM((D_MODEL, D_FF), BF16),
            pltpu.VMEM((D_FF, D_MODEL), BF16),
        ],
        compiler_params=_params(),
        name="attn_proj_ffn",
    )(x, mods, attn_prompt, attn_sample, w_o, norm_g, w1, w2)


def kernel(x_prompt, x_sample, cache_k, cache_v, c, c_ctx, norm_g, ada_w, ada_b, a_w_in, a_v_gain, a_ws, a_bs, a_w_out, b_w_qkv, b_q_gain, b_k_gain, b_rpb, b_w_o, c_w_in, c_conv_w, c_conv_b, c_w_out, ff_w1, ff_w2):
    n_a = a_w_in.shape[0]
    cond = jnp.concatenate(
        [c, c_ctx[None, :], jnp.zeros((COND_ROWS - DEC_BATCH - 1, D_MODEL), F32)], axis=0)
    mods = _adaln(cond, ada_w, ada_b).reshape(DEPTH, COND_ROWS, 6, D_MODEL)
    norm_g = norm_g.reshape(2 * DEPTH, 1, D_MODEL)
    a_v_gain = a_v_gain.reshape(n_a, 1, A_HALF)
    a_bs = a_bs.reshape(n_a, A_GROUPS, CHUNK, 1)
    c_conv_b = c_conv_b.reshape(-1, 1, D_MODEL)
    x = (x_prompt.reshape(P_TOK, D_MODEL), x_sample.reshape(S_TOK, D_MODEL))
    new_k, new_v = [], []
    for i in range(DEPTH):
        kind, j = i % 3, i // 3
        if kind == 0:
            x = _gmlp(x, mods, i, norm_g, j, a_w_in, a_v_gain, a_ws, a_bs, a_w_out)
        elif kind == 1:
            q, k, v, k_new, v_new = _qkv(x, mods, i, norm_g, j, b_w_qkv, b_q_gain, b_k_gain)
            new_k.append(k_new.reshape(BATCH, SEQ, N_HEADS, HEAD_DIM))
            new_v.append(v_new.reshape(BATCH, SEQ, N_HEADS, HEAD_DIM))
            o_prompt = _ctx_attention(q, k, v)
            bias = _window_bias(b_rpb[j])
            o_sample = _nbr_attention(
                q, k, v,
                cache_k[:, j].reshape(DEC_BATCH, PAST_LEN, D_MODEL),
                cache_v[:, j].reshape(DEC_BATCH, PAST_LEN, D_MODEL), bias)
            x = _attn_proj_ffn(x, mods, i, norm_g, o_prompt, o_sample, j, b_w_o, ff_w1, ff_w2)
            continue
        else:
            x = _conv(x, mods, i, norm_g, j, c_w_in, c_conv_w, c_conv_b, c_w_out)
        x = _ffn(x, mods, i, norm_g, ff_w1, ff_w2, split_out=(i == DEPTH - 1))
    y_prompt = x[0].reshape(BATCH, SEQ, D_MODEL)
    y_sample = x[1].reshape(DEC_BATCH, DEC_SEQ, D_MODEL)
    return (y_prompt, y_sample, jnp.stack(new_k, axis=1), jnp.stack(new_v, axis=1))
```

```python
import jax
import jax.numpy as jnp
import numpy as np
from jax import lax
from jax.experimental import pallas as pl
from jax.experimental.pallas import tpu as pltpu

D_MODEL = 1024
BATCH = 16
SEQ = 256
DEPTH = 4
DEC_BATCH = 8
DEC_SEQ = 1024
PAST_LEN = 512
GRID_W = 64
GRID_ROWS = DEC_SEQ // GRID_W
CHUNK = 128
A_HALF = 2 * D_MODEL
A_GROUPS = 8
A_GROUP_W = A_HALF // A_GROUPS
N_HEADS = 16
HEAD_DIM = D_MODEL // N_HEADS
WIN_ROWS = 8
WIN_COLS = 16
RPB_ROWS = 2 * WIN_ROWS - 1
RPB_COLS = 2 * WIN_COLS - 1
D_FF = 4 * D_MODEL
EPS = 1e-6
ATT_SCALE = HEAD_DIM ** -0.5
MASK_VALUE = -1e30

P_TOK = BATCH * SEQ
S_TOK = DEC_BATCH * DEC_SEQ
N_TOK = P_TOK + S_TOK
COND_ROWS = 16
CTX_ROW = DEC_BATCH
LOAD_STEPS = 8
HEAD_PAIR = 2 * HEAD_DIM
N_PAIRS = N_HEADS // 2
MXU_TILE = 256

F32 = jnp.float32
BF16 = jnp.bfloat16
VMEM_LIMIT = 56 * 1024 * 1024


def _params(n_axes=1, vmem=VMEM_LIMIT):
    return pltpu.CompilerParams(dimension_semantics=("arbitrary",) * n_axes, vmem_limit_bytes=vmem)


def _dot(a, b):
    return jnp.dot(a, b, preferred_element_type=F32)


def _dot_nt(a, b):
    return lax.dot_general(a, b, (((1,), (1,)), ((), ())), preferred_element_type=F32)


def _rms(x, g):
    return x * lax.rsqrt(jnp.mean(x * x, axis=-1, keepdims=True) + EPS) * g


def _gelu(x):
    return 0.5 * x * (1.0 + jnp.tanh(0.7978845608028654 * (x + 0.044715 * (x * x * x))))


def _tok_spec(tm, width=D_MODEL):
    return pl.BlockSpec((tm, width), lambda s: (jnp.maximum(s - LOAD_STEPS, 0), 0))


def _prompt_tok_spec(tm):
    last = P_TOK // tm - 1
    return pl.BlockSpec((tm, D_MODEL), lambda s: (jnp.clip(s - LOAD_STEPS, 0, last), 0))


def _sample_tok_spec(tm):
    first, last = P_TOK // tm, S_TOK // tm - 1
    return pl.BlockSpec((tm, D_MODEL), lambda s: (jnp.clip(s - LOAD_STEPS - first, 0, last), 0))


def _is_prompt_tile(step, tm):
    return step - LOAD_STEPS < P_TOK // tm


def _mod_spec(layer, tm):
    n_prompt_tiles = P_TOK // tm
    tiles_per_seq = DEC_SEQ // tm

    def index(s):
        t = jnp.maximum(s - LOAD_STEPS, 0)
        row = jnp.where(t < n_prompt_tiles, CTX_ROW, jnp.maximum(t - n_prompt_tiles, 0) // tiles_per_seq)
        return (layer, row, 0, 0)

    return pl.BlockSpec((None, 1, 6, D_MODEL), index)


def _chunk_spec(layer, rows, cols):
    return pl.BlockSpec((None, rows // LOAD_STEPS, cols),
                        lambda s: (layer, jnp.minimum(s, LOAD_STEPS - 1), 0))


def _layer_spec(layer, shape):
    zeros = (0,) * len(shape)
    return pl.BlockSpec((None,) + tuple(shape), lambda s: (layer,) + zeros)


def _const_spec(shape):
    zeros = (0,) * len(shape)
    return pl.BlockSpec(shape, lambda s: zeros)


def _load_chunk(step, w_ref, w_scr):
    rows = w_ref.shape[0]
    off = pl.multiple_of(step * rows, rows)
    w_scr[pl.ds(off, rows), :] = w_ref[...].astype(BF16)


ADA_TN = 1024


def _ada_kernel(c_ref, w_ref, b_ref, o_ref):
    c = c_ref[...]
    a = c * (1.0 / (1.0 + jnp.exp(-c)))
    a_hi = a.astype(BF16)
    a_lo = (a - a_hi.astype(F32)).astype(BF16)
    w = w_ref[0]
    w_hi = w.astype(BF16)
    w_lo = (w - w_hi.astype(F32)).astype(BF16)
    o_ref[0] = _dot(a_hi, w_hi) + _dot(a_hi, w_lo) + _dot(a_lo, w_hi) + b_ref[0]


def _adaln(cond, ada_w, ada_b):
    n_out = 6 * D_MODEL
    return pl.pallas_call(
        _ada_kernel,
        grid=(DEPTH, n_out // ADA_TN),
        in_specs=[
            pl.BlockSpec((COND_ROWS, D_MODEL), lambda i, j: (0, 0)),
            pl.BlockSpec((1, D_MODEL, ADA_TN), lambda i, j: (i, 0, j)),
            pl.BlockSpec((1, 1, ADA_TN), lambda i, j: (i, 0, j)),
        ],
        out_specs=pl.BlockSpec((1, COND_ROWS, ADA_TN), lambda i, j: (i, 0, j)),
        out_shape=jax.ShapeDtypeStruct((DEPTH, COND_ROWS, n_out), F32),
        compiler_params=_params(2),
        name="adaln",
    )(cond, ada_w, ada_b.reshape(DEPTH, 1, n_out))


GMLP_TM = 512
GMLP_TC = 1024


def _gmlp_body(x, mod_ref, g_ref, vg_ref, ws_ref, bs_ref, o_ref, win_s, wout_s, v_s):
    mod = mod_ref[0]
    h = (_rms(x, g_ref[...]) * (1.0 + mod[1:2]) + mod[0:1]).astype(BF16)
    ssq = jnp.zeros((GMLP_TM, 1), F32)
    for c in range(0, A_HALF, GMLP_TC):
        cols = slice(c, c + GMLP_TC)
        v_c = _gelu(_dot(h, win_s[:, A_HALF + c:A_HALF + c + GMLP_TC]))
        ssq = ssq + jnp.sum(v_c * v_c, axis=-1, keepdims=True)
        v_s[:, cols] = v_c
    inv_rms = lax.rsqrt(ssq * (1.0 / A_HALF) + EPS)
    acc = jnp.zeros((GMLP_TM, D_MODEL), F32)
    for c in range(0, A_HALF, GMLP_TC):
        cols = slice(c, c + GMLP_TC)
        v_n = (v_s[:, cols] * inv_rms * vg_ref[:, cols]).astype(BF16)
        u_c = _gelu(_dot(h, win_s[:, cols]))
        gated_cols = []
        for g in range(GMLP_TC // A_GROUP_W):
            gcols = slice(g * A_GROUP_W, (g + 1) * A_GROUP_W)
            w_g = ws_ref[c // A_GROUP_W + g].astype(BF16)
            b_g = bs_ref[c // A_GROUP_W + g]
            gated = []
            for n in range(GMLP_TM // CHUNK):
                rows = slice(n * CHUNK, (n + 1) * CHUNK)
                s = _dot(w_g, v_n[rows, gcols]) + b_g
                gated.append((u_c[rows, gcols] * s).astype(BF16))
            gated_cols.append(jnp.concatenate(gated, axis=0))
        acc = acc + _dot(jnp.concatenate(gated_cols, axis=1), wout_s[cols, :])
    o_ref[...] = x + mod[2:3] * acc


def _gmlp_kernel(x_ref, mod_ref, g_ref, win_ref, vg_ref, ws_ref, bs_ref, wout_ref, o_ref,
                 win_s, wout_s, v_s):
    step = pl.program_id(0)

    @pl.when(step < LOAD_STEPS)
    def _():
        _load_chunk(step, win_ref, win_s)
        _load_chunk(step, wout_ref, wout_s)

    @pl.when(step >= LOAD_STEPS)
    def _():
        _gmlp_body(x_ref[...], mod_ref, g_ref, vg_ref, ws_ref, bs_ref, o_ref, win_s, wout_s, v_s)


def _gmlp_split_in_kernel(xp_ref, xs_ref, mod_ref, g_ref, win_ref, vg_ref, ws_ref, bs_ref, wout_ref,
                          o_ref, win_s, wout_s, v_s):
    step = pl.program_id(0)

    @pl.when(step < LOAD_STEPS)
    def _():
        _load_chunk(step, win_ref, win_s)
        _load_chunk(step, wout_ref, wout_s)

    @pl.when(step >= LOAD_STEPS)
    def _():
        x = jnp.where(_is_prompt_tile(step, GMLP_TM), xp_ref[...], xs_ref[...])
        _gmlp_body(x, mod_ref, g_ref, vg_ref, ws_ref, bs_ref, o_ref, win_s, wout_s, v_s)


def _gmlp(xs, mods, layer, norm_g, j, w_in, v_gain, ws, bs, w_out):
    tm = GMLP_TM
    split_in = isinstance(xs, tuple)
    x_specs = [_prompt_tok_spec(tm), _sample_tok_spec(tm)] if split_in else [_tok_spec(tm)]
    x_args = list(xs) if split_in else [xs]
    return pl.pallas_call(
        _gmlp_split_in_kernel if split_in else _gmlp_kernel,
        grid=(LOAD_STEPS + N_TOK // tm,),
        in_specs=x_specs + [
            _mod_spec(layer, tm),
            _layer_spec(2 * layer, (1, D_MODEL)),
            _chunk_spec(j, D_MODEL, 2 * A_HALF),
            _layer_spec(j, (1, A_HALF)),
            _layer_spec(j, (A_GROUPS, CHUNK, CHUNK)),
            _layer_spec(j, (A_GROUPS, CHUNK, 1)),
            _chunk_spec(j, A_HALF, D_MODEL),
        ],
        out_specs=_tok_spec(tm),
        out_shape=jax.ShapeDtypeStruct((N_TOK, D_MODEL), F32),
        scratch_shapes=[
            pltpu.VMEM((D_MODEL, 2 * A_HALF), BF16),
            pltpu.VMEM((A_HALF, D_MODEL), BF16),
            pltpu.VMEM((tm, A_HALF), F32),
        ],
        compiler_params=_params(),
        name="gmlp",
    )(*x_args, mods, norm_g, w_in, v_gain, ws, bs, w_out)


FFN_TM = 512
FFN_TC = 1024


def _ffn_body(x, mod_ref, g_ref, w1_s, w2_s):
    mod = mod_ref[0]
    h = (_rms(x, g_ref[...]) * (1.0 + mod[4:5]) + mod[3:4]).astype(BF16)
    acc = jnp.zeros((FFN_TM, D_MODEL), F32)
    for c in range(D_FF // FFN_TC):
        cols = slice(c * FFN_TC, (c + 1) * FFN_TC)
        hid = jnp.square(jnp.maximum(_dot(h, w1_s[:, cols]), 0.0)).astype(BF16)
        acc = acc + _dot(hid, w2_s[cols, :])
    return x + mod[5:6] * acc


def _ffn_kernel(x_ref, mod_ref, g_ref, w1_ref, w2_ref, o_ref, w1_s, w2_s):
    step = pl.program_id(0)

    @pl.when(step < LOAD_STEPS)
    def _():
        _load_chunk(step, w1_ref, w1_s)
        _load_chunk(step, w2_ref, w2_s)

    @pl.when(step >= LOAD_STEPS)
    def _():
        o_ref[...] = _ffn_body(x_ref[...], mod_ref, g_ref, w1_s, w2_s)


def _ffn_split_out_kernel(x_ref, mod_ref, g_ref, w1_ref, w2_ref, op_ref, os_ref, w1_s, w2_s):
    step = pl.program_id(0)

    @pl.when(step < LOAD_STEPS)
    def _():
        _load_chunk(step, w1_ref, w1_s)
        _load_chunk(step, w2_ref, w2_s)

    is_prompt = _is_prompt_tile(step, FFN_TM)

    @pl.when((step >= LOAD_STEPS) & is_prompt)
    def _():
        op_ref[...] = _ffn_body(x_ref[...], mod_ref, g_ref, w1_s, w2_s)

    @pl.when(jnp.logical_not(is_prompt))
    def _():
        os_ref[...] = _ffn_body(x_ref[...], mod_ref, g_ref, w1_s, w2_s)


def _ffn(x, mods, layer, norm_g, w1, w2, split_out=False):
    tm = FFN_TM
    if split_out:
        out_specs = [_prompt_tok_spec(tm), _sample_tok_spec(tm)]
        out_shape = [jax.ShapeDtypeStruct((P_TOK, D_MODEL), F32), jax.ShapeDtypeStruct((S_TOK, D_MODEL), F32)]
    else:
        out_specs = _tok_spec(tm)
        out_shape = jax.ShapeDtypeStruct((N_TOK, D_MODEL), F32)
    return pl.pallas_call(
        _ffn_split_out_kernel if split_out else _ffn_kernel,
        grid=(LOAD_STEPS + N_TOK // tm,),
        in_specs=[
            _tok_spec(tm),
            _mod_spec(layer, tm),
            _layer_spec(2 * layer + 1, (1, D_MODEL)),
            _chunk_spec(layer, D_MODEL, D_FF),
            _chunk_spec(layer, D_FF, D_MODEL),
        ],
        out_specs=out_specs,
        out_shape=out_shape,
        scratch_shapes=[
            pltpu.VMEM((D_MODEL, D_FF), BF16),
            pltpu.VMEM((D_FF, D_MODEL), BF16),
        ],
        compiler_params=_params(),
        name="ffn",
    )(x, mods, norm_g, w1, w2)


CONV_TM = 1024
CONV_TC = 512


def _conv_kernel(x_ref, mod_ref, g_ref, win_ref, cw_ref, cb_ref, wout_ref, o_ref, win_s, wout_s):
    step = pl.program_id(0)

    @pl.when(step < LOAD_STEPS)
    def _():
        _load_chunk(step, win_ref, win_s)
        _load_chunk(step, wout_ref, wout_s)

    @pl.when(step >= LOAD_STEPS)
    def _():
        seq_len = jnp.where(_is_prompt_tile(step, CONV_TM), SEQ, DEC_SEQ)
        pos = lax.broadcasted_iota(jnp.int32, (CONV_TM, 1), 0) & (seq_len - 1)
        has_prev = pos != 0
        has_next = pos != seq_len - 1
        x = x_ref[...]
        mod = mod_ref[0]
        h = (_rms(x, g_ref[...]) * (1.0 + mod[1:2]) + mod[0:1]).astype(BF16)
        cw = cw_ref[...]
        cb = cb_ref[...]
        acc = jnp.zeros((CONV_TM, D_MODEL), F32)
        for c in range(D_MODEL // CONV_TC):
            lo = c * CONV_TC
            cols = slice(lo, lo + CONV_TC)
            bg = _dot(h, win_s[:, lo:lo + CONV_TC])
            cg = _dot(h, win_s[:, D_MODEL + lo:D_MODEL + lo + CONV_TC])
            xt = _dot(h, win_s[:, 2 * D_MODEL + lo:2 * D_MODEL + lo + CONV_TC])
            z = cg * xt
            z_prev = jnp.where(has_prev, pltpu.roll(z, 1, axis=0), 0.0)
            z_next = jnp.where(has_next, pltpu.roll(z, CONV_TM - 1, axis=0), 0.0)
            zc = cw[0:1, cols] * z_prev + cw[1:2, cols] * z + cw[2:3, cols] * z_next + cb[:, cols]
            acc = acc + _dot((bg * zc).astype(BF16), wout_s[cols, :])
        o_ref[...] = x + mod[2:3] * acc


def _conv(x, mods, layer, norm_g, j, w_in, conv_w, conv_b, w_out):
    tm = CONV_TM
    return pl.pallas_call(
        _conv_kernel,
        grid=(LOAD_STEPS + N_TOK // tm,),
        in_specs=[
            _tok_spec(tm),
            _mod_spec(layer, tm),
            _layer_spec(2 * layer, (1, D_MODEL)),
            _chunk_spec(j, D_MODEL, 3 * D_MODEL),
            _layer_spec(j, (3, D_MODEL)),
            _layer_spec(j, (1, D_MODEL)),
            _chunk_spec(j, D_MODEL, D_MODEL),
        ],
        out_specs=_tok_spec(tm),
        out_shape=jax.ShapeDtypeStruct((N_TOK, D_MODEL), F32),
        scratch_shapes=[
            pltpu.VMEM((D_MODEL, 3 * D_MODEL), BF16),
            pltpu.VMEM((D_MODEL, D_MODEL), BF16),
        ],
        compiler_params=_params(),
        name="sconv",
    )(x, mods, norm_g, w_in, conv_w, conv_b, w_out)


QKV_TM = 512


def _qkv_kernel(x_ref, mod_ref, g_ref, w_ref, gq_ref, gk_ref, hm_ref,
                q_ref, k_ref, v_ref, kc_ref, vc_ref, w_s):
    step = pl.program_id(0)

    @pl.when(step < LOAD_STEPS)
    def _():
        _load_chunk(step, w_ref, w_s)

    @pl.when(step >= LOAD_STEPS)
    def _():
        x = x_ref[...]
        mod = mod_ref[0]
        h = (_rms(x, g_ref[...]) * (1.0 + mod[1:2]) + mod[0:1]).astype(BF16)
        head_mean = hm_ref[...]

        def head_norm(y, gain):
            sq = (y * y).astype(BF16)
            ms = jnp.concatenate(
                [_dot(sq[:, c:c + MXU_TILE], head_mean) for c in range(0, D_MODEL, MXU_TILE)], axis=1)
            return y * lax.rsqrt(ms + EPS) * gain

        q = head_norm(_dot(h, w_s[:, :D_MODEL]), gq_ref[...])
        k = head_norm(_dot(h, w_s[:, D_MODEL:2 * D_MODEL]), gk_ref[...])
        v = _dot(h, w_s[:, 2 * D_MODEL:])
        q_ref[...] = q.astype(BF16)
        k_ref[...] = k.astype(BF16)
        v_ref[...] = v.astype(BF16)

        @pl.when(_is_prompt_tile(step, QKV_TM))
        def _():
            for hd in range(N_HEADS):
                rows = pl.ds(hd, QKV_TM, stride=N_HEADS)
                kc_ref[rows, :] = k[:, hd * HEAD_DIM:(hd + 1) * HEAD_DIM]
                vc_ref[rows, :] = v[:, hd * HEAD_DIM:(hd + 1) * HEAD_DIM]


def _qkv(x, mods, layer, norm_g, j, w_qkv, q_gain, k_gain):
    tm = QKV_TM
    head_mean = jnp.asarray(
        np.kron(np.eye(MXU_TILE // HEAD_DIM), np.full((HEAD_DIM, HEAD_DIM), 1.0 / HEAD_DIM)), BF16)
    last_prompt_tile = P_TOK // tm - 1
    cache_spec = pl.BlockSpec(
        (tm * N_HEADS, HEAD_DIM), lambda s: (jnp.clip(s - LOAD_STEPS, 0, last_prompt_tile), 0))
    return pl.pallas_call(
        _qkv_kernel,
        grid=(LOAD_STEPS + N_TOK // tm,),
        in_specs=[
            _tok_spec(tm),
            _mod_spec(layer, tm),
            _layer_spec(2 * layer, (1, D_MODEL)),
            _chunk_spec(j, D_MODEL, 3 * D_MODEL),
            _const_spec((1, D_MODEL)),
            _const_spec((1, D_MODEL)),
            _const_spec((MXU_TILE, MXU_TILE)),
        ],
        out_specs=[_tok_spec(tm), _tok_spec(tm), _tok_spec(tm), cache_spec, cache_spec],
        out_shape=[
            jax.ShapeDtypeStruct((N_TOK, D_MODEL), BF16),
            jax.ShapeDtypeStruct((N_TOK, D_MODEL), BF16),
            jax.ShapeDtypeStruct((N_TOK, D_MODEL), BF16),
            jax.ShapeDtypeStruct((P_TOK * N_HEADS, HEAD_DIM), F32),
            jax.ShapeDtypeStruct((P_TOK * N_HEADS, HEAD_DIM), F32),
        ],
        scratch_shapes=[pltpu.VMEM((D_MODEL, 3 * D_MODEL), BF16)],
        compiler_params=_params(),
        name="qkv",
    )(x, mods, norm_g, w_qkv,
      jnp.tile(q_gain[j], N_HEADS).reshape(1, D_MODEL), jnp.tile(k_gain[j], N_HEADS).reshape(1, D_MODEL),
      head_mean)


def _cache_rows_kernel(ck_ref, cv_ref, ok_ref, ov_ref):
    for hd in range(N_HEADS):
        rows = pl.ds(hd, PAST_LEN, stride=N_HEADS)
        cols = slice(hd * HEAD_DIM, (hd + 1) * HEAD_DIM)
        ok_ref[:, cols] = ck_ref[rows, :].astype(BF16)
        ov_ref[:, cols] = cv_ref[rows, :].astype(BF16)


def _cache_rows(cache_k, cache_v, j):
    n_layers = cache_k.shape[1]
    flat = (DEC_BATCH, n_layers, PAST_LEN * N_HEADS, HEAD_DIM)
    in_spec = pl.BlockSpec((None, None, PAST_LEN * N_HEADS, HEAD_DIM), lambda b: (b, j, 0, 0))
    out_spec = pl.BlockSpec((None, PAST_LEN, D_MODEL), lambda b: (b, 0, 0))
    out = jax.ShapeDtypeStruct((DEC_BATCH, PAST_LEN, D_MODEL), BF16)
    return pl.pallas_call(
        _cache_rows_kernel,
        grid=(DEC_BATCH,),
        in_specs=[in_spec, in_spec],
        out_specs=[out_spec, out_spec],
        out_shape=[out, out],
        compiler_params=_params(),
        name="cache_rows",
    )(cache_k.reshape(flat), cache_v.reshape(flat))


NBR_GROUP_ROWS = GRID_ROWS // 2
NBR_KEY_ROWS = NBR_GROUP_ROWS + WIN_ROWS // 2
NBR_Q = NBR_GROUP_ROWS * GRID_W
NBR_KEYS = NBR_KEY_ROWS * GRID_W


def _band_start(r):
    return min(max(r - WIN_ROWS // 2, 0), GRID_ROWS - WIN_ROWS)


def _key_row_start(r):
    return 0 if r < NBR_GROUP_ROWS else GRID_ROWS - NBR_KEY_ROWS


def _bias_kernel(rpb_ref, o_ref):
    h = pl.program_id(0)
    qc = lax.broadcasted_iota(jnp.int32, (GRID_W, GRID_W), 0)
    kc = lax.broadcasted_iota(jnp.int32, (GRID_W, GRID_W), 1)
    rel_c = jnp.clip(kc - qc + (WIN_COLS - 1), 0, RPB_COLS - 1)
    col_start = jnp.clip(qc - WIN_COLS // 2, 0, GRID_W - WIN_COLS)
    col_ok = (kc >= col_start) & (kc < col_start + WIN_COLS)
    base = h * (RPB_ROWS * RPB_COLS)
    row_tiles = []
    for rel_r in range(RPB_ROWS):
        tile = jnp.zeros((GRID_W, GRID_W), F32)
        for c in range(RPB_COLS):
            tile = jnp.where(rel_c == c, rpb_ref[base + rel_r * RPB_COLS + c], tile)
        row_tiles.append(jnp.where(col_ok, tile, MASK_VALUE))
    masked = jnp.full((GRID_W, GRID_W), MASK_VALUE, F32)
    for r in range(GRID_ROWS):
        for i in range(NBR_KEY_ROWS):
            key_row = _key_row_start(r) + i
            in_window = _band_start(r) <= key_row < _band_start(r) + WIN_ROWS
            tile = row_tiles[key_row - r + WIN_ROWS - 1] if in_window else masked
            o_ref[0, r, :, i * GRID_W:(i + 1) * GRID_W] = tile


def _window_bias(rpb):
    return pl.pallas_call(
        _bias_kernel,
        grid=(N_HEADS,),
        in_specs=[pl.BlockSpec(memory_space=pltpu.SMEM)],
        out_specs=pl.BlockSpec((1, GRID_ROWS, GRID_W, NBR_KEYS), lambda h: (h, 0, 0, 0)),
        out_shape=jax.ShapeDtypeStruct((N_HEADS, GRID_ROWS, GRID_W, NBR_KEYS), F32),
        compiler_params=_params(),
        name="window_bias",
    )(rpb.reshape(N_HEADS * RPB_ROWS * RPB_COLS))


def _head_lane_mask(half):
    lane = lax.broadcasted_iota(jnp.int32, (1, HEAD_PAIR), 1)
    return (lane >= HEAD_DIM) if half else (lane < HEAD_DIM)


def _scaled_head_queries(q2, sel):
    return jnp.where(sel, q2 * ATT_SCALE, jnp.zeros_like(q2))


def _ctx_attn_kernel(q_ref, k_ref, v_ref, o_ref):
    for p in range(N_PAIRS):
        cols = slice(p * HEAD_PAIR, (p + 1) * HEAD_PAIR)
        q2 = q_ref[:, cols]
        k2 = k_ref[:, cols]
        v2 = v_ref[:, cols]
        out = None
        for half in range(2):
            sel = _head_lane_mask(half)
            s = _dot_nt(_scaled_head_queries(q2, sel), k2)
            e = jnp.exp(s - jnp.max(s, axis=-1, keepdims=True))
            den = jnp.sum(e, axis=-1, keepdims=True)
            o = _dot(e.astype(BF16), v2) / den
            out = o if out is None else jnp.where(sel, o, out)
        o_ref[:, cols] = out.astype(BF16)


def _ctx_attention(q, k, v):
    spec = pl.BlockSpec((SEQ, D_MODEL), lambda b: (b, 0))
    return pl.pallas_call(
        _ctx_attn_kernel,
        grid=(BATCH,),
        in_specs=[spec, spec, spec],
        out_specs=spec,
        out_shape=jax.ShapeDtypeStruct((P_TOK, D_MODEL), BF16),
        compiler_params=_params(),
        name="ctx_attention",
    )(q, k, v)


def _nbr_attn_kernel(q_ref, k_ref, v_ref, ck_ref, cv_ref, bias_ref, o_ref):
    ck2 = ck_ref[0]
    cv2 = cv_ref[0]
    for g in range(GRID_ROWS // NBR_GROUP_ROWS):
        r0 = g * NBR_GROUP_ROWS
        rows = slice(r0 * GRID_W, r0 * GRID_W + NBR_Q)
        key_lo = _key_row_start(r0) * GRID_W
        q2 = q_ref[rows, :]
        k_band = k_ref[key_lo:key_lo + NBR_KEYS, :]
        v_band = v_ref[key_lo:key_lo + NBR_KEYS, :]
        out = None
        for half in range(2):
            sel = _head_lane_mask(half)
            qm = _scaled_head_queries(q2, sel)
            bias = bias_ref[half, r0:r0 + NBR_GROUP_ROWS].reshape(NBR_Q, NBR_KEYS)
            s_loc = _dot_nt(qm, k_band) + bias
            s_ctx = _dot_nt(qm, ck2)
            m = jnp.maximum(jnp.max(s_loc, axis=-1, keepdims=True), jnp.max(s_ctx, axis=-1, keepdims=True))
            e_loc = jnp.exp(s_loc - m)
            e_ctx = jnp.exp(s_ctx - m)
            den = jnp.sum(e_loc, axis=-1, keepdims=True) + jnp.sum(e_ctx, axis=-1, keepdims=True)
            o = (_dot(e_loc.astype(BF16), v_band) + _dot(e_ctx.astype(BF16), cv2)) / den
            out = o if out is None else jnp.where(sel, o, out)
        o_ref[rows, :] = out.astype(BF16)


def _nbr_attention(q, k, v, ck, cv, bias):
    first = P_TOK // DEC_SEQ
    tok = pl.BlockSpec((DEC_SEQ, HEAD_PAIR), lambda p, b: (first + b, p))
    ctx = pl.BlockSpec((1, PAST_LEN, HEAD_PAIR), lambda p, b: (b, 0, p))
    return pl.pallas_call(
        _nbr_attn_kernel,
        grid=(N_PAIRS, DEC_BATCH),
        in_specs=[tok, tok, tok, ctx, ctx,
                  pl.BlockSpec((2, GRID_ROWS, GRID_W, NBR_KEYS), lambda p, b: (p, 0, 0, 0))],
        out_specs=pl.BlockSpec((DEC_SEQ, HEAD_PAIR), lambda p, b: (b, p)),
        out_shape=jax.ShapeDtypeStruct((S_TOK, D_MODEL), BF16),
        compiler_params=_params(2),
        name="nbr_attention",
    )(q, k, v, ck, cv, bias)


def _proj_ffn_kernel(x_ref, mod_ref, ap_ref, as_ref, wo_ref, g_ref, w1_ref, w2_ref, o_ref,
                     wo_s, w1_s, w2_s):
    step = pl.program_id(0)

    @pl.when(step < LOAD_STEPS)
    def _():
        _load_chunk(step, wo_ref, wo_s)
        _load_chunk(step, w1_ref, w1_s)
        _load_chunk(step, w2_ref, w2_s)

    @pl.when(step >= LOAD_STEPS)
    def _():
        a = jnp.where(_is_prompt_tile(step, FFN_TM), ap_ref[...], as_ref[...])
        x = x_ref[...] + mod_ref[0][2:3] * _dot(a, wo_s[...])
        o_ref[...] = _ffn_body(x, mod_ref, g_ref, w1_s, w2_s)


def _attn_proj_ffn(x, mods, layer, norm_g, attn_prompt, attn_sample, j, w_o, w1, w2):
    tm = FFN_TM
    return pl.pallas_call(
        _proj_ffn_kernel,
        grid=(LOAD_STEPS + N_TOK // tm,),
        in_specs=[
            _tok_spec(tm),
            _mod_spec(layer, tm),
            _prompt_tok_spec(tm),
            _sample_tok_spec(tm),
            _chunk_spec(j, D_MODEL, D_MODEL),
            _layer_spec(2 * layer + 1, (1, D_MODEL)),
            _chunk_spec(layer, D_MODEL, D_FF),
            _chunk_spec(layer, D_FF, D_MODEL),
        ],
        out_specs=_tok_spec(tm),
        out_shape=jax.ShapeDtypeStruct((N_TOK, D_MODEL), F32),
        scratch_shapes=[
            pltpu.VMEM((D_MODEL, D_MODEL), BF16),
            pltpu.VMEM((D_MODEL, D_FF), BF16),
            pltpu.VMEM((D_FF, D_MODEL), BF16),
        ],
        compiler_params=_params(),
        name="attn_proj_ffn",
    )(x, mods, attn_prompt, attn_sample, w_o, norm_g, w1, w2)


def kernel(x_prompt, x_sample, cache_k, cache_v, c, c_ctx, norm_g, ada_w, ada_b, a_w_in, a_v_gain, a_ws, a_bs, a_w_out, b_w_qkv, b_q_gain, b_k_gain, b_rpb, b_w_o, c_w_in, c_conv_w, c_conv_b, c_w_out, ff_w1, ff_w2):
    n_a = a_w_in.shape[0]
    cond = jnp.concatenate(
        [c, c_ctx[None, :], jnp.zeros((COND_ROWS - DEC_BATCH - 1, D_MODEL), F32)], axis=0)
    mods = _adaln(cond, ada_w, ada_b).reshape(DEPTH, COND_ROWS, 6, D_MODEL)
    norm_g = norm_g.reshape(2 * DEPTH, 1, D_MODEL)
    a_v_gain = a_v_gain.reshape(n_a, 1, A_HALF)
    a_bs = a_bs.reshape(n_a, A_GROUPS, CHUNK, 1)
    c_conv_b = c_conv_b.reshape(-1, 1, D_MODEL)
    x = (x_prompt.reshape(P_TOK, D_MODEL), x_sample.reshape(S_TOK, D_MODEL))
    new_k, new_v = [], []
    for i in range(DEPTH):
        kind, j = i % 3, i // 3
        if kind == 0:
            x = _gmlp(x, mods, i, norm_g, j, a_w_in, a_v_gain, a_ws, a_bs, a_w_out)
        elif kind == 1:
            q, k, v, k_new, v_new = _qkv(x, mods, i, norm_g, j, b_w_qkv, b_q_gain, b_k_gain)
            new_k.append(k_new.reshape(BATCH, SEQ, N_HEADS, HEAD_DIM))
            new_v.append(v_new.reshape(BATCH, SEQ, N_HEADS, HEAD_DIM))
            o_prompt = _ctx_attention(q, k, v)
            bias = _window_bias(b_rpb[j])
            ctx_k, ctx_v = _cache_rows(cache_k, cache_v, j)
            o_sample = _nbr_attention(q, k, v, ctx_k, ctx_v, bias)
            x = _attn_proj_ffn(x, mods, i, norm_g, o_prompt, o_sample, j, b_w_o, ff_w1, ff_w2)
            continue
        else:
            x = _conv(x, mods, i, norm_g, j, c_w_in, c_conv_w, c_conv_b, c_w_out)
        x = _ffn(x, mods, i, norm_g, ff_w1, ff_w2, split_out=(i == DEPTH - 1))
    y_prompt = x[0].reshape(BATCH, SEQ, D_MODEL)
    y_sample = x[1].reshape(DEC_BATCH, DEC_SEQ, D_MODEL)
    return (y_prompt, y_sample, jnp.stack(new_k, axis=1), jnp.stack(new_v, axis=1))
```

```python
import jax
import jax.numpy as jnp
import numpy as np
from jax import lax
from jax.experimental import pallas as pl
from jax.experimental.pallas import tpu as pltpu

D_MODEL = 1024
BATCH = 16
SEQ = 256
DEPTH = 4
DEC_BATCH = 8
DEC_SEQ = 1024
PAST_LEN = 512
GRID_W = 64
GRID_ROWS = DEC_SEQ // GRID_W
CHUNK = 128
A_HALF = 2 * D_MODEL
A_GROUPS = 8
A_GROUP_W = A_HALF // A_GROUPS
N_HEADS = 16
HEAD_DIM = D_MODEL // N_HEADS
WIN_ROWS = 8
WIN_COLS = 16
RPB_ROWS = 2 * WIN_ROWS - 1
RPB_COLS = 2 * WIN_COLS - 1
D_FF = 4 * D_MODEL
EPS = 1e-6
ATT_SCALE = HEAD_DIM ** -0.5
MASK_VALUE = -1e30

P_TOK = BATCH * SEQ
S_TOK = DEC_BATCH * DEC_SEQ
N_TOK = P_TOK + S_TOK
COND_ROWS = 16
CTX_ROW = DEC_BATCH
LOAD_STEPS = 8
HEAD_PAIR = 2 * HEAD_DIM
N_PAIRS = N_HEADS // 2
MXU_TILE = 256

F32 = jnp.float32
BF16 = jnp.bfloat16
VMEM_LIMIT = 56 * 1024 * 1024


def _params(n_axes=1, vmem=VMEM_LIMIT):
    return pltpu.CompilerParams(dimension_semantics=("arbitrary",) * n_axes, vmem_limit_bytes=vmem)


def _dot(a, b):
    return jnp.dot(a, b, preferred_element_type=F32)


def _dot_nt(a, b):
    return lax.dot_general(a, b, (((1,), (1,)), ((), ())), preferred_element_type=F32)


def _rms(x, g):
    return x * lax.rsqrt(jnp.mean(x * x, axis=-1, keepdims=True) + EPS) * g


def _gelu(x):
    return 0.5 * x * (1.0 + jnp.tanh(0.7978845608028654 * (x + 0.044715 * (x * x * x))))


def _tok_spec(tm, width=D_MODEL):
    return pl.BlockSpec((tm, width), lambda s: (jnp.maximum(s - LOAD_STEPS, 0), 0))


def _prompt_tok_spec(tm):
    last = P_TOK // tm - 1
    return pl.BlockSpec((tm, D_MODEL), lambda s: (jnp.clip(s - LOAD_STEPS, 0, last), 0))


def _sample_tok_spec(tm):
    first, last = P_TOK // tm, S_TOK // tm - 1
    return pl.BlockSpec((tm, D_MODEL), lambda s: (jnp.clip(s - LOAD_STEPS - first, 0, last), 0))


def _is_prompt_tile(step, tm):
    return step - LOAD_STEPS < P_TOK // tm


def _mod_spec(layer, tm):
    n_prompt_tiles = P_TOK // tm
    tiles_per_seq = DEC_SEQ // tm

    def index(s):
        t = jnp.maximum(s - LOAD_STEPS, 0)
        row = jnp.where(t < n_prompt_tiles, CTX_ROW, jnp.maximum(t - n_prompt_tiles, 0) // tiles_per_seq)
        return (layer, row, 0, 0)

    return pl.BlockSpec((None, 1, 6, D_MODEL), index)


def _chunk_spec(layer, rows, cols):
    return pl.BlockSpec((None, rows // LOAD_STEPS, cols),
                        lambda s: (layer, jnp.minimum(s, LOAD_STEPS - 1), 0))


def _layer_spec(layer, shape):
    zeros = (0,) * len(shape)
    return pl.BlockSpec((None,) + tuple(shape), lambda s: (layer,) + zeros)


def _const_spec(shape):
    zeros = (0,) * len(shape)
    return pl.BlockSpec(shape, lambda s: zeros)


def _load_chunk(step, w_ref, w_scr):
    rows = w_ref.shape[0]
    off = pl.multiple_of(step * rows, rows)
    w_scr[pl.ds(off, rows), :] = w_ref[...].astype(BF16)


ADA_TN = 1024


def _ada_kernel(c_ref, w_ref, b_ref, o_ref):
    c = c_ref[...]
    a = c * (1.0 / (1.0 + jnp.exp(-c)))
    a_hi = a.astype(BF16)
    a_lo = (a - a_hi.astype(F32)).astype(BF16)
    w = w_ref[0]
    w_hi = w.astype(BF16)
    w_lo = (w - w_hi.astype(F32)).astype(BF16)
    o_ref[0] = _dot(a_hi, w_hi) + _dot(a_hi, w_lo) + _dot(a_lo, w_hi) + b_ref[0]


def _adaln(cond, ada_w, ada_b):
    n_out = 6 * D_MODEL
    return pl.pallas_call(
        _ada_kernel,
        grid=(DEPTH, n_out // ADA_TN),
        in_specs=[
            pl.BlockSpec((COND_ROWS, D_MODEL), lambda i, j: (0, 0)),
            pl.BlockSpec((1, D_MODEL, ADA_TN), lambda i, j: (i, 0, j)),
            pl.BlockSpec((1, 1, ADA_TN), lambda i, j: (i, 0, j)),
        ],
        out_specs=pl.BlockSpec((1, COND_ROWS, ADA_TN), lambda i, j: (i, 0, j)),
        out_shape=jax.ShapeDtypeStruct((DEPTH, COND_ROWS, n_out), F32),
        compiler_params=_params(2),
        name="adaln",
    )(cond, ada_w, ada_b.reshape(DEPTH, 1, n_out))


GMLP_TM = 512
GMLP_TC = 1024


def _gmlp_body(x, mod_ref, g_ref, vg_ref, ws_ref, bs_ref, o_ref, win_s, wout_s, v_s):
    mod = mod_ref[0]
    h = (_rms(x, g_ref[...]) * (1.0 + mod[1:2]) + mod[0:1]).astype(BF16)
    ssq = jnp.zeros((GMLP_TM, 1), F32)
    for c in range(0, A_HALF, GMLP_TC):
        cols = slice(c, c + GMLP_TC)
        v_c = _gelu(_dot(h, win_s[:, A_HALF + c:A_HALF + c + GMLP_TC]))
        ssq = ssq + jnp.sum(v_c * v_c, axis=-1, keepdims=True)
        v_s[:, cols] = v_c
    inv_rms = lax.rsqrt(ssq * (1.0 / A_HALF) + EPS)
    acc = jnp.zeros((GMLP_TM, D_MODEL), F32)
    for c in range(0, A_HALF, GMLP_TC):
        cols = slice(c, c + GMLP_TC)
        v_n = (v_s[:, cols] * inv_rms * vg_ref[:, cols]).astype(BF16)
        u_c = _gelu(_dot(h, win_s[:, cols]))
        gated_cols = []
        for g in range(GMLP_TC // A_GROUP_W):
            gcols = slice(g * A_GROUP_W, (g + 1) * A_GROUP_W)
            w_g = ws_ref[c // A_GROUP_W + g].astype(BF16)
            b_g = bs_ref[c // A_GROUP_W + g]
            gated = []
            for n in range(GMLP_TM // CHUNK):
                rows = slice(n * CHUNK, (n + 1) * CHUNK)
                s = _dot(w_g, v_n[rows, gcols]) + b_g
                gated.append((u_c[rows, gcols] * s).astype(BF16))
            gated_cols.append(jnp.concatenate(gated, axis=0))
        acc = acc + _dot(jnp.concatenate(gated_cols, axis=1), wout_s[cols, :])
    o_ref[...] = x + mod[2:3] * acc


def _gmlp_kernel(x_ref, mod_ref, g_ref, win_ref, vg_ref, ws_ref, bs_ref, wout_ref, o_ref,
                 win_s, wout_s, v_s):
    step = pl.program_id(0)

    @pl.when(step < LOAD_STEPS)
    def _():
        _load_chunk(step, win_ref, win_s)
        _load_chunk(step, wout_ref, wout_s)

    @pl.when(step >= LOAD_STEPS)
    def _():
        _gmlp_body(x_ref[...], mod_ref, g_ref, vg_ref, ws_ref, bs_ref, o_ref, win_s, wout_s, v_s)


def _gmlp_split_in_kernel(xp_ref, xs_ref, mod_ref, g_ref, win_ref, vg_ref, ws_ref, bs_ref, wout_ref,
                          o_ref, win_s, wout_s, v_s):
    step = pl.program_id(0)

    @pl.when(step < LOAD_STEPS)
    def _():
        _load_chunk(step, win_ref, win_s)
        _load_chunk(step, wout_ref, wout_s)

    @pl.when(step >= LOAD_STEPS)
    def _():
        x = jnp.where(_is_prompt_tile(step, GMLP_TM), xp_ref[...], xs_ref[...])
        _gmlp_body(x, mod_ref, g_ref, vg_ref, ws_ref, bs_ref, o_ref, win_s, wout_s, v_s)


def _gmlp(xs, mods, layer, norm_g, j, w_in, v_gain, ws, bs, w_out):
    tm = GMLP_TM
    split_in = isinstance(xs, tuple)
    x_specs = [_prompt_tok_spec(tm), _sample_tok_spec(tm)] if split_in else [_tok_spec(tm)]
    x_args = list(xs) if split_in else [xs]
    return pl.pallas_call(
        _gmlp_split_in_kernel if split_in else _gmlp_kernel,
        grid=(LOAD_STEPS + N_TOK // tm,),
        in_specs=x_specs + [
            _mod_spec(layer, tm),
            _layer_spec(2 * layer, (1, D_MODEL)),
            _chunk_spec(j, D_MODEL, 2 * A_HALF),
            _layer_spec(j, (1, A_HALF)),
            _layer_spec(j, (A_GROUPS, CHUNK, CHUNK)),
            _layer_spec(j, (A_GROUPS, CHUNK, 1)),
            _chunk_spec(j, A_HALF, D_MODEL),
        ],
        out_specs=_tok_spec(tm),
        out_shape=jax.ShapeDtypeStruct((N_TOK, D_MODEL), F32),
        scratch_shapes=[
            pltpu.VMEM((D_MODEL, 2 * A_HALF), BF16),
            pltpu.VMEM((A_HALF, D_MODEL), BF16),
            pltpu.VMEM((tm, A_HALF), F32),
        ],
        compiler_params=_params(),
        name="gmlp",
    )(*x_args, mods, norm_g, w_in, v_gain, ws, bs, w_out)


FFN_TM = 512
FFN_TC = 1024


def _ffn_body(x, mod_ref, g_ref, w1_s, w2_s):
    mod = mod_ref[0]
    h = (_rms(x, g_ref[...]) * (1.0 + mod[4:5]) + mod[3:4]).astype(BF16)
    acc = jnp.zeros((FFN_TM, D_MODEL), F32)
    for c in range(D_FF // FFN_TC):
        cols = slice(c * FFN_TC, (c + 1) * FFN_TC)
        hid = jnp.square(jnp.maximum(_dot(h, w1_s[:, cols]), 0.0)).astype(BF16)
        acc = acc + _dot(hid, w2_s[cols, :])
    return x + mod[5:6] * acc


def _ffn_kernel(x_ref, mod_ref, g_ref, w1_ref, w2_ref, o_ref, w1_s, w2_s):
    step = pl.program_id(0)

    @pl.when(step < LOAD_STEPS)
    def _():
        _load_chunk(step, w1_ref, w1_s)
        _load_chunk(step, w2_ref, w2_s)

    @pl.when(step >= LOAD_STEPS)
    def _():
        o_ref[...] = _ffn_body(x_ref[...], mod_ref, g_ref, w1_s, w2_s)


def _ffn_split_out_kernel(x_ref, mod_ref, g_ref, w1_ref, w2_ref, op_ref, os_ref, w1_s, w2_s):
    step = pl.program_id(0)

    @pl.when(step < LOAD_STEPS)
    def _():
        _load_chunk(step, w1_ref, w1_s)
        _load_chunk(step, w2_ref, w2_s)

    is_prompt = _is_prompt_tile(step, FFN_TM)

    @pl.when((step >= LOAD_STEPS) & is_prompt)
    def _():
        op_ref[...] = _ffn_body(x_ref[...], mod_ref, g_ref, w1_s, w2_s)

    @pl.when(jnp.logical_not(is_prompt))
    def _():
        os_ref[...] = _ffn_body(x_ref[...], mod_ref, g_ref, w1_s, w2_s)


def _ffn(x, mods, layer, norm_g, w1, w2, split_out=False):
    tm = FFN_TM
    if split_out:
        out_specs = [_prompt_tok_spec(tm), _sample_tok_spec(tm)]
        out_shape = [jax.ShapeDtypeStruct((P_TOK, D_MODEL), F32), jax.ShapeDtypeStruct((S_TOK, D_MODEL), F32)]
    else:
        out_specs = _tok_spec(tm)
        out_shape = jax.ShapeDtypeStruct((N_TOK, D_MODEL), F32)
    return pl.pallas_call(
        _ffn_split_out_kernel if split_out else _ffn_kernel,
        grid=(LOAD_STEPS + N_TOK // tm,),
        in_specs=[
            _tok_spec(tm),
            _mod_spec(layer, tm),
            _layer_spec(2 * layer + 1, (1, D_MODEL)),
            _chunk_spec(layer, D_MODEL, D_FF),
            _chunk_spec(layer, D_FF, D_MODEL),
        ],
        out_specs=out_specs,
        out_shape=out_shape,
        scratch_shapes=[
            pltpu.VMEM((D_MODEL, D_FF), BF16),
            pltpu.VMEM((D_FF, D_MODEL), BF16),
        ],
        compiler_params=_params(),
        name="ffn",
    )(x, mods, norm_g, w1, w2)


CONV_TM = 1024
CONV_TC = 512


def _conv_kernel(x_ref, mod_ref, g_ref, win_ref, cw_ref, cb_ref, wout_ref, o_ref, win_s, wout_s):
    step = pl.program_id(0)

    @pl.when(step < LOAD_STEPS)
    def _():
        _load_chunk(step, win_ref, win_s)
        _load_chunk(step, wout_ref, wout_s)

    @pl.when(step >= LOAD_STEPS)
    def _():
        seq_len = jnp.where(_is_prompt_tile(step, CONV_TM), SEQ, DEC_SEQ)
        pos = lax.broadcasted_iota(jnp.int32, (CONV_TM, 1), 0) & (seq_len - 1)
        has_prev = pos != 0
        has_next = pos != seq_len - 1
        x = x_ref[...]
        mod = mod_ref[0]
        h = (_rms(x, g_ref[...]) * (1.0 + mod[1:2]) + mod[0:1]).astype(BF16)
        cw = cw_ref[...]
        cb = cb_ref[...]
        acc = jnp.zeros((CONV_TM, D_MODEL), F32)
        for c in range(D_MODEL // CONV_TC):
            lo = c * CONV_TC
            cols = slice(lo, lo + CONV_TC)
            bg = _dot(h, win_s[:, lo:lo + CONV_TC])
            cg = _dot(h, win_s[:, D_MODEL + lo:D_MODEL + lo + CONV_TC])
            xt = _dot(h, win_s[:, 2 * D_MODEL + lo:2 * D_MODEL + lo + CONV_TC])
            z = cg * xt
            z_prev = jnp.where(has_prev, pltpu.roll(z, 1, axis=0), 0.0)
            z_next = jnp.where(has_next, pltpu.roll(z, CONV_TM - 1, axis=0), 0.0)
            zc = cw[0:1, cols] * z_prev + cw[1:2, cols] * z + cw[2:3, cols] * z_next + cb[:, cols]
            acc = acc + _dot((bg * zc).astype(BF16), wout_s[cols, :])
        o_ref[...] = x + mod[2:3] * acc


def _conv(x, mods, layer, norm_g, j, w_in, conv_w, conv_b, w_out):
    tm = CONV_TM
    return pl.pallas_call(
        _conv_kernel,
        grid=(LOAD_STEPS + N_TOK // tm,),
        in_specs=[
            _tok_spec(tm),
            _mod_spec(layer, tm),
            _layer_spec(2 * layer, (1, D_MODEL)),
            _chunk_spec(j, D_MODEL, 3 * D_MODEL),
            _layer_spec(j, (3, D_MODEL)),
            _layer_spec(j, (1, D_MODEL)),
            _chunk_spec(j, D_MODEL, D_MODEL),
        ],
        out_specs=_tok_spec(tm),
        out_shape=jax.ShapeDtypeStruct((N_TOK, D_MODEL), F32),
        scratch_shapes=[
            pltpu.VMEM((D_MODEL, 3 * D_MODEL), BF16),
            pltpu.VMEM((D_MODEL, D_MODEL), BF16),
        ],
        compiler_params=_params(),
        name="sconv",
    )(x, mods, norm_g, w_in, conv_w, conv_b, w_out)


QKV_TM = 512


def _qkv_kernel(x_ref, mod_ref, g_ref, w_ref, gq_ref, gk_ref, hm_ref,
                q_ref, k_ref, v_ref, kc_ref, vc_ref, w_s):
    step = pl.program_id(0)

    @pl.when(step < LOAD_STEPS)
    def _():
        _load_chunk(step, w_ref, w_s)

    @pl.when(step >= LOAD_STEPS)
    def _():
        x = x_ref[...]
        mod = mod_ref[0]
        h = (_rms(x, g_ref[...]) * (1.0 + mod[1:2]) + mod[0:1]).astype(BF16)
        head_mean = hm_ref[...]

        def head_norm(y, gain):
            sq = (y * y).astype(BF16)
            ms = jnp.concatenate(
                [_dot(sq[:, c:c + MXU_TILE], head_mean) for c in range(0, D_MODEL, MXU_TILE)], axis=1)
            return y * lax.rsqrt(ms + EPS) * gain

        q = head_norm(_dot(h, w_s[:, :D_MODEL]), gq_ref[...])
        k = head_norm(_dot(h, w_s[:, D_MODEL:2 * D_MODEL]), gk_ref[...])
        v = _dot(h, w_s[:, 2 * D_MODEL:])
        q_ref[...] = q.astype(BF16)
        k_ref[...] = k.astype(BF16)
        v_ref[...] = v.astype(BF16)

        @pl.when(_is_prompt_tile(step, QKV_TM))
        def _():
            kc_ref[...] = k
            vc_ref[...] = v


def _qkv(x, mods, layer, norm_g, j, w_qkv, q_gain, k_gain):
    tm = QKV_TM
    head_mean = jnp.asarray(
        np.kron(np.eye(MXU_TILE // HEAD_DIM), np.full((HEAD_DIM, HEAD_DIM), 1.0 / HEAD_DIM)), BF16)
    return pl.pallas_call(
        _qkv_kernel,
        grid=(LOAD_STEPS + N_TOK // tm,),
        in_specs=[
            _tok_spec(tm),
            _mod_spec(layer, tm),
            _layer_spec(2 * layer, (1, D_MODEL)),
            _chunk_spec(j, D_MODEL, 3 * D_MODEL),
            _const_spec((1, D_MODEL)),
            _const_spec((1, D_MODEL)),
            _const_spec((MXU_TILE, MXU_TILE)),
        ],
        out_specs=[_tok_spec(tm), _tok_spec(tm), _tok_spec(tm), _prompt_tok_spec(tm), _prompt_tok_spec(tm)],
        out_shape=[
            jax.ShapeDtypeStruct((N_TOK, D_MODEL), BF16),
            jax.ShapeDtypeStruct((N_TOK, D_MODEL), BF16),
            jax.ShapeDtypeStruct((N_TOK, D_MODEL), BF16),
            jax.ShapeDtypeStruct((P_TOK, D_MODEL), F32),
            jax.ShapeDtypeStruct((P_TOK, D_MODEL), F32),
        ],
        scratch_shapes=[pltpu.VMEM((D_MODEL, 3 * D_MODEL), BF16)],
        compiler_params=_params(),
        name="qkv",
    )(x, mods, norm_g, w_qkv,
      jnp.tile(q_gain[j], N_HEADS).reshape(1, D_MODEL), jnp.tile(k_gain[j], N_HEADS).reshape(1, D_MODEL),
      head_mean)


NBR_GROUP_ROWS = GRID_ROWS // 2
NBR_KEY_ROWS = NBR_GROUP_ROWS + WIN_ROWS // 2
NBR_Q = NBR_GROUP_ROWS * GRID_W
NBR_KEYS = NBR_KEY_ROWS * GRID_W


def _band_start(r):
    return min(max(r - WIN_ROWS // 2, 0), GRID_ROWS - WIN_ROWS)


def _key_row_start(r):
    return 0 if r < NBR_GROUP_ROWS else GRID_ROWS - NBR_KEY_ROWS


def _bias_kernel(rpb_ref, o_ref):
    h = pl.program_id(0)
    qc = lax.broadcasted_iota(jnp.int32, (GRID_W, GRID_W), 0)
    kc = lax.broadcasted_iota(jnp.int32, (GRID_W, GRID_W), 1)
    rel_c = jnp.clip(kc - qc + (WIN_COLS - 1), 0, RPB_COLS - 1)
    col_start = jnp.clip(qc - WIN_COLS // 2, 0, GRID_W - WIN_COLS)
    col_ok = (kc >= col_start) & (kc < col_start + WIN_COLS)
    base = h * (RPB_ROWS * RPB_COLS)
    row_tiles = []
    for rel_r in range(RPB_ROWS):
        tile = jnp.zeros((GRID_W, GRID_W), F32)
        for c in range(RPB_COLS):
            tile = jnp.where(rel_c == c, rpb_ref[base + rel_r * RPB_COLS + c], tile)
        row_tiles.append(jnp.where(col_ok, tile, MASK_VALUE))
    masked = jnp.full((GRID_W, GRID_W), MASK_VALUE, F32)
    for r in range(GRID_ROWS):
        for i in range(NBR_KEY_ROWS):
            key_row = _key_row_start(r) + i
            in_window = _band_start(r) <= key_row < _band_start(r) + WIN_ROWS
            tile = row_tiles[key_row - r + WIN_ROWS - 1] if in_window else masked
            o_ref[0, r, :, i * GRID_W:(i + 1) * GRID_W] = tile


def _window_bias(rpb):
    return pl.pallas_call(
        _bias_kernel,
        grid=(N_HEADS,),
        in_specs=[pl.BlockSpec(memory_space=pltpu.SMEM)],
        out_specs=pl.BlockSpec((1, GRID_ROWS, GRID_W, NBR_KEYS), lambda h: (h, 0, 0, 0)),
        out_shape=jax.ShapeDtypeStruct((N_HEADS, GRID_ROWS, GRID_W, NBR_KEYS), F32),
        compiler_params=_params(),
        name="window_bias",
    )(rpb.reshape(N_HEADS * RPB_ROWS * RPB_COLS))


def _head_lane_mask(half):
    lane = lax.broadcasted_iota(jnp.int32, (1, HEAD_PAIR), 1)
    return (lane >= HEAD_DIM) if half else (lane < HEAD_DIM)


def _scaled_head_queries(q2, sel):
    return jnp.where(sel, q2 * ATT_SCALE, jnp.zeros_like(q2))


def _ctx_attn_kernel(q_ref, k_ref, v_ref, o_ref):
    for p in range(N_PAIRS):
        cols = slice(p * HEAD_PAIR, (p + 1) * HEAD_PAIR)
        q2 = q_ref[:, cols]
        k2 = k_ref[:, cols]
        v2 = v_ref[:, cols]
        out = None
        for half in range(2):
            sel = _head_lane_mask(half)
            s = _dot_nt(_scaled_head_queries(q2, sel), k2)
            e = jnp.exp(s - jnp.max(s, axis=-1, keepdims=True))
            den = jnp.sum(e, axis=-1, keepdims=True)
            o = _dot(e.astype(BF16), v2) / den
            out = o if out is None else jnp.where(sel, o, out)
        o_ref[:, cols] = out.astype(BF16)


def _ctx_attention(q, k, v):
    spec = pl.BlockSpec((SEQ, D_MODEL), lambda b: (b, 0))
    return pl.pallas_call(
        _ctx_attn_kernel,
        grid=(BATCH,),
        in_specs=[spec, spec, spec],
        out_specs=spec,
        out_shape=jax.ShapeDtypeStruct((P_TOK, D_MODEL), BF16),
        compiler_params=_params(),
        name="ctx_attention",
    )(q, k, v)


NBR_SOFTMAX_ROWS = 32


def _nbr_attn_kernel(q_ref, k_ref, v_ref, ck_ref, cv_ref, bias_ref, o_ref):
    ck2 = ck_ref[0].astype(BF16)
    cv2 = cv_ref[0].astype(BF16)
    blocks = [(g, half) for g in range(GRID_ROWS // NBR_GROUP_ROWS) for half in range(2)]

    def scores(g, half):
        r0 = g * NBR_GROUP_ROWS
        key_lo = _key_row_start(r0) * GRID_W
        qm = _scaled_head_queries(q_ref[r0 * GRID_W:r0 * GRID_W + NBR_Q, :], _head_lane_mask(half))
        return _dot_nt(qm, k_ref[key_lo:key_lo + NBR_KEYS, :]), _dot_nt(qm, ck2)

    def attend(g, half, s_loc, s_ctx):
        r0 = g * NBR_GROUP_ROWS
        key_lo = _key_row_start(r0) * GRID_W
        e_loc, e_ctx, inv_den = [], [], []
        for lo in range(0, NBR_Q, NBR_SOFTMAX_ROWS):
            hi = lo + NBR_SOFTMAX_ROWS
            r, q_lo = r0 + lo // GRID_W, lo % GRID_W
            sl = s_loc[lo:hi] + bias_ref[half, r, q_lo:q_lo + NBR_SOFTMAX_ROWS, :]
            sc = s_ctx[lo:hi]
            m = jnp.maximum(jnp.max(sl, axis=-1, keepdims=True), jnp.max(sc, axis=-1, keepdims=True))
            el = jnp.exp(sl - m)
            ec = jnp.exp(sc - m)
            inv_den.append(1.0 / (jnp.sum(el, axis=-1, keepdims=True) + jnp.sum(ec, axis=-1, keepdims=True)))
            e_loc.append(el.astype(BF16))
            e_ctx.append(ec.astype(BF16))
        o = _dot(jnp.concatenate(e_loc, axis=0), v_ref[key_lo:key_lo + NBR_KEYS, :])
        o = o + _dot(jnp.concatenate(e_ctx, axis=0), cv2)
        return o * jnp.concatenate(inv_den, axis=0)

    pending = scores(*blocks[0])
    outs = {}
    for i, (g, half) in enumerate(blocks):
        s_loc, s_ctx = pending
        if i + 1 < len(blocks):
            pending = scores(*blocks[i + 1])
        outs[(g, half)] = attend(g, half, s_loc, s_ctx)
    for g in range(GRID_ROWS // NBR_GROUP_ROWS):
        rows = slice(g * NBR_Q, (g + 1) * NBR_Q)
        o_ref[rows, :] = jnp.where(_head_lane_mask(0), outs[(g, 0)], outs[(g, 1)]).astype(BF16)


def _nbr_attention(q, k, v, ck, cv, bias):
    first = P_TOK // DEC_SEQ
    tok = pl.BlockSpec((DEC_SEQ, HEAD_PAIR), lambda p, b: (first + b, p))
    ctx = pl.BlockSpec((1, PAST_LEN, HEAD_PAIR), lambda p, b: (b, 0, p))
    return pl.pallas_call(
        _nbr_attn_kernel,
        grid=(N_PAIRS, DEC_BATCH),
        in_specs=[tok, tok, tok, ctx, ctx,
                  pl.BlockSpec((2, GRID_ROWS, GRID_W, NBR_KEYS), lambda p, b: (p, 0, 0, 0))],
        out_specs=pl.BlockSpec((DEC_SEQ, HEAD_PAIR), lambda p, b: (b, p)),
        out_shape=jax.ShapeDtypeStruct((S_TOK, D_MODEL), BF16),
        compiler_params=_params(2),
        name="nbr_attention",
    )(q, k, v, ck, cv, bias)


def _proj_ffn_kernel(x_ref, mod_ref, ap_ref, as_ref, wo_ref, g_ref, w1_ref, w2_ref, o_ref,
                     wo_s, w1_s, w2_s):
    step = pl.program_id(0)

    @pl.when(step < LOAD_STEPS)
    def _():
        _load_chunk(step, wo_ref, wo_s)
        _load_chunk(step, w1_ref, w1_s)
        _load_chunk(step, w2_ref, w2_s)

    @pl.when(step >= LOAD_STEPS)
    def _():
        a = jnp.where(_is_prompt_tile(step, FFN_TM), ap_ref[...], as_ref[...])
        x = x_ref[...] + mod_ref[0][2:3] * _dot(a, wo_s[...])
        o_ref[...] = _ffn_body(x, mod_ref, g_ref, w1_s, w2_s)


def _attn_proj_ffn(x, mods, layer, norm_g, attn_prompt, attn_sample, j, w_o, w1, w2):
    tm = FFN_TM
    return pl.pallas_call(
        _proj_ffn_kernel,
        grid=(LOAD_STEPS + N_TOK // tm,),
        in_specs=[
            _tok_spec(tm),
            _mod_spec(layer, tm),
            _prompt_tok_spec(tm),
            _sample_tok_spec(tm),
            _chunk_spec(j, D_MODEL, D_MODEL),
            _layer_spec(2 * layer + 1, (1, D_MODEL)),
            _chunk_spec(layer, D_MODEL, D_FF),
            _chunk_spec(layer, D_FF, D_MODEL),
        ],
        out_specs=_tok_spec(tm),
        out_shape=jax.ShapeDtypeStruct((N_TOK, D_MODEL), F32),
        scratch_shapes=[
            pltpu.VMEM((D_MODEL, D_MODEL), BF16),
            pltpu.VMEM((D_MODEL, D_FF), BF16),
            pltpu.VMEM((D_FF, D_MODEL), BF16),
        ],
        compiler_params=_params(),
        name="attn_proj_ffn",
    )(x, mods, attn_prompt, attn_sample, w_o, norm_g, w1, w2)


def kernel(x_prompt, x_sample, cache_k, cache_v, c, c_ctx, norm_g, ada_w, ada_b, a_w_in, a_v_gain, a_ws, a_bs, a_w_out, b_w_qkv, b_q_gain, b_k_gain, b_rpb, b_w_o, c_w_in, c_conv_w, c_conv_b, c_w_out, ff_w1, ff_w2):
    n_a = a_w_in.shape[0]
    cond = jnp.concatenate(
        [c, c_ctx[None, :], jnp.zeros((COND_ROWS - DEC_BATCH - 1, D_MODEL), F32)], axis=0)
    mods = _adaln(cond, ada_w, ada_b).reshape(DEPTH, COND_ROWS, 6, D_MODEL)
    norm_g = norm_g.reshape(2 * DEPTH, 1, D_MODEL)
    a_v_gain = a_v_gain.reshape(n_a, 1, A_HALF)
    a_bs = a_bs.reshape(n_a, A_GROUPS, CHUNK, 1)
    c_conv_b = c_conv_b.reshape(-1, 1, D_MODEL)
    x = (x_prompt.reshape(P_TOK, D_MODEL), x_sample.reshape(S_TOK, D_MODEL))
    new_k, new_v = [], []
    for i in range(DEPTH):
        kind, j = i % 3, i // 3
        if kind == 0:
            x = _gmlp(x, mods, i, norm_g, j, a_w_in, a_v_gain, a_ws, a_bs, a_w_out)
        elif kind == 1:
            q, k, v, k_new, v_new = _qkv(x, mods, i, norm_g, j, b_w_qkv, b_q_gain, b_k_gain)
            new_k.append(k_new.reshape(BATCH, SEQ, N_HEADS, HEAD_DIM))
            new_v.append(v_new.reshape(BATCH, SEQ, N_HEADS, HEAD_DIM))
            o_prompt = _ctx_attention(q, k, v)
            bias = _window_bias(b_rpb[j])
            o_sample = _nbr_attention(
                q, k, v,
                cache_k[:, j].reshape(DEC_BATCH, PAST_LEN, D_MODEL),
                cache_v[:, j].reshape(DEC_BATCH, PAST_LEN, D_MODEL), bias)
            x = _attn_proj_ffn(x, mods, i, norm_g, o_prompt, o_sample, j, b_w_o, ff_w1, ff_w2)
            continue
        else:
            x = _conv(x, mods, i, norm_g, j, c_w_in, c_conv_w, c_conv_b, c_w_out)
        x = _ffn(x, mods, i, norm_g, ff_w1, ff_w2, split_out=(i == DEPTH - 1))
    y_prompt = x[0].reshape(BATCH, SEQ, D_MODEL)
    y_sample = x[1].reshape(DEC_BATCH, DEC_SEQ, D_MODEL)
    return (y_prompt, y_sample, jnp.stack(new_k, axis=1), jnp.stack(new_v, axis=1))
```

```python
import jax
import jax.numpy as jnp
import numpy as np
from jax import lax
from jax.experimental import pallas as pl
from jax.experimental.pallas import tpu as pltpu

D_MODEL = 1024
BATCH = 16
SEQ = 256
DEPTH = 4
DEC_BATCH = 8
DEC_SEQ = 1024
PAST_LEN = 512
GRID_W = 64
GRID_ROWS = DEC_SEQ // GRID_W
CHUNK = 128
A_HALF = 2 * D_MODEL
A_GROUPS = 8
A_GROUP_W = A_HALF // A_GROUPS
N_HEADS = 16
HEAD_DIM = D_MODEL // N_HEADS
WIN_ROWS = 8
WIN_COLS = 16
RPB_ROWS = 2 * WIN_ROWS - 1
RPB_COLS = 2 * WIN_COLS - 1
D_FF = 4 * D_MODEL
EPS = 1e-6
ATT_SCALE = HEAD_DIM ** -0.5
MASK_VALUE = -1e30

P_TOK = BATCH * SEQ
S_TOK = DEC_BATCH * DEC_SEQ
N_TOK = P_TOK + S_TOK
COND_ROWS = 16
CTX_ROW = DEC_BATCH
LOAD_STEPS = 16
HEAD_PAIR = 2 * HEAD_DIM
N_PAIRS = N_HEADS // 2
MXU_TILE = 256

F32 = jnp.float32
BF16 = jnp.bfloat16
VMEM_LIMIT = 56 * 1024 * 1024


def _params(n_axes=1, vmem=VMEM_LIMIT):
    return pltpu.CompilerParams(dimension_semantics=("arbitrary",) * n_axes, vmem_limit_bytes=vmem)


def _dot(a, b):
    return jnp.dot(a, b, preferred_element_type=F32)


def _dot_nt(a, b):
    return lax.dot_general(a, b, (((1,), (1,)), ((), ())), preferred_element_type=F32)


def _rms(x, g):
    return x * lax.rsqrt(jnp.mean(x * x, axis=-1, keepdims=True) + EPS) * g


def _gelu(x):
    return 0.5 * x * (1.0 + jnp.tanh(0.7978845608028654 * (x + 0.044715 * (x * x * x))))


def _tok_spec(tm, width=D_MODEL):
    return pl.BlockSpec((tm, width), lambda s: (jnp.maximum(s - LOAD_STEPS, 0), 0))


def _prompt_tok_spec(tm):
    last = P_TOK // tm - 1
    return pl.BlockSpec((tm, D_MODEL), lambda s: (jnp.clip(s - LOAD_STEPS, 0, last), 0))


def _sample_tok_spec(tm):
    first, last = P_TOK // tm, S_TOK // tm - 1
    return pl.BlockSpec((tm, D_MODEL), lambda s: (jnp.clip(s - LOAD_STEPS - first, 0, last), 0))


def _is_prompt_tile(step, tm):
    return step - LOAD_STEPS < P_TOK // tm


def _mod_spec(layer, tm):
    n_prompt_tiles = P_TOK // tm
    tiles_per_seq = DEC_SEQ // tm

    def index(s):
        t = jnp.maximum(s - LOAD_STEPS, 0)
        row = jnp.where(t < n_prompt_tiles, CTX_ROW, jnp.maximum(t - n_prompt_tiles, 0) // tiles_per_seq)
        return (layer, row, 0, 0)

    return pl.BlockSpec((None, 1, 6, D_MODEL), index)


def _chunk_spec(layer, rows, cols):
    return pl.BlockSpec((None, rows // LOAD_STEPS, cols),
                        lambda s: (layer, jnp.minimum(s, LOAD_STEPS - 1), 0))


def _layer_spec(layer, shape):
    zeros = (0,) * len(shape)
    return pl.BlockSpec((None,) + tuple(shape), lambda s: (layer,) + zeros)


def _const_spec(shape):
    zeros = (0,) * len(shape)
    return pl.BlockSpec(shape, lambda s: zeros)


def _load_chunk(step, w_ref, w_scr):
    rows = w_ref.shape[0]
    off = pl.multiple_of(step * rows, rows)
    w_scr[pl.ds(off, rows), :] = w_ref[...].astype(BF16)


ADA_TN = 1024


def _ada_kernel(c_ref, w_ref, b_ref, o_ref):
    c = c_ref[...]
    a = c * (1.0 / (1.0 + jnp.exp(-c)))
    a_hi = a.astype(BF16)
    a_lo = (a - a_hi.astype(F32)).astype(BF16)
    w = w_ref[0]
    w_hi = w.astype(BF16)
    w_lo = (w - w_hi.astype(F32)).astype(BF16)
    o_ref[0] = _dot(a_hi, w_hi) + _dot(a_hi, w_lo) + _dot(a_lo, w_hi) + b_ref[0]


def _adaln(cond, ada_w, ada_b):
    n_out = 6 * D_MODEL
    return pl.pallas_call(
        _ada_kernel,
        grid=(DEPTH, n_out // ADA_TN),
        in_specs=[
            pl.BlockSpec((COND_ROWS, D_MODEL), lambda i, j: (0, 0)),
            pl.BlockSpec((1, D_MODEL, ADA_TN), lambda i, j: (i, 0, j)),
            pl.BlockSpec((1, 1, ADA_TN), lambda i, j: (i, 0, j)),
        ],
        out_specs=pl.BlockSpec((1, COND_ROWS, ADA_TN), lambda i, j: (i, 0, j)),
        out_shape=jax.ShapeDtypeStruct((DEPTH, COND_ROWS, n_out), F32),
        compiler_params=_params(2),
        name="adaln",
    )(cond, ada_w, ada_b.reshape(DEPTH, 1, n_out))


GMLP_TM = 512
GMLP_SUB = 512
GMLP_TC = 1024


def _gmlp_gate_half(h, win_s, v_s):
    ssq = jnp.zeros((GMLP_SUB, 1), F32)
    for c in range(0, A_HALF, GMLP_TC):
        cols = slice(c, c + GMLP_TC)
        v_c = _gelu(_dot(h, win_s[:, A_HALF + c:A_HALF + c + GMLP_TC]))
        ssq = ssq + jnp.sum(v_c * v_c, axis=-1, keepdims=True)
        v_s[:, cols] = v_c
    return lax.rsqrt(ssq * (1.0 / A_HALF) + EPS)


def _gmlp_mix(x, h, inv_rms, mod, vg_ref, ws_ref, bs_ref, win_s, wout_s, v_s):
    acc = jnp.zeros((GMLP_SUB, D_MODEL), F32)
    for c in range(0, A_HALF, GMLP_TC):
        cols = slice(c, c + GMLP_TC)
        v_n = (v_s[:, cols] * inv_rms * vg_ref[:, cols]).astype(BF16)
        u_c = _gelu(_dot(h, win_s[:, cols]))
        gated_cols = []
        for g in range(GMLP_TC // A_GROUP_W):
            gcols = slice(g * A_GROUP_W, (g + 1) * A_GROUP_W)
            w_g = ws_ref[c // A_GROUP_W + g].astype(BF16)
            b_g = bs_ref[c // A_GROUP_W + g]
            gated = []
            for n in range(GMLP_SUB // CHUNK):
                rows = slice(n * CHUNK, (n + 1) * CHUNK)
                s = _dot(w_g, v_n[rows, gcols]) + b_g
                gated.append((u_c[rows, gcols] * s).astype(BF16))
            gated_cols.append(jnp.concatenate(gated, axis=0))
        acc = acc + _dot(jnp.concatenate(gated_cols, axis=1), wout_s[cols, :])
    return x + mod[2:3] * acc


def _gmlp_body(x, mod_ref, g_ref, vg_ref, ws_ref, bs_ref, o_ref, win_s, wout_s, v_s):
    mod = mod_ref[0]
    row_sets = [slice(r, r + GMLP_SUB) for r in range(0, GMLP_TM, GMLP_SUB)]
    hs = [(_rms(x[rows], g_ref[...]) * (1.0 + mod[1:2]) + mod[0:1]).astype(BF16) for rows in row_sets]
    inv = [_gmlp_gate_half(h, win_s, v_s.at[rows]) for rows, h in zip(row_sets, hs)]
    for rows, h, inv_rms in zip(row_sets, hs, inv):
        o_ref[rows, :] = _gmlp_mix(x[rows], h, inv_rms, mod, vg_ref, ws_ref, bs_ref, win_s, wout_s,
                                   v_s.at[rows])


def _gmlp_kernel(x_ref, mod_ref, g_ref, win_ref, vg_ref, ws_ref, bs_ref, wout_ref, o_ref,
                 win_s, wout_s, v_s):
    step = pl.program_id(0)

    @pl.when(step < LOAD_STEPS)
    def _():
        _load_chunk(step, win_ref, win_s)
        _load_chunk(step, wout_ref, wout_s)

    @pl.when(step >= LOAD_STEPS)
    def _():
        _gmlp_body(x_ref[...], mod_ref, g_ref, vg_ref, ws_ref, bs_ref, o_ref, win_s, wout_s, v_s)


def _gmlp_split_in_kernel(xp_ref, xs_ref, mod_ref, g_ref, win_ref, vg_ref, ws_ref, bs_ref, wout_ref,
                          o_ref, win_s, wout_s, v_s):
    step = pl.program_id(0)

    @pl.when(step < LOAD_STEPS)
    def _():
        _load_chunk(step, win_ref, win_s)
        _load_chunk(step, wout_ref, wout_s)

    @pl.when(step >= LOAD_STEPS)
    def _():
        x = jnp.where(_is_prompt_tile(step, GMLP_TM), xp_ref[...], xs_ref[...])
        _gmlp_body(x, mod_ref, g_ref, vg_ref, ws_ref, bs_ref, o_ref, win_s, wout_s, v_s)


def _gmlp(xs, mods, layer, norm_g, j, w_in, v_gain, ws, bs, w_out):
    tm = GMLP_TM
    split_in = isinstance(xs, tuple)
    x_specs = [_prompt_tok_spec(tm), _sample_tok_spec(tm)] if split_in else [_tok_spec(tm)]
    x_args = list(xs) if split_in else [xs]
    return pl.pallas_call(
        _gmlp_split_in_kernel if split_in else _gmlp_kernel,
        grid=(LOAD_STEPS + N_TOK // tm,),
        in_specs=x_specs + [
            _mod_spec(layer, tm),
            _layer_spec(2 * layer, (1, D_MODEL)),
            _chunk_spec(j, D_MODEL, 2 * A_HALF),
            _layer_spec(j, (1, A_HALF)),
            _layer_spec(j, (A_GROUPS, CHUNK, CHUNK)),
            _layer_spec(j, (A_GROUPS, CHUNK, 1)),
            _chunk_spec(j, A_HALF, D_MODEL),
        ],
        out_specs=_tok_spec(tm),
        out_shape=jax.ShapeDtypeStruct((N_TOK, D_MODEL), F32),
        scratch_shapes=[
            pltpu.VMEM((D_MODEL, 2 * A_HALF), BF16),
            pltpu.VMEM((A_HALF, D_MODEL), BF16),
            pltpu.VMEM((tm, A_HALF), F32),
        ],
        compiler_params=_params(),
        name="gmlp",
    )(*x_args, mods, norm_g, w_in, v_gain, ws, bs, w_out)


FFN_TM = 1024
FFN_SPLIT_TM = 512
FFN_SUB = 512
FFN_TC = 1024


def _ffn_body(x, mod_ref, g_ref, w1_s, w2_s):
    mod = mod_ref[0]
    subs = [x[r:r + FFN_SUB] for r in range(0, x.shape[0], FFN_SUB)]
    hs = [(_rms(xs, g_ref[...]) * (1.0 + mod[4:5]) + mod[3:4]).astype(BF16) for xs in subs]
    outs = []
    for xs, h in zip(subs, hs):
        acc = jnp.zeros((FFN_SUB, D_MODEL), F32)
        for c in range(D_FF // FFN_TC):
            cols = slice(c * FFN_TC, (c + 1) * FFN_TC)
            hid = jnp.square(jnp.maximum(_dot(h, w1_s[:, cols]), 0.0)).astype(BF16)
            acc = acc + _dot(hid, w2_s[cols, :])
        outs.append(xs + mod[5:6] * acc)
    return jnp.concatenate(outs, axis=0)


def _ffn_kernel(x_ref, mod_ref, g_ref, w1_ref, w2_ref, o_ref, w1_s, w2_s):
    step = pl.program_id(0)

    @pl.when(step < LOAD_STEPS)
    def _():
        _load_chunk(step, w1_ref, w1_s)
        _load_chunk(step, w2_ref, w2_s)

    @pl.when(step >= LOAD_STEPS)
    def _():
        o_ref[...] = _ffn_body(x_ref[...], mod_ref, g_ref, w1_s, w2_s)


def _ffn_split_out_kernel(x_ref, mod_ref, g_ref, w1_ref, w2_ref, op_ref, os_ref, w1_s, w2_s):
    step = pl.program_id(0)

    @pl.when(step < LOAD_STEPS)
    def _():
        _load_chunk(step, w1_ref, w1_s)
        _load_chunk(step, w2_ref, w2_s)

    @pl.when(step >= LOAD_STEPS)
    def _():
        y = _ffn_body(x_ref[...], mod_ref, g_ref, w1_s, w2_s)
        is_prompt = _is_prompt_tile(step, FFN_SPLIT_TM)

        @pl.when(is_prompt)
        def _():
            op_ref[...] = y

        @pl.when(jnp.logical_not(is_prompt))
        def _():
            os_ref[...] = y


def _ffn(x, mods, layer, norm_g, w1, w2, split_out=False):
    tm = FFN_SPLIT_TM if split_out else FFN_TM
    if split_out:
        out_specs = [_prompt_tok_spec(tm), _sample_tok_spec(tm)]
        out_shape = [jax.ShapeDtypeStruct((P_TOK, D_MODEL), F32), jax.ShapeDtypeStruct((S_TOK, D_MODEL), F32)]
    else:
        out_specs = _tok_spec(tm)
        out_shape = jax.ShapeDtypeStruct((N_TOK, D_MODEL), F32)
    return pl.pallas_call(
        _ffn_split_out_kernel if split_out else _ffn_kernel,
        grid=(LOAD_STEPS + N_TOK // tm,),
        in_specs=[
            _tok_spec(tm),
            _mod_spec(layer, tm),
            _layer_spec(2 * layer + 1, (1, D_MODEL)),
            _chunk_spec(layer, D_MODEL, D_FF),
            _chunk_spec(layer, D_FF, D_MODEL),
        ],
        out_specs=out_specs,
        out_shape=out_shape,
        scratch_shapes=[
            pltpu.VMEM((D_MODEL, D_FF), BF16),
            pltpu.VMEM((D_FF, D_MODEL), BF16),
        ],
        compiler_params=_params(),
        name="ffn",
    )(x, mods, norm_g, w1, w2)


CONV_TM = 1024
CONV_TC = 512


def _conv_kernel(x_ref, mod_ref, g_ref, win_ref, cw_ref, cb_ref, wout_ref, o_ref, win_s, wout_s):
    step = pl.program_id(0)

    @pl.when(step < LOAD_STEPS)
    def _():
        _load_chunk(step, win_ref, win_s)
        _load_chunk(step, wout_ref, wout_s)

    @pl.when(step >= LOAD_STEPS)
    def _():
        seq_len = jnp.where(_is_prompt_tile(step, CONV_TM), SEQ, DEC_SEQ)
        pos = lax.broadcasted_iota(jnp.int32, (CONV_TM, 1), 0) & (seq_len - 1)
        has_prev = pos != 0
        has_next = pos != seq_len - 1
        x = x_ref[...]
        mod = mod_ref[0]
        h = (_rms(x, g_ref[...]) * (1.0 + mod[1:2]) + mod[0:1]).astype(BF16)
        cw = cw_ref[...]
        cb = cb_ref[...]
        acc = jnp.zeros((CONV_TM, D_MODEL), F32)
        for c in range(D_MODEL // CONV_TC):
            lo = c * CONV_TC
            cols = slice(lo, lo + CONV_TC)
            bg = _dot(h, win_s[:, lo:lo + CONV_TC])
            cg = _dot(h, win_s[:, D_MODEL + lo:D_MODEL + lo + CONV_TC])
            xt = _dot(h, win_s[:, 2 * D_MODEL + lo:2 * D_MODEL + lo + CONV_TC])
            z = cg * xt
            z_prev = jnp.where(has_prev, pltpu.roll(z, 1, axis=0), 0.0)
            z_next = jnp.where(has_next, pltpu.roll(z, CONV_TM - 1, axis=0), 0.0)
            zc = cw[0:1, cols] * z_prev + cw[1:2, cols] * z + cw[2:3, cols] * z_next + cb[:, cols]
            acc = acc + _dot((bg * zc).astype(BF16), wout_s[cols, :])
        o_ref[...] = x + mod[2:3] * acc


def _conv(x, mods, layer, norm_g, j, w_in, conv_w, conv_b, w_out):
    tm = CONV_TM
    return pl.pallas_call(
        _conv_kernel,
        grid=(LOAD_STEPS + N_TOK // tm,),
        in_specs=[
            _tok_spec(tm),
            _mod_spec(layer, tm),
            _layer_spec(2 * layer, (1, D_MODEL)),
            _chunk_spec(j, D_MODEL, 3 * D_MODEL),
            _layer_spec(j, (3, D_MODEL)),
            _layer_spec(j, (1, D_MODEL)),
            _chunk_spec(j, D_MODEL, D_MODEL),
        ],
        out_specs=_tok_spec(tm),
        out_shape=jax.ShapeDtypeStruct((N_TOK, D_MODEL), F32),
        scratch_shapes=[
            pltpu.VMEM((D_MODEL, 3 * D_MODEL), BF16),
            pltpu.VMEM((D_MODEL, D_MODEL), BF16),
        ],
        compiler_params=_params(),
        name="sconv",
    )(x, mods, norm_g, w_in, conv_w, conv_b, w_out)


QKV_TM = 512


def _qkv_kernel(x_ref, mod_ref, g_ref, w_ref, gq_ref, gk_ref, hm_ref,
                q_ref, k_ref, v_ref, kc_ref, vc_ref, w_s):
    step = pl.program_id(0)

    @pl.when(step < LOAD_STEPS)
    def _():
        _load_chunk(step, w_ref, w_s)

    @pl.when(step >= LOAD_STEPS)
    def _():
        x = x_ref[...]
        mod = mod_ref[0]
        h = (_rms(x, g_ref[...]) * (1.0 + mod[1:2]) + mod[0:1]).astype(BF16)
        head_mean = hm_ref[...]

        def head_norm(y, gain):
            sq = (y * y).astype(BF16)
            ms = jnp.concatenate(
                [_dot(sq[:, c:c + MXU_TILE], head_mean) for c in range(0, D_MODEL, MXU_TILE)], axis=1)
            return y * lax.rsqrt(ms + EPS) * gain

        q = head_norm(_dot(h, w_s[:, :D_MODEL]), gq_ref[...])
        k = head_norm(_dot(h, w_s[:, D_MODEL:2 * D_MODEL]), gk_ref[...])
        v = _dot(h, w_s[:, 2 * D_MODEL:])
        q_ref[...] = q.astype(BF16)
        k_ref[...] = k.astype(BF16)
        v_ref[...] = v.astype(BF16)

        @pl.when(_is_prompt_tile(step, QKV_TM))
        def _():
            kc_ref[...] = k
            vc_ref[...] = v


def _qkv(x, mods, layer, norm_g, j, w_qkv, q_gain, k_gain):
    tm = QKV_TM
    head_mean = jnp.asarray(
        np.kron(np.eye(MXU_TILE // HEAD_DIM), np.full((HEAD_DIM, HEAD_DIM), 1.0 / HEAD_DIM)), BF16)
    return pl.pallas_call(
        _qkv_kernel,
        grid=(LOAD_STEPS + N_TOK // tm,),
        in_specs=[
            _tok_spec(tm),
            _mod_spec(layer, tm),
            _layer_spec(2 * layer, (1, D_MODEL)),
            _chunk_spec(j, D_MODEL, 3 * D_MODEL),
            _const_spec((1, D_MODEL)),
            _const_spec((1, D_MODEL)),
            _const_spec((MXU_TILE, MXU_TILE)),
        ],
        out_specs=[_tok_spec(tm), _tok_spec(tm), _tok_spec(tm), _prompt_tok_spec(tm), _prompt_tok_spec(tm)],
        out_shape=[
            jax.ShapeDtypeStruct((N_TOK, D_MODEL), BF16),
            jax.ShapeDtypeStruct((N_TOK, D_MODEL), BF16),
            jax.ShapeDtypeStruct((N_TOK, D_MODEL), BF16),
            jax.ShapeDtypeStruct((P_TOK, D_MODEL), F32),
            jax.ShapeDtypeStruct((P_TOK, D_MODEL), F32),
        ],
        scratch_shapes=[pltpu.VMEM((D_MODEL, 3 * D_MODEL), BF16)],
        compiler_params=_params(),
        name="qkv",
    )(x, mods, norm_g, w_qkv,
      jnp.tile(q_gain[j], N_HEADS).reshape(1, D_MODEL), jnp.tile(k_gain[j], N_HEADS).reshape(1, D_MODEL),
      head_mean)


NBR_GROUP_ROWS = GRID_ROWS // 2
NBR_KEY_ROWS = NBR_GROUP_ROWS + WIN_ROWS // 2
NBR_Q = NBR_GROUP_ROWS * GRID_W
NBR_KEYS = NBR_KEY_ROWS * GRID_W


def _band_start(r):
    return min(max(r - WIN_ROWS // 2, 0), GRID_ROWS - WIN_ROWS)


def _key_row_start(r):
    return 0 if r < NBR_GROUP_ROWS else GRID_ROWS - NBR_KEY_ROWS


def _bias_kernel(rpb_ref, o_ref):
    h = pl.program_id(0)
    qc = lax.broadcasted_iota(jnp.int32, (GRID_W, GRID_W), 0)
    kc = lax.broadcasted_iota(jnp.int32, (GRID_W, GRID_W), 1)
    rel_c = jnp.clip(kc - qc + (WIN_COLS - 1), 0, RPB_COLS - 1)
    col_start = jnp.clip(qc - WIN_COLS // 2, 0, GRID_W - WIN_COLS)
    col_ok = (kc >= col_start) & (kc < col_start + WIN_COLS)
    base = h * (RPB_ROWS * RPB_COLS)
    row_tiles = []
    for rel_r in range(RPB_ROWS):
        tile = jnp.zeros((GRID_W, GRID_W), F32)
        for c in range(RPB_COLS):
            tile = jnp.where(rel_c == c, rpb_ref[base + rel_r * RPB_COLS + c], tile)
        row_tiles.append(jnp.where(col_ok, tile, MASK_VALUE))
    masked = jnp.full((GRID_W, GRID_W), MASK_VALUE, F32)
    for r in range(GRID_ROWS):
        for i in range(NBR_KEY_ROWS):
            key_row = _key_row_start(r) + i
            in_window = _band_start(r) <= key_row < _band_start(r) + WIN_ROWS
            tile = row_tiles[key_row - r + WIN_ROWS - 1] if in_window else masked
            o_ref[0, r, :, i * GRID_W:(i + 1) * GRID_W] = tile


def _window_bias(rpb):
    return pl.pallas_call(
        _bias_kernel,
        grid=(N_HEADS,),
        in_specs=[pl.BlockSpec(memory_space=pltpu.SMEM)],
        out_specs=pl.BlockSpec((1, GRID_ROWS, GRID_W, NBR_KEYS), lambda h: (h, 0, 0, 0)),
        out_shape=jax.ShapeDtypeStruct((N_HEADS, GRID_ROWS, GRID_W, NBR_KEYS), F32),
        compiler_params=_params(),
        name="window_bias",
    )(rpb.reshape(N_HEADS * RPB_ROWS * RPB_COLS))


def _head_lane_mask(half):
    lane = lax.broadcasted_iota(jnp.int32, (1, HEAD_PAIR), 1)
    return (lane >= HEAD_DIM) if half else (lane < HEAD_DIM)


def _scaled_head_queries(q2, sel):
    return jnp.where(sel, q2 * ATT_SCALE, jnp.zeros_like(q2))


def _ctx_attn_kernel(q_ref, k_ref, v_ref, o_ref):
    for p in range(N_PAIRS):
        cols = slice(p * HEAD_PAIR, (p + 1) * HEAD_PAIR)
        q2 = q_ref[:, cols]
        k2 = k_ref[:, cols]
        v2 = v_ref[:, cols]
        out = None
        for half in range(2):
            sel = _head_lane_mask(half)
            s = _dot_nt(_scaled_head_queries(q2, sel), k2)
            e = jnp.exp(s - jnp.max(s, axis=-1, keepdims=True))
            den = jnp.sum(e, axis=-1, keepdims=True)
            o = _dot(e.astype(BF16), v2) / den
            out = o if out is None else jnp.where(sel, o, out)
        o_ref[:, cols] = out.astype(BF16)


def _ctx_attention(q, k, v):
    spec = pl.BlockSpec((SEQ, D_MODEL), lambda b: (b, 0))
    return pl.pallas_call(
        _ctx_attn_kernel,
        grid=(BATCH,),
        in_specs=[spec, spec, spec],
        out_specs=spec,
        out_shape=jax.ShapeDtypeStruct((P_TOK, D_MODEL), BF16),
        compiler_params=_params(),
        name="ctx_attention",
    )(q, k, v)


NBR_SOFTMAX_ROWS = 32


def _nbr_attn_kernel(q_ref, k_ref, v_ref, ck_ref, cv_ref, bias_ref, o_ref):
    ck2 = ck_ref[0].astype(BF16)
    cv2 = cv_ref[0].astype(BF16)
    blocks = [(g, half) for g in range(GRID_ROWS // NBR_GROUP_ROWS) for half in range(2)]

    def scores(g, half):
        r0 = g * NBR_GROUP_ROWS
        key_lo = _key_row_start(r0) * GRID_W
        qm = _scaled_head_queries(q_ref[r0 * GRID_W:r0 * GRID_W + NBR_Q, :], _head_lane_mask(half))
        return _dot_nt(qm, k_ref[key_lo:key_lo + NBR_KEYS, :]), _dot_nt(qm, ck2)

    def attend(g, half, s_loc, s_ctx):
        r0 = g * NBR_GROUP_ROWS
        key_lo = _key_row_start(r0) * GRID_W
        e_loc, e_ctx, inv_den = [], [], []
        for lo in range(0, NBR_Q, NBR_SOFTMAX_ROWS):
            hi = lo + NBR_SOFTMAX_ROWS
            r, q_lo = r0 + lo // GRID_W, lo % GRID_W
            sl = s_loc[lo:hi] + bias_ref[half, r, q_lo:q_lo + NBR_SOFTMAX_ROWS, :]
            sc = s_ctx[lo:hi]
            m = jnp.maximum(jnp.max(sl, axis=-1, keepdims=True), jnp.max(sc, axis=-1, keepdims=True))
            el = jnp.exp(sl - m)
            ec = jnp.exp(sc - m)
            inv_den.append(1.0 / (jnp.sum(el, axis=-1, keepdims=True) + jnp.sum(ec, axis=-1, keepdims=True)))
            e_loc.append(el.astype(BF16))
            e_ctx.append(ec.astype(BF16))
        o = _dot(jnp.concatenate(e_loc, axis=0), v_ref[key_lo:key_lo + NBR_KEYS, :])
        o = o + _dot(jnp.concatenate(e_ctx, axis=0), cv2)
        return o * jnp.concatenate(inv_den, axis=0)

    pending = scores(*blocks[0])
    outs = {}
    for i, (g, half) in enumerate(blocks):
        s_loc, s_ctx = pending
        if i + 1 < len(blocks):
            pending = scores(*blocks[i + 1])
        outs[(g, half)] = attend(g, half, s_loc, s_ctx)
    for g in range(GRID_ROWS // NBR_GROUP_ROWS):
        rows = slice(g * NBR_Q, (g + 1) * NBR_Q)
        o_ref[rows, :] = jnp.where(_head_lane_mask(0), outs[(g, 0)], outs[(g, 1)]).astype(BF16)


def _nbr_attention(q, k, v, ck, cv, bias):
    first = P_TOK // DEC_SEQ
    tok = pl.BlockSpec((DEC_SEQ, HEAD_PAIR), lambda p, b: (first + b, p))
    ctx = pl.BlockSpec((1, PAST_LEN, HEAD_PAIR), lambda p, b: (b, 0, p))
    return pl.pallas_call(
        _nbr_attn_kernel,
        grid=(N_PAIRS, DEC_BATCH),
        in_specs=[tok, tok, tok, ctx, ctx,
                  pl.BlockSpec((2, GRID_ROWS, GRID_W, NBR_KEYS), lambda p, b: (p, 0, 0, 0))],
        out_specs=pl.BlockSpec((DEC_SEQ, HEAD_PAIR), lambda p, b: (b, p)),
        out_shape=jax.ShapeDtypeStruct((S_TOK, D_MODEL), BF16),
        compiler_params=_params(2),
        name="nbr_attention",
    )(q, k, v, ck, cv, bias)


PROJ_FFN_TM = 512


def _proj_ffn_kernel(x_ref, mod_ref, ap_ref, as_ref, wo_ref, g_ref, w1_ref, w2_ref, o_ref,
                     wo_s, w1_s, w2_s):
    step = pl.program_id(0)

    @pl.when(step < LOAD_STEPS)
    def _():
        _load_chunk(step, wo_ref, wo_s)
        _load_chunk(step, w1_ref, w1_s)
        _load_chunk(step, w2_ref, w2_s)

    @pl.when(step >= LOAD_STEPS)
    def _():
        a = jnp.where(_is_prompt_tile(step, PROJ_FFN_TM), ap_ref[...], as_ref[...])
        x = x_ref[...] + mod_ref[0][2:3] * _dot(a, wo_s[...])
        o_ref[...] = _ffn_body(x, mod_ref, g_ref, w1_s, w2_s)


def _attn_proj_ffn(x, mods, layer, norm_g, attn_prompt, attn_sample, j, w_o, w1, w2):
    tm = PROJ_FFN_TM
    return pl.pallas_call(
        _proj_ffn_kernel,
        grid=(LOAD_STEPS + N_TOK // tm,),
        in_specs=[
            _tok_spec(tm),
            _mod_spec(layer, tm),
            _prompt_tok_spec(tm),
            _sample_tok_spec(tm),
            _chunk_spec(j, D_MODEL, D_MODEL),
            _layer_spec(2 * layer + 1, (1, D_MODEL)),
            _chunk_spec(layer, D_MODEL, D_FF),
            _chunk_spec(layer, D_FF, D_MODEL),
        ],
        out_specs=_tok_spec(tm),
        out_shape=jax.ShapeDtypeStruct((N_TOK, D_MODEL), F32),
        scratch_shapes=[
            pltpu.VMEM((D_MODEL, D_MODEL), BF16),
            pltpu.VMEM((D_MODEL, D_FF), BF16),
            pltpu.VMEM((D_FF, D_MODEL), BF16),
        ],
        compiler_params=_params(),
        name="attn_proj_ffn",
    )(x, mods, attn_prompt, attn_sample, w_o, norm_g, w1, w2)


def kernel(x_prompt, x_sample, cache_k, cache_v, c, c_ctx, norm_g, ada_w, ada_b, a_w_in, a_v_gain, a_ws, a_bs, a_w_out, b_w_qkv, b_q_gain, b_k_gain, b_rpb, b_w_o, c_w_in, c_conv_w, c_conv_b, c_w_out, ff_w1, ff_w2):
    n_a = a_w_in.shape[0]
    cond = jnp.concatenate(
        [c, c_ctx[None, :], jnp.zeros((COND_ROWS - DEC_BATCH - 1, D_MODEL), F32)], axis=0)
    mods = _adaln(cond, ada_w, ada_b).reshape(DEPTH, COND_ROWS, 6, D_MODEL)
    norm_g = norm_g.reshape(2 * DEPTH, 1, D_MODEL)
    a_v_gain = a_v_gain.reshape(n_a, 1, A_HALF)
    a_bs = a_bs.reshape(n_a, A_GROUPS, CHUNK, 1)
    c_conv_b = c_conv_b.reshape(-1, 1, D_MODEL)
    x = (x_prompt.reshape(P_TOK, D_MODEL), x_sample.reshape(S_TOK, D_MODEL))
    new_k, new_v = [], []
    for i in range(DEPTH):
        kind, j = i % 3, i // 3
        if kind == 0:
            x = _gmlp(x, mods, i, norm_g, j, a_w_in, a_v_gain, a_ws, a_bs, a_w_out)
        elif kind == 1:
            q, k, v, k_new, v_new = _qkv(x, mods, i, norm_g, j, b_w_qkv, b_q_gain, b_k_gain)
            new_k.append(k_new.reshape(BATCH, SEQ, N_HEADS, HEAD_DIM))
            new_v.append(v_new.reshape(BATCH, SEQ, N_HEADS, HEAD_DIM))
            o_prompt = _ctx_attention(q, k, v)
            bias = _window_bias(b_rpb[j])
            o_sample = _nbr_attention(
                q, k, v,
                cache_k[:, j].reshape(DEC_BATCH, PAST_LEN, D_MODEL),
                cache_v[:, j].reshape(DEC_BATCH, PAST_LEN, D_MODEL), bias)
            x = _attn_proj_ffn(x, mods, i, norm_g, o_prompt, o_sample, j, b_w_o, ff_w1, ff_w2)
            continue
        else:
            x = _conv(x, mods, i, norm_g, j, c_w_in, c_conv_w, c_conv_b, c_w_out)
        x = _ffn(x, mods, i, norm_g, ff_w1, ff_w2, split_out=(i == DEPTH - 1))
    y_prompt = x[0].reshape(BATCH, SEQ, D_MODEL)
    y_sample = x[1].reshape(DEC_BATCH, DEC_SEQ, D_MODEL)
    return (y_prompt, y_sample, jnp.stack(new_k, axis=1), jnp.stack(new_v, axis=1))
```

```python
import jax
import jax.numpy as jnp
import numpy as np
from jax import lax
from jax.experimental import pallas as pl
from jax.experimental.pallas import tpu as pltpu

D_MODEL = 1024
BATCH = 16
SEQ = 256
DEPTH = 4
DEC_BATCH = 8
DEC_SEQ = 1024
PAST_LEN = 512
GRID_W = 64
GRID_ROWS = DEC_SEQ // GRID_W
CHUNK = 128
A_HALF = 2 * D_MODEL
A_GROUPS = 8
A_GROUP_W = A_HALF // A_GROUPS
N_HEADS = 16
HEAD_DIM = D_MODEL // N_HEADS
WIN_ROWS = 8
WIN_COLS = 16
RPB_ROWS = 2 * WIN_ROWS - 1
RPB_COLS = 2 * WIN_COLS - 1
D_FF = 4 * D_MODEL
EPS = 1e-6
ATT_SCALE = HEAD_DIM ** -0.5
MASK_VALUE = -1e30

P_TOK = BATCH * SEQ
S_TOK = DEC_BATCH * DEC_SEQ
N_TOK = P_TOK + S_TOK
COND_ROWS = 16
CTX_ROW = DEC_BATCH
LOAD_STEPS = 8
HEAD_PAIR = 2 * HEAD_DIM
N_PAIRS = N_HEADS // 2
MXU_TILE = 256

F32 = jnp.float32
BF16 = jnp.bfloat16
VMEM_LIMIT = 56 * 1024 * 1024


def _params(n_axes=1, vmem=VMEM_LIMIT):
    return pltpu.CompilerParams(dimension_semantics=("arbitrary",) * n_axes, vmem_limit_bytes=vmem)


def _dot(a, b):
    return jnp.dot(a, b, preferred_element_type=F32)


def _dot_nt(a, b):
    return lax.dot_general(a, b, (((1,), (1,)), ((), ())), preferred_element_type=F32)


def _rms(x, g):
    return x * lax.rsqrt(jnp.mean(x * x, axis=-1, keepdims=True) + EPS) * g


def _gelu(x):
    return 0.5 * x * (1.0 + jnp.tanh(0.7978845608028654 * (x + 0.044715 * (x * x * x))))


def _tok_spec(tm, width=D_MODEL):
    return pl.BlockSpec((tm, width), lambda s: (jnp.maximum(s - LOAD_STEPS, 0), 0))


def _prompt_tok_spec(tm):
    last = P_TOK // tm - 1
    return pl.BlockSpec((tm, D_MODEL), lambda s: (jnp.clip(s - LOAD_STEPS, 0, last), 0))


def _sample_tok_spec(tm):
    first, last = P_TOK // tm, S_TOK // tm - 1
    return pl.BlockSpec((tm, D_MODEL), lambda s: (jnp.clip(s - LOAD_STEPS - first, 0, last), 0))


def _is_prompt_tile(step, tm):
    return step - LOAD_STEPS < P_TOK // tm


def _mod_spec(layer, tm):
    n_prompt_tiles = P_TOK // tm
    tiles_per_seq = DEC_SEQ // tm

    def index(s):
        t = jnp.maximum(s - LOAD_STEPS, 0)
        row = jnp.where(t < n_prompt_tiles, CTX_ROW, jnp.maximum(t - n_prompt_tiles, 0) // tiles_per_seq)
        return (layer, row, 0, 0)

    return pl.BlockSpec((None, 1, 6, D_MODEL), index)


def _chunk_spec(layer, rows, cols):
    return pl.BlockSpec((None, rows // LOAD_STEPS, cols),
                        lambda s: (layer, jnp.minimum(s, LOAD_STEPS - 1), 0))


def _layer_spec(layer, shape):
    zeros = (0,) * len(shape)
    return pl.BlockSpec((None,) + tuple(shape), lambda s: (layer,) + zeros)


def _const_spec(shape):
    zeros = (0,) * len(shape)
    return pl.BlockSpec(shape, lambda s: zeros)


def _load_chunk(step, w_ref, w_scr):
    rows = w_ref.shape[0]
    off = pl.multiple_of(step * rows, rows)
    w_scr[pl.ds(off, rows), :] = w_ref[...].astype(BF16)


ADA_TN = 2048


def _ada_kernel(c_ref, w_ref, b_ref, o_ref):
    c = c_ref[...]
    a = c * (1.0 / (1.0 + jnp.exp(-c)))
    a_hi = a.astype(BF16)
    a_lo = (a - a_hi.astype(F32)).astype(BF16)
    w = w_ref[0]
    w_hi = w.astype(BF16)
    w_lo = (w - w_hi.astype(F32)).astype(BF16)
    o_ref[0] = _dot(a_hi, w_hi) + _dot(a_hi, w_lo) + _dot(a_lo, w_hi) + b_ref[0]


def _adaln(cond, ada_w, ada_b):
    n_out = 6 * D_MODEL
    return pl.pallas_call(
        _ada_kernel,
        grid=(DEPTH, n_out // ADA_TN),
        in_specs=[
            pl.BlockSpec((COND_ROWS, D_MODEL), lambda i, j: (0, 0)),
            pl.BlockSpec((1, D_MODEL, ADA_TN), lambda i, j: (i, 0, j)),
            pl.BlockSpec((1, 1, ADA_TN), lambda i, j: (i, 0, j)),
        ],
        out_specs=pl.BlockSpec((1, COND_ROWS, ADA_TN), lambda i, j: (i, 0, j)),
        out_shape=jax.ShapeDtypeStruct((DEPTH, COND_ROWS, n_out), F32),
        compiler_params=_params(2),
        name="adaln",
    )(cond, ada_w, ada_b.reshape(DEPTH, 1, n_out))


GMLP_TM = 512
GMLP_SUB = 512
GMLP_TC = 1024


def _gmlp_gate_half(h, win_s, v_s):
    ssq = jnp.zeros((GMLP_SUB, 1), F32)
    for c in range(0, A_HALF, GMLP_TC):
        cols = slice(c, c + GMLP_TC)
        v_c = _gelu(_dot(h, win_s[:, A_HALF + c:A_HALF + c + GMLP_TC]))
        ssq = ssq + jnp.sum(v_c * v_c, axis=-1, keepdims=True)
        v_s[:, cols] = v_c
    return lax.rsqrt(ssq * (1.0 / A_HALF) + EPS)


def _gmlp_mix(x, h, inv_rms, mod, vg_ref, ws_ref, bs_ref, win_s, wout_s, v_s):
    acc = jnp.zeros((GMLP_SUB, D_MODEL), F32)
    for c in range(0, A_HALF, GMLP_TC):
        cols = slice(c, c + GMLP_TC)
        v_n = (v_s[:, cols] * inv_rms * vg_ref[:, cols]).astype(BF16)
        u_c = _gelu(_dot(h, win_s[:, cols]))
        gated_cols = []
        for g in range(GMLP_TC // A_GROUP_W):
            gcols = slice(g * A_GROUP_W, (g + 1) * A_GROUP_W)
            w_g = ws_ref[c // A_GROUP_W + g].astype(BF16)
            b_g = bs_ref[c // A_GROUP_W + g]
            gated = []
            for n in range(GMLP_SUB // CHUNK):
                rows = slice(n * CHUNK, (n + 1) * CHUNK)
                s = _dot(w_g, v_n[rows, gcols]) + b_g
                gated.append((u_c[rows, gcols] * s).astype(BF16))
            gated_cols.append(jnp.concatenate(gated, axis=0))
        acc = acc + _dot(jnp.concatenate(gated_cols, axis=1), wout_s[cols, :])
    return x + mod[2:3] * acc


def _gmlp_body(x, mod_ref, g_ref, vg_ref, ws_ref, bs_ref, o_ref, win_s, wout_s, v_s):
    mod = mod_ref[0]
    row_sets = [slice(r, r + GMLP_SUB) for r in range(0, GMLP_TM, GMLP_SUB)]
    hs = [(_rms(x[rows], g_ref[...]) * (1.0 + mod[1:2]) + mod[0:1]).astype(BF16) for rows in row_sets]
    inv = [_gmlp_gate_half(h, win_s, v_s.at[rows]) for rows, h in zip(row_sets, hs)]
    for rows, h, inv_rms in zip(row_sets, hs, inv):
        o_ref[rows, :] = _gmlp_mix(x[rows], h, inv_rms, mod, vg_ref, ws_ref, bs_ref, win_s, wout_s,
                                   v_s.at[rows])


def _gmlp_kernel(x_ref, mod_ref, g_ref, win_ref, vg_ref, ws_ref, bs_ref, wout_ref, o_ref,
                 win_s, wout_s, v_s):
    step = pl.program_id(0)

    @pl.when(step < LOAD_STEPS)
    def _():
        _load_chunk(step, win_ref, win_s)
        _load_chunk(step, wout_ref, wout_s)

    @pl.when(step >= LOAD_STEPS)
    def _():
        _gmlp_body(x_ref[...], mod_ref, g_ref, vg_ref, ws_ref, bs_ref, o_ref, win_s, wout_s, v_s)


def _gmlp_split_in_kernel(xp_ref, xs_ref, mod_ref, g_ref, win_ref, vg_ref, ws_ref, bs_ref, wout_ref,
                          o_ref, win_s, wout_s, v_s):
    step = pl.program_id(0)

    @pl.when(step < LOAD_STEPS)
    def _():
        _load_chunk(step, win_ref, win_s)
        _load_chunk(step, wout_ref, wout_s)

    @pl.when(step >= LOAD_STEPS)
    def _():
        x = jnp.where(_is_prompt_tile(step, GMLP_TM), xp_ref[...], xs_ref[...])
        _gmlp_body(x, mod_ref, g_ref, vg_ref, ws_ref, bs_ref, o_ref, win_s, wout_s, v_s)


def _gmlp(xs, mods, layer, norm_g, j, w_in, v_gain, ws, bs, w_out):
    tm = GMLP_TM
    split_in = isinstance(xs, tuple)
    x_specs = [_prompt_tok_spec(tm), _sample_tok_spec(tm)] if split_in else [_tok_spec(tm)]
    x_args = list(xs) if split_in else [xs]
    return pl.pallas_call(
        _gmlp_split_in_kernel if split_in else _gmlp_kernel,
        grid=(LOAD_STEPS + N_TOK // tm,),
        in_specs=x_specs + [
            _mod_spec(layer, tm),
            _layer_spec(2 * layer, (1, D_MODEL)),
            _chunk_spec(j, D_MODEL, 2 * A_HALF),
            _layer_spec(j, (1, A_HALF)),
            _layer_spec(j, (A_GROUPS, CHUNK, CHUNK)),
            _layer_spec(j, (A_GROUPS, CHUNK, 1)),
            _chunk_spec(j, A_HALF, D_MODEL),
        ],
        out_specs=_tok_spec(tm),
        out_shape=jax.ShapeDtypeStruct((N_TOK, D_MODEL), F32),
        scratch_shapes=[
            pltpu.VMEM((D_MODEL, 2 * A_HALF), BF16),
            pltpu.VMEM((A_HALF, D_MODEL), BF16),
            pltpu.VMEM((tm, A_HALF), F32),
        ],
        compiler_params=_params(),
        name="gmlp",
    )(*x_args, mods, norm_g, w_in, v_gain, ws, bs, w_out)


FFN_TM = 512
FFN_SPLIT_TM = 512
FFN_SUB = 512
FFN_TC = 1024


def _ffn_body(x, mod_ref, g_ref, w1_s, w2_s):
    mod = mod_ref[0]
    subs = [x[r:r + FFN_SUB] for r in range(0, x.shape[0], FFN_SUB)]
    hs = [(_rms(xs, g_ref[...]) * (1.0 + mod[4:5]) + mod[3:4]).astype(BF16) for xs in subs]
    outs = []
    for xs, h in zip(subs, hs):
        acc = jnp.zeros((FFN_SUB, D_MODEL), F32)
        for c in range(D_FF // FFN_TC):
            cols = slice(c * FFN_TC, (c + 1) * FFN_TC)
            hid = jnp.square(jnp.maximum(_dot(h, w1_s[:, cols]), 0.0)).astype(BF16)
            acc = acc + _dot(hid, w2_s[cols, :])
        outs.append(xs + mod[5:6] * acc)
    return jnp.concatenate(outs, axis=0)


def _ffn_kernel(x_ref, mod_ref, g_ref, w1_ref, w2_ref, o_ref, w1_s, w2_s):
    step = pl.program_id(0)

    @pl.when(step < LOAD_STEPS)
    def _():
        _load_chunk(step, w1_ref, w1_s)
        _load_chunk(step, w2_ref, w2_s)

    @pl.when(step >= LOAD_STEPS)
    def _():
        o_ref[...] = _ffn_body(x_ref[...], mod_ref, g_ref, w1_s, w2_s)


def _ffn_split_out_kernel(x_ref, mod_ref, g_ref, w1_ref, w2_ref, op_ref, os_ref, w1_s, w2_s):
    step = pl.program_id(0)

    @pl.when(step < LOAD_STEPS)
    def _():
        _load_chunk(step, w1_ref, w1_s)
        _load_chunk(step, w2_ref, w2_s)

    @pl.when(step >= LOAD_STEPS)
    def _():
        y = _ffn_body(x_ref[...], mod_ref, g_ref, w1_s, w2_s)
        is_prompt = _is_prompt_tile(step, FFN_SPLIT_TM)

        @pl.when(is_prompt)
        def _():
            op_ref[...] = y

        @pl.when(jnp.logical_not(is_prompt))
        def _():
            os_ref[...] = y


def _ffn(x, mods, layer, norm_g, w1, w2, split_out=False):
    tm = FFN_SPLIT_TM if split_out else FFN_TM
    if split_out:
        out_specs = [_prompt_tok_spec(tm), _sample_tok_spec(tm)]
        out_shape = [jax.ShapeDtypeStruct((P_TOK, D_MODEL), F32), jax.ShapeDtypeStruct((S_TOK, D_MODEL), F32)]
    else:
        out_specs = _tok_spec(tm)
        out_shape = jax.ShapeDtypeStruct((N_TOK, D_MODEL), F32)
    return pl.pallas_call(
        _ffn_split_out_kernel if split_out else _ffn_kernel,
        grid=(LOAD_STEPS + N_TOK // tm,),
        in_specs=[
            _tok_spec(tm),
            _mod_spec(layer, tm),
            _layer_spec(2 * layer + 1, (1, D_MODEL)),
            _chunk_spec(layer, D_MODEL, D_FF),
            _chunk_spec(layer, D_FF, D_MODEL),
        ],
        out_specs=out_specs,
        out_shape=out_shape,
        scratch_shapes=[
            pltpu.VMEM((D_MODEL, D_FF), BF16),
            pltpu.VMEM((D_FF, D_MODEL), BF16),
        ],
        compiler_params=_params(),
        name="ffn",
    )(x, mods, norm_g, w1, w2)


CONV_TM = 1024
CONV_TC = 512


def _conv_kernel(x_ref, mod_ref, g_ref, win_ref, cw_ref, cb_ref, wout_ref, o_ref, win_s, wout_s):
    step = pl.program_id(0)

    @pl.when(step < LOAD_STEPS)
    def _():
        _load_chunk(step, win_ref, win_s)
        _load_chunk(step, wout_ref, wout_s)

    @pl.when(step >= LOAD_STEPS)
    def _():
        seq_len = jnp.where(_is_prompt_tile(step, CONV_TM), SEQ, DEC_SEQ)
        pos = lax.broadcasted_iota(jnp.int32, (CONV_TM, 1), 0) & (seq_len - 1)
        has_prev = pos != 0
        has_next = pos != seq_len - 1
        x = x_ref[...]
        mod = mod_ref[0]
        h = (_rms(x, g_ref[...]) * (1.0 + mod[1:2]) + mod[0:1]).astype(BF16)
        cw = cw_ref[...]
        cb = cb_ref[...]
        acc = jnp.zeros((CONV_TM, D_MODEL), F32)
        for c in range(D_MODEL // CONV_TC):
            lo = c * CONV_TC
            cols = slice(lo, lo + CONV_TC)
            bg = _dot(h, win_s[:, lo:lo + CONV_TC])
            cg = _dot(h, win_s[:, D_MODEL + lo:D_MODEL + lo + CONV_TC])
            xt = _dot(h, win_s[:, 2 * D_MODEL + lo:2 * D_MODEL + lo + CONV_TC])
            z = cg * xt
            z_prev = jnp.where(has_prev, pltpu.roll(z, 1, axis=0), 0.0)
            z_next = jnp.where(has_next, pltpu.roll(z, CONV_TM - 1, axis=0), 0.0)
            zc = cw[0:1, cols] * z_prev + cw[1:2, cols] * z + cw[2:3, cols] * z_next + cb[:, cols]
            acc = acc + _dot((bg * zc).astype(BF16), wout_s[cols, :])
        o_ref[...] = x + mod[2:3] * acc


def _conv(x, mods, layer, norm_g, j, w_in, conv_w, conv_b, w_out):
    tm = CONV_TM
    return pl.pallas_call(
        _conv_kernel,
        grid=(LOAD_STEPS + N_TOK // tm,),
        in_specs=[
            _tok_spec(tm),
            _mod_spec(layer, tm),
            _layer_spec(2 * layer, (1, D_MODEL)),
            _chunk_spec(j, D_MODEL, 3 * D_MODEL),
            _layer_spec(j, (3, D_MODEL)),
            _layer_spec(j, (1, D_MODEL)),
            _chunk_spec(j, D_MODEL, D_MODEL),
        ],
        out_specs=_tok_spec(tm),
        out_shape=jax.ShapeDtypeStruct((N_TOK, D_MODEL), F32),
        scratch_shapes=[
            pltpu.VMEM((D_MODEL, 3 * D_MODEL), BF16),
            pltpu.VMEM((D_MODEL, D_MODEL), BF16),
        ],
        compiler_params=_params(),
        name="sconv",
    )(x, mods, norm_g, w_in, conv_w, conv_b, w_out)


QKV_TM = 512


def _qkv_kernel(x_ref, mod_ref, g_ref, w_ref, gq_ref, gk_ref, hm_ref,
                q_ref, k_ref, v_ref, kc_ref, vc_ref, w_s):
    step = pl.program_id(0)

    @pl.when(step < LOAD_STEPS)
    def _():
        _load_chunk(step, w_ref, w_s)

    @pl.when(step >= LOAD_STEPS)
    def _():
        x = x_ref[...]
        mod = mod_ref[0]
        h = (_rms(x, g_ref[...]) * (1.0 + mod[1:2]) + mod[0:1]).astype(BF16)
        head_mean = hm_ref[...]

        def head_norm(y, gain):
            sq = (y * y).astype(BF16)
            ms = jnp.concatenate(
                [_dot(sq[:, c:c + MXU_TILE], head_mean) for c in range(0, D_MODEL, MXU_TILE)], axis=1)
            return y * lax.rsqrt(ms + EPS) * gain

        q = head_norm(_dot(h, w_s[:, :D_MODEL]), gq_ref[...])
        k = head_norm(_dot(h, w_s[:, D_MODEL:2 * D_MODEL]), gk_ref[...])
        v = _dot(h, w_s[:, 2 * D_MODEL:])
        q_ref[...] = q.astype(BF16)
        k_ref[...] = k.astype(BF16)
        v_ref[...] = v.astype(BF16)

        @pl.when(_is_prompt_tile(step, QKV_TM))
        def _():
            kc_ref[...] = k
            vc_ref[...] = v


def _qkv(x, mods, layer, norm_g, j, w_qkv, q_gain, k_gain):
    tm = QKV_TM
    head_mean = jnp.asarray(
        np.kron(np.eye(MXU_TILE // HEAD_DIM), np.full((HEAD_DIM, HEAD_DIM), 1.0 / HEAD_DIM)), BF16)
    return pl.pallas_call(
        _qkv_kernel,
        grid=(LOAD_STEPS + N_TOK // tm,),
        in_specs=[
            _tok_spec(tm),
            _mod_spec(layer, tm),
            _layer_spec(2 * layer, (1, D_MODEL)),
            _chunk_spec(j, D_MODEL, 3 * D_MODEL),
            _const_spec((1, D_MODEL)),
            _const_spec((1, D_MODEL)),
            _const_spec((MXU_TILE, MXU_TILE)),
        ],
        out_specs=[_tok_spec(tm), _tok_spec(tm), _tok_spec(tm), _prompt_tok_spec(tm), _prompt_tok_spec(tm)],
        out_shape=[
            jax.ShapeDtypeStruct((N_TOK, D_MODEL), BF16),
            jax.ShapeDtypeStruct((N_TOK, D_MODEL), BF16),
            jax.ShapeDtypeStruct((N_TOK, D_MODEL), BF16),
            jax.ShapeDtypeStruct((P_TOK, D_MODEL), F32),
            jax.ShapeDtypeStruct((P_TOK, D_MODEL), F32),
        ],
        scratch_shapes=[pltpu.VMEM((D_MODEL, 3 * D_MODEL), BF16)],
        compiler_params=_params(),
        name="qkv",
    )(x, mods, norm_g, w_qkv,
      jnp.tile(q_gain[j], N_HEADS).reshape(1, D_MODEL), jnp.tile(k_gain[j], N_HEADS).reshape(1, D_MODEL),
      head_mean)


NBR_GROUP_ROWS = GRID_ROWS // 2
NBR_KEY_ROWS = NBR_GROUP_ROWS + WIN_ROWS // 2
NBR_Q = NBR_GROUP_ROWS * GRID_W
NBR_KEYS = NBR_KEY_ROWS * GRID_W


def _band_start(r):
    return min(max(r - WIN_ROWS // 2, 0), GRID_ROWS - WIN_ROWS)


def _key_row_start(r):
    return 0 if r < NBR_GROUP_ROWS else GRID_ROWS - NBR_KEY_ROWS


def _bias_kernel(rpb_ref, o_ref):
    h = pl.program_id(0)
    qc = lax.broadcasted_iota(jnp.int32, (GRID_W, GRID_W), 0)
    kc = lax.broadcasted_iota(jnp.int32, (GRID_W, GRID_W), 1)
    rel_c = jnp.clip(kc - qc + (WIN_COLS - 1), 0, RPB_COLS - 1)
    col_start = jnp.clip(qc - WIN_COLS // 2, 0, GRID_W - WIN_COLS)
    col_ok = (kc >= col_start) & (kc < col_start + WIN_COLS)
    base = h * (RPB_ROWS * RPB_COLS)
    row_tiles = []
    for rel_r in range(RPB_ROWS):
        tile = jnp.zeros((GRID_W, GRID_W), F32)
        for c in range(RPB_COLS):
            tile = jnp.where(rel_c == c, rpb_ref[base + rel_r * RPB_COLS + c], tile)
        row_tiles.append(jnp.where(col_ok, tile, MASK_VALUE))
    masked = jnp.full((GRID_W, GRID_W), MASK_VALUE, F32)
    for r in range(GRID_ROWS):
        for i in range(NBR_KEY_ROWS):
            key_row = _key_row_start(r) + i
            in_window = _band_start(r) <= key_row < _band_start(r) + WIN_ROWS
            tile = row_tiles[key_row - r + WIN_ROWS - 1] if in_window else masked
            o_ref[0, r, :, i * GRID_W:(i + 1) * GRID_W] = tile


def _window_bias(rpb):
    return pl.pallas_call(
        _bias_kernel,
        grid=(N_HEADS,),
        in_specs=[pl.BlockSpec(memory_space=pltpu.SMEM)],
        out_specs=pl.BlockSpec((1, GRID_ROWS, GRID_W, NBR_KEYS), lambda h: (h, 0, 0, 0)),
        out_shape=jax.ShapeDtypeStruct((N_HEADS, GRID_ROWS, GRID_W, NBR_KEYS), F32),
        compiler_params=_params(),
        name="window_bias",
    )(rpb.reshape(N_HEADS * RPB_ROWS * RPB_COLS))


def _head_lane_mask(half):
    lane = lax.broadcasted_iota(jnp.int32, (1, HEAD_PAIR), 1)
    return (lane >= HEAD_DIM) if half else (lane < HEAD_DIM)


def _scaled_head_queries(q2, sel):
    return jnp.where(sel, q2 * ATT_SCALE, jnp.zeros_like(q2))


def _values_and_ones(v2, sel):
    return jnp.where(sel, v2, jnp.ones_like(v2))


def _normalise(o):
    return o * (1.0 / pltpu.roll(o, HEAD_DIM, axis=1))


def _ctx_attn_kernel(q_ref, k_ref, v_ref, o_ref):
    for p in range(N_PAIRS):
        cols = slice(p * HEAD_PAIR, (p + 1) * HEAD_PAIR)
        q2 = q_ref[:, cols]
        k2 = k_ref[:, cols]
        v2 = v_ref[:, cols]
        out = None
        for half in range(2):
            sel = _head_lane_mask(half)
            s = _dot_nt(_scaled_head_queries(q2, sel), k2)
            e = jnp.exp(s - jnp.max(s, axis=-1, keepdims=True))
            o = _dot(e.astype(BF16), v2) / jnp.sum(e, axis=-1, keepdims=True)
            out = o if out is None else jnp.where(sel, o, out)
        o_ref[:, cols] = out.astype(BF16)


def _ctx_attention(q, k, v):
    spec = pl.BlockSpec((SEQ, D_MODEL), lambda b: (b, 0))
    return pl.pallas_call(
        _ctx_attn_kernel,
        grid=(BATCH,),
        in_specs=[spec, spec, spec],
        out_specs=spec,
        out_shape=jax.ShapeDtypeStruct((P_TOK, D_MODEL), BF16),
        compiler_params=_params(),
        name="ctx_attention",
    )(q, k, v)


NBR_SOFTMAX_ROWS = 32


def _nbr_attn_kernel(q_ref, k_ref, v_ref, ck_ref, cv_ref, bias_ref, o_ref):
    ck2 = ck_ref[0].astype(BF16)
    cv2 = cv_ref[0].astype(BF16)
    blocks = [(g, half) for g in range(GRID_ROWS // NBR_GROUP_ROWS) for half in range(2)]

    def scores(g, half):
        r0 = g * NBR_GROUP_ROWS
        key_lo = _key_row_start(r0) * GRID_W
        qm = _scaled_head_queries(q_ref[r0 * GRID_W:r0 * GRID_W + NBR_Q, :], _head_lane_mask(half))
        return _dot_nt(qm, k_ref[key_lo:key_lo + NBR_KEYS, :]), _dot_nt(qm, ck2)

    def attend(g, half, s_loc, s_ctx):
        r0 = g * NBR_GROUP_ROWS
        key_lo = _key_row_start(r0) * GRID_W
        sel = _head_lane_mask(half)
        e_loc, e_ctx = [], []
        for lo in range(0, NBR_Q, NBR_SOFTMAX_ROWS):
            hi = lo + NBR_SOFTMAX_ROWS
            r, q_lo = r0 + lo // GRID_W, lo % GRID_W
            sl = s_loc[lo:hi] + bias_ref[half, r, q_lo:q_lo + NBR_SOFTMAX_ROWS, :]
            sc = s_ctx[lo:hi]
            m = jnp.maximum(jnp.max(sl, axis=-1, keepdims=True), jnp.max(sc, axis=-1, keepdims=True))
            e_loc.append(jnp.exp(sl - m).astype(BF16))
            e_ctx.append(jnp.exp(sc - m).astype(BF16))
        o = _dot(jnp.concatenate(e_loc, axis=0), _values_and_ones(v_ref[key_lo:key_lo + NBR_KEYS, :], sel))
        o = o + _dot(jnp.concatenate(e_ctx, axis=0), _values_and_ones(cv2, sel))
        return _normalise(o)

    pending = scores(*blocks[0])
    outs = {}
    for i, (g, half) in enumerate(blocks):
        s_loc, s_ctx = pending
        if i + 1 < len(blocks):
            pending = scores(*blocks[i + 1])
        outs[(g, half)] = attend(g, half, s_loc, s_ctx)
    for g in range(GRID_ROWS // NBR_GROUP_ROWS):
        rows = slice(g * NBR_Q, (g + 1) * NBR_Q)
        o_ref[rows, :] = jnp.where(_head_lane_mask(0), outs[(g, 0)], outs[(g, 1)]).astype(BF16)


def _nbr_attention(q, k, v, ck, cv, bias):
    first = P_TOK // DEC_SEQ
    tok = pl.BlockSpec((DEC_SEQ, HEAD_PAIR), lambda p, b: (first + b, p))
    ctx = pl.BlockSpec((1, PAST_LEN, HEAD_PAIR), lambda p, b: (b, 0, p))
    return pl.pallas_call(
        _nbr_attn_kernel,
        grid=(N_PAIRS, DEC_BATCH),
        in_specs=[tok, tok, tok, ctx, ctx,
                  pl.BlockSpec((2, GRID_ROWS, GRID_W, NBR_KEYS), lambda p, b: (p, 0, 0, 0))],
        out_specs=pl.BlockSpec((DEC_SEQ, HEAD_PAIR), lambda p, b: (b, p)),
        out_shape=jax.ShapeDtypeStruct((S_TOK, D_MODEL), BF16),
        compiler_params=_params(2),
        name="nbr_attention",
    )(q, k, v, ck, cv, bias)


PROJ_FFN_TM = 512


def _proj_ffn_kernel(x_ref, mod_ref, ap_ref, as_ref, wo_ref, g_ref, w1_ref, w2_ref, o_ref,
                     wo_s, w1_s, w2_s):
    step = pl.program_id(0)

    @pl.when(step < LOAD_STEPS)
    def _():
        _load_chunk(step, wo_ref, wo_s)
        _load_chunk(step, w1_ref, w1_s)
        _load_chunk(step, w2_ref, w2_s)

    @pl.when(step >= LOAD_STEPS)
    def _():
        a = jnp.where(_is_prompt_tile(step, PROJ_FFN_TM), ap_ref[...], as_ref[...])
        x = x_ref[...] + mod_ref[0][2:3] * _dot(a, wo_s[...])
        o_ref[...] = _ffn_body(x, mod_ref, g_ref, w1_s, w2_s)


def _attn_proj_ffn(x, mods, layer, norm_g, attn_prompt, attn_sample, j, w_o, w1, w2):
    tm = PROJ_FFN_TM
    return pl.pallas_call(
        _proj_ffn_kernel,
        grid=(LOAD_STEPS + N_TOK // tm,),
        in_specs=[
            _tok_spec(tm),
            _mod_spec(layer, tm),
            _prompt_tok_spec(tm),
            _sample_tok_spec(tm),
            _chunk_spec(j, D_MODEL, D_MODEL),
            _layer_spec(2 * layer + 1, (1, D_MODEL)),
            _chunk_spec(layer, D_MODEL, D_FF),
            _chunk_spec(layer, D_FF, D_MODEL),
        ],
        out_specs=_tok_spec(tm),
        out_shape=jax.ShapeDtypeStruct((N_TOK, D_MODEL), F32),
        scratch_shapes=[
            pltpu.VMEM((D_MODEL, D_MODEL), BF16),
            pltpu.VMEM((D_MODEL, D_FF), BF16),
            pltpu.VMEM((D_FF, D_MODEL), BF16),
        ],
        compiler_params=_params(),
        name="attn_proj_ffn",
    )(x, mods, attn_prompt, attn_sample, w_o, norm_g, w1, w2)


def kernel(x_prompt, x_sample, cache_k, cache_v, c, c_ctx, norm_g, ada_w, ada_b, a_w_in, a_v_gain, a_ws, a_bs, a_w_out, b_w_qkv, b_q_gain, b_k_gain, b_rpb, b_w_o, c_w_in, c_conv_w, c_conv_b, c_w_out, ff_w1, ff_w2):
    n_a = a_w_in.shape[0]
    cond = jnp.concatenate(
        [c, c_ctx[None, :], jnp.zeros((COND_ROWS - DEC_BATCH - 1, D_MODEL), F32)], axis=0)
    mods = _adaln(cond, ada_w, ada_b).reshape(DEPTH, COND_ROWS, 6, D_MODEL)
    norm_g = norm_g.reshape(2 * DEPTH, 1, D_MODEL)
    a_v_gain = a_v_gain.reshape(n_a, 1, A_HALF)
    a_bs = a_bs.reshape(n_a, A_GROUPS, CHUNK, 1)
    c_conv_b = c_conv_b.reshape(-1, 1, D_MODEL)
    x = (x_prompt.reshape(P_TOK, D_MODEL), x_sample.reshape(S_TOK, D_MODEL))
    new_k, new_v = [], []
    for i in range(DEPTH):
        kind, j = i % 3, i // 3
        if kind == 0:
            x = _gmlp(x, mods, i, norm_g, j, a_w_in, a_v_gain, a_ws, a_bs, a_w_out)
        elif kind == 1:
            q, k, v, k_new, v_new = _qkv(x, mods, i, norm_g, j, b_w_qkv, b_q_gain, b_k_gain)
            new_k.append(k_new.reshape(BATCH, SEQ, N_HEADS, HEAD_DIM))
            new_v.append(v_new.reshape(BATCH, SEQ, N_HEADS, HEAD_DIM))
            o_prompt = _ctx_attention(q, k, v)
            bias = _window_bias(b_rpb[j])
            o_sample = _nbr_attention(
                q, k, v,
                cache_k[:, j].reshape(DEC_BATCH, PAST_LEN, D_MODEL),
                cache_v[:, j].reshape(DEC_BATCH, PAST_LEN, D_MODEL), bias)
            x = _attn_proj_ffn(x, mods, i, norm_g, o_prompt, o_sample, j, b_w_o, ff_w1, ff_w2)
            continue
        else:
            x = _conv(x, mods, i, norm_g, j, c_w_in, c_conv_w, c_conv_b, c_w_out)
        x = _ffn(x, mods, i, norm_g, ff_w1, ff_w2, split_out=(i == DEPTH - 1))
    y_prompt = x[0].reshape(BATCH, SEQ, D_MODEL)
    y_sample = x[1].reshape(DEC_BATCH, DEC_SEQ, D_MODEL)
    return (y_prompt, y_sample, jnp.stack(new_k, axis=1), jnp.stack(new_v, axis=1))
```

```python
import functools

import jax
import jax.numpy as jnp
import numpy as np
from jax import lax
from jax.experimental import pallas as pl
from jax.experimental.pallas import tpu as pltpu

D_MODEL = 1024
BATCH = 16
SEQ = 256
DEPTH = 4
DEC_BATCH = 8
DEC_SEQ = 1024
PAST_LEN = 512
GRID_W = 64
GRID_ROWS = DEC_SEQ // GRID_W
CHUNK = 128
A_HALF = 2 * D_MODEL
A_GROUPS = 8
A_GROUP_W = A_HALF // A_GROUPS
N_HEADS = 16
HEAD_DIM = D_MODEL // N_HEADS
WIN_ROWS = 8
WIN_COLS = 16
RPB_ROWS = 2 * WIN_ROWS - 1
RPB_COLS = 2 * WIN_COLS - 1
D_FF = 4 * D_MODEL
EPS = 1e-6
ATT_SCALE = HEAD_DIM ** -0.5
MASK_VALUE = -1e30

P_TOK = BATCH * SEQ
S_TOK = DEC_BATCH * DEC_SEQ
N_TOK = P_TOK + S_TOK
COND_ROWS = 16
CTX_ROW = DEC_BATCH
LOAD_STEPS = 8
HEAD_PAIR = 2 * HEAD_DIM
N_PAIRS = N_HEADS // 2
MXU_TILE = 256

F32 = jnp.float32
BF16 = jnp.bfloat16
VMEM_LIMIT = 56 * 1024 * 1024


def _params(n_axes=1, vmem=VMEM_LIMIT):
    return pltpu.CompilerParams(dimension_semantics=("arbitrary",) * n_axes, vmem_limit_bytes=vmem)


def _dot(a, b):
    return jnp.dot(a, b, preferred_element_type=F32)


def _dot_nt(a, b):
    return lax.dot_general(a, b, (((1,), (1,)), ((), ())), preferred_element_type=F32)


def _rms(x, g):
    return x * lax.rsqrt(jnp.mean(x * x, axis=-1, keepdims=True) + EPS) * g


def _gelu(x):
    return 0.5 * x * (1.0 + jnp.tanh(0.7978845608028654 * (x + 0.044715 * (x * x * x))))


def _tok_spec(tm, tile0_step=LOAD_STEPS):
    return pl.BlockSpec((tm, D_MODEL), lambda s: (jnp.maximum(s - tile0_step, 0), 0))


def _prompt_tok_spec(tm, tile0_step=LOAD_STEPS):
    last = P_TOK // tm - 1
    return pl.BlockSpec((tm, D_MODEL), lambda s: (jnp.clip(s - tile0_step, 0, last), 0))


def _sample_tok_spec(tm, tile0_step=LOAD_STEPS):
    first, last = P_TOK // tm, S_TOK // tm - 1
    return pl.BlockSpec((tm, D_MODEL), lambda s: (jnp.clip(s - tile0_step - first, 0, last), 0))


def _is_prompt_tile(step, tm, tile0_step=LOAD_STEPS):
    return step - tile0_step < P_TOK // tm


def _mod_spec(layer, tm, tile0_step=LOAD_STEPS):
    n_prompt_tiles = P_TOK // tm
    tiles_per_seq = DEC_SEQ // tm

    def index(s):
        t = jnp.maximum(s - tile0_step, 0)
        row = jnp.where(t < n_prompt_tiles, CTX_ROW, jnp.maximum(t - n_prompt_tiles, 0) // tiles_per_seq)
        return (layer, row, 0, 0)

    return pl.BlockSpec((None, 1, 6, D_MODEL), index)


def _chunk_spec(layer, rows, cols):
    return pl.BlockSpec((None, rows // LOAD_STEPS, cols),
                        lambda s: (layer, jnp.minimum(s, LOAD_STEPS - 1), 0))


def _layer_spec(layer, shape):
    zeros = (0,) * len(shape)
    return pl.BlockSpec((None,) + tuple(shape), lambda s: (layer,) + zeros)


def _const_spec(shape):
    zeros = (0,) * len(shape)
    return pl.BlockSpec(shape, lambda s: zeros)


def _load_chunk(step, w_ref, w_scr):
    rows = w_ref.shape[0]
    off = pl.multiple_of(step * rows, rows)
    w_scr[pl.ds(off, rows), :] = w_ref[...].astype(BF16)


ADA_TN = 2048


def _ada_kernel(c_ref, w_ref, b_ref, o_ref):
    c = c_ref[...]
    a = c * (1.0 / (1.0 + jnp.exp(-c)))
    a_hi = a.astype(BF16)
    a_lo = (a - a_hi.astype(F32)).astype(BF16)
    w = w_ref[0]
    w_hi = w.astype(BF16)
    w_lo = (w - w_hi.astype(F32)).astype(BF16)
    o_ref[0] = _dot(a_hi, w_hi) + _dot(a_hi, w_lo) + _dot(a_lo, w_hi) + b_ref[0]


def _adaln(cond, ada_w, ada_b):
    n_out = 6 * D_MODEL
    return pl.pallas_call(
        _ada_kernel,
        grid=(DEPTH, n_out // ADA_TN),
        in_specs=[
            pl.BlockSpec((COND_ROWS, D_MODEL), lambda i, j: (0, 0)),
            pl.BlockSpec((1, D_MODEL, ADA_TN), lambda i, j: (i, 0, j)),
            pl.BlockSpec((1, 1, ADA_TN), lambda i, j: (i, 0, j)),
        ],
        out_specs=pl.BlockSpec((1, COND_ROWS, ADA_TN), lambda i, j: (i, 0, j)),
        out_shape=jax.ShapeDtypeStruct((DEPTH, COND_ROWS, n_out), F32),
        compiler_params=_params(2),
        name="adaln",
    )(cond, ada_w, ada_b.reshape(DEPTH, 1, n_out))


GMLP_TM = 512
GMLP_SUB = 512
GMLP_TC = 1024


def _gmlp_gate_half(h, win_s, v_s):
    ssq = jnp.zeros((GMLP_SUB, 1), F32)
    for c in range(0, A_HALF, GMLP_TC):
        cols = slice(c, c + GMLP_TC)
        v_c = _gelu(_dot(h, win_s[:, A_HALF + c:A_HALF + c + GMLP_TC]))
        ssq = ssq + jnp.sum(v_c * v_c, axis=-1, keepdims=True)
        v_s[:, cols] = v_c
    return lax.rsqrt(ssq * (1.0 / A_HALF) + EPS)


def _gmlp_mix(x, h, inv_rms, mod, vg_ref, ws_ref, bs_ref, win_s, wout_s, v_s):
    acc = jnp.zeros((GMLP_SUB, D_MODEL), F32)
    for c in range(0, A_HALF, GMLP_TC):
        cols = slice(c, c + GMLP_TC)
        v_n = (v_s[:, cols] * inv_rms * vg_ref[:, cols]).astype(BF16)
        u_c = _gelu(_dot(h, win_s[:, cols]))
        gated_cols = []
        for g in range(GMLP_TC // A_GROUP_W):
            gcols = slice(g * A_GROUP_W, (g + 1) * A_GROUP_W)
            w_g = ws_ref[c // A_GROUP_W + g].astype(BF16)
            b_g = bs_ref[c // A_GROUP_W + g]
            gated = []
            for n in range(GMLP_SUB // CHUNK):
                rows = slice(n * CHUNK, (n + 1) * CHUNK)
                s = _dot(w_g, v_n[rows, gcols]) + b_g
                gated.append((u_c[rows, gcols] * s).astype(BF16))
            gated_cols.append(jnp.concatenate(gated, axis=0))
        acc = acc + _dot(jnp.concatenate(gated_cols, axis=1), wout_s[cols, :])
    return x + mod[2:3] * acc


def _gmlp_body(x, mod_ref, g_ref, vg_ref, ws_ref, bs_ref, o_ref, win_s, wout_s, v_s):
    mod = mod_ref[0]
    row_sets = [slice(r, r + GMLP_SUB) for r in range(0, GMLP_TM, GMLP_SUB)]
    hs = [(_rms(x[rows], g_ref[...]) * (1.0 + mod[1:2]) + mod[0:1]).astype(BF16) for rows in row_sets]
    inv = [_gmlp_gate_half(h, win_s, v_s.at[rows]) for rows, h in zip(row_sets, hs)]
    for rows, h, inv_rms in zip(row_sets, hs, inv):
        o_ref[rows, :] = _gmlp_mix(x[rows], h, inv_rms, mod, vg_ref, ws_ref, bs_ref, win_s, wout_s,
                                   v_s.at[rows])


def _gmlp_kernel(x_ref, mod_ref, g_ref, win_ref, vg_ref, ws_ref, bs_ref, wout_ref, o_ref,
                 win_s, wout_s, v_s):
    step = pl.program_id(0)

    @pl.when(step < LOAD_STEPS)
    def _():
        _load_chunk(step, win_ref, win_s)
        _load_chunk(step, wout_ref, wout_s)

    @pl.when(step >= LOAD_STEPS)
    def _():
        _gmlp_body(x_ref[...], mod_ref, g_ref, vg_ref, ws_ref, bs_ref, o_ref, win_s, wout_s, v_s)


def _gmlp_split_in_kernel(xp_ref, xs_ref, mod_ref, g_ref, win_ref, vg_ref, ws_ref, bs_ref, wout_ref,
                          o_ref, win_s, wout_s, v_s):
    step = pl.program_id(0)

    @pl.when(step < LOAD_STEPS)
    def _():
        _load_chunk(step, win_ref, win_s)
        _load_chunk(step, wout_ref, wout_s)

    @pl.when(step >= LOAD_STEPS)
    def _():
        x = jnp.where(_is_prompt_tile(step, GMLP_TM), xp_ref[...], xs_ref[...])
        _gmlp_body(x, mod_ref, g_ref, vg_ref, ws_ref, bs_ref, o_ref, win_s, wout_s, v_s)


def _gmlp(xs, mods, layer, norm_g, j, w_in, v_gain, ws, bs, w_out):
    tm = GMLP_TM
    split_in = isinstance(xs, tuple)
    x_specs = [_prompt_tok_spec(tm), _sample_tok_spec(tm)] if split_in else [_tok_spec(tm)]
    x_args = list(xs) if split_in else [xs]
    return pl.pallas_call(
        _gmlp_split_in_kernel if split_in else _gmlp_kernel,
        grid=(LOAD_STEPS + N_TOK // tm,),
        in_specs=x_specs + [
            _mod_spec(layer, tm),
            _layer_spec(2 * layer, (1, D_MODEL)),
            _chunk_spec(j, D_MODEL, 2 * A_HALF),
            _layer_spec(j, (1, A_HALF)),
            _layer_spec(j, (A_GROUPS, CHUNK, CHUNK)),
            _layer_spec(j, (A_GROUPS, CHUNK, 1)),
            _chunk_spec(j, A_HALF, D_MODEL),
        ],
        out_specs=_tok_spec(tm),
        out_shape=jax.ShapeDtypeStruct((N_TOK, D_MODEL), F32),
        scratch_shapes=[
            pltpu.VMEM((D_MODEL, 2 * A_HALF), BF16),
            pltpu.VMEM((A_HALF, D_MODEL), BF16),
            pltpu.VMEM((tm, A_HALF), F32),
        ],
        compiler_params=_params(),
        name="gmlp",
    )(*x_args, mods, norm_g, w_in, v_gain, ws, bs, w_out)


FFN_TM = 512
FFN_TC = 1024
FFN_CW = D_FF // LOAD_STEPS
FFN_TILE0_STEP = LOAD_STEPS - 1


def _ffn_hidden_in(x, mod, g_ref):
    return (_rms(x, g_ref[...]) * (1.0 + mod[4:5]) + mod[3:4]).astype(BF16)


def _ffn_body(x, mod_ref, g_ref, w1_s, w2_s):
    mod = mod_ref[0]
    h = _ffn_hidden_in(x, mod, g_ref)
    acc = jnp.zeros((FFN_TM, D_MODEL), F32)
    for c in range(D_FF // FFN_TC):
        cols = slice(c * FFN_TC, (c + 1) * FFN_TC)
        hid = jnp.square(jnp.maximum(_dot(h, w1_s[:, cols]), 0.0)).astype(BF16)
        acc = acc + _dot(hid, w2_s[cols, :])
    return x + mod[5:6] * acc


def _ffn_kernel(*refs, proj, split_out):
    refs = list(refs)
    x_ref, mod_ref = refs[:2]
    del refs[:2]
    if proj:
        ap_ref, as_ref, wo_ref = refs[:3]
        del refs[:3]
    g_ref, w1_ref, w2_ref = refs[:3]
    del refs[:3]
    n_out = 2 if split_out else 1
    out_refs = refs[:n_out]
    del refs[:n_out]
    if proj:
        wo_s, x0_s = refs[:2]
        del refs[:2]
    w1_s, w2_s, h0_s, acc0_s = refs
    step = pl.program_id(0)

    def tile_input():
        x = x_ref[...]
        if proj:
            is_prompt = _is_prompt_tile(step, FFN_TM, FFN_TILE0_STEP)
            a = jnp.where(is_prompt, ap_ref[...], as_ref[...])
            x = x + mod_ref[0][2:3] * _dot(a, wo_s[...])
        return x

    for k in range(LOAD_STEPS):
        @pl.when(step == k)
        def _(k=k):
            units = slice(k * FFN_CW, (k + 1) * FFN_CW)
            w1_k = w1_ref[...].astype(BF16)
            w2_k = w2_ref[...].astype(BF16)
            w1_s[:, units] = w1_k
            w2_s[units, :] = w2_k
            mod = mod_ref[0]
            if k == 0:
                if proj:
                    wo_s[...] = wo_ref[...].astype(BF16)
                x0 = tile_input()
                if proj:
                    x0_s[...] = x0
                h0_s[...] = _ffn_hidden_in(x0, mod, g_ref)
            hid = jnp.square(jnp.maximum(_dot(h0_s[...], w1_k), 0.0)).astype(BF16)
            part = _dot(hid, w2_k)
            if k == 0:
                acc0_s[...] = part
            elif k < LOAD_STEPS - 1:
                acc0_s[...] = acc0_s[...] + part
            else:
                x0 = x0_s[...] if proj else x_ref[...]
                out_refs[0][...] = x0 + mod[5:6] * (acc0_s[...] + part)

    @pl.when(step >= LOAD_STEPS)
    def _():
        y = _ffn_body(tile_input(), mod_ref, g_ref, w1_s, w2_s)
        if not split_out:
            out_refs[0][...] = y
        else:
            is_prompt = _is_prompt_tile(step, FFN_TM, FFN_TILE0_STEP)

            @pl.when(is_prompt)
            def _():
                out_refs[0][...] = y

            @pl.when(jnp.logical_not(is_prompt))
            def _():
                out_refs[1][...] = y


def _ffn(x, mods, layer, norm_g, w1, w2, attn=None, split_out=False):
    tm, t0 = FFN_TM, FFN_TILE0_STEP
    proj = attn is not None
    if split_out:
        out_specs = [_prompt_tok_spec(tm, t0), _sample_tok_spec(tm, t0)]
        out_shape = [jax.ShapeDtypeStruct((P_TOK, D_MODEL), F32), jax.ShapeDtypeStruct((S_TOK, D_MODEL), F32)]
    else:
        out_specs = _tok_spec(tm, t0)
        out_shape = jax.ShapeDtypeStruct((N_TOK, D_MODEL), F32)
    in_specs = [_tok_spec(tm, t0), _mod_spec(layer, tm, t0)]
    args = [x, mods]
    scratch = []
    if proj:
        attn_prompt, attn_sample, j, w_o = attn
        in_specs += [_prompt_tok_spec(tm, t0), _sample_tok_spec(tm, t0), _layer_spec(j, (D_MODEL, D_MODEL))]
        args += [attn_prompt, attn_sample, w_o]
        scratch += [pltpu.VMEM((D_MODEL, D_MODEL), BF16), pltpu.VMEM((tm, D_MODEL), F32)]
    last = LOAD_STEPS - 1
    in_specs += [
        _layer_spec(2 * layer + 1, (1, D_MODEL)),
        pl.BlockSpec((None, D_MODEL, FFN_CW), lambda s: (layer, 0, jnp.minimum(s, last))),
        pl.BlockSpec((None, FFN_CW, D_MODEL), lambda s: (layer, jnp.minimum(s, last), 0)),
    ]
    args += [norm_g, w1, w2]
    scratch += [
        pltpu.VMEM((D_MODEL, D_FF), BF16),
        pltpu.VMEM((D_FF, D_MODEL), BF16),
        pltpu.VMEM((tm, D_MODEL), BF16),
        pltpu.VMEM((tm, D_MODEL), F32),
    ]
    return pl.pallas_call(
        functools.partial(_ffn_kernel, proj=proj, split_out=split_out),
        grid=(t0 + N_TOK // tm,),
        in_specs=in_specs,
        out_specs=out_specs,
        out_shape=out_shape,
        scratch_shapes=scratch,
        compiler_params=_params(),
        name="attn_proj_ffn" if proj else "ffn",
    )(*args)


CONV_TM = 1024
CONV_TC = 512


def _conv_kernel(x_ref, mod_ref, g_ref, win_ref, cw_ref, cb_ref, wout_ref, o_ref, win_s, wout_s):
    step = pl.program_id(0)

    @pl.when(step < LOAD_STEPS)
    def _():
        _load_chunk(step, win_ref, win_s)
        _load_chunk(step, wout_ref, wout_s)

    @pl.when(step >= LOAD_STEPS)
    def _():
        seq_len = jnp.where(_is_prompt_tile(step, CONV_TM), SEQ, DEC_SEQ)
        pos = lax.broadcasted_iota(jnp.int32, (CONV_TM, 1), 0) & (seq_len - 1)
        has_prev = pos != 0
        has_next = pos != seq_len - 1
        x = x_ref[...]
        mod = mod_ref[0]
        h = (_rms(x, g_ref[...]) * (1.0 + mod[1:2]) + mod[0:1]).astype(BF16)
        cw = cw_ref[...]
        cb = cb_ref[...]
        acc = jnp.zeros((CONV_TM, D_MODEL), F32)
        for c in range(D_MODEL // CONV_TC):
            lo = c * CONV_TC
            cols = slice(lo, lo + CONV_TC)
            bg = _dot(h, win_s[:, lo:lo + CONV_TC])
            cg = _dot(h, win_s[:, D_MODEL + lo:D_MODEL + lo + CONV_TC])
            xt = _dot(h, win_s[:, 2 * D_MODEL + lo:2 * D_MODEL + lo + CONV_TC])
            z = cg * xt
            z_prev = jnp.where(has_prev, pltpu.roll(z, 1, axis=0), 0.0)
            z_next = jnp.where(has_next, pltpu.roll(z, CONV_TM - 1, axis=0), 0.0)
            zc = cw[0:1, cols] * z_prev + cw[1:2, cols] * z + cw[2:3, cols] * z_next + cb[:, cols]
            acc = acc + _dot((bg * zc).astype(BF16), wout_s[cols, :])
        o_ref[...] = x + mod[2:3] * acc


def _conv(x, mods, layer, norm_g, j, w_in, conv_w, conv_b, w_out):
    tm = CONV_TM
    return pl.pallas_call(
        _conv_kernel,
        grid=(LOAD_STEPS + N_TOK // tm,),
        in_specs=[
            _tok_spec(tm),
            _mod_spec(layer, tm),
            _layer_spec(2 * layer, (1, D_MODEL)),
            _chunk_spec(j, D_MODEL, 3 * D_MODEL),
            _layer_spec(j, (3, D_MODEL)),
            _layer_spec(j, (1, D_MODEL)),
            _chunk_spec(j, D_MODEL, D_MODEL),
        ],
        out_specs=_tok_spec(tm),
        out_shape=jax.ShapeDtypeStruct((N_TOK, D_MODEL), F32),
        scratch_shapes=[
            pltpu.VMEM((D_MODEL, 3 * D_MODEL), BF16),
            pltpu.VMEM((D_MODEL, D_MODEL), BF16),
        ],
        compiler_params=_params(),
        name="sconv",
    )(x, mods, norm_g, w_in, conv_w, conv_b, w_out)


QKV_TM = 512


def _qkv_kernel(x_ref, mod_ref, g_ref, w_ref, gq_ref, gk_ref, hm_ref,
                q_ref, k_ref, v_ref, kc_ref, vc_ref, w_s):
    step = pl.program_id(0)

    @pl.when(step < LOAD_STEPS)
    def _():
        _load_chunk(step, w_ref, w_s)

    @pl.when(step >= LOAD_STEPS)
    def _():
        x = x_ref[...]
        mod = mod_ref[0]
        h = (_rms(x, g_ref[...]) * (1.0 + mod[1:2]) + mod[0:1]).astype(BF16)
        head_mean = hm_ref[...]

        def head_norm(y, gain):
            sq = (y * y).astype(BF16)
            ms = jnp.concatenate(
                [_dot(sq[:, c:c + MXU_TILE], head_mean) for c in range(0, D_MODEL, MXU_TILE)], axis=1)
            return y * lax.rsqrt(ms + EPS) * gain

        q = head_norm(_dot(h, w_s[:, :D_MODEL]), gq_ref[...])
        k = head_norm(_dot(h, w_s[:, D_MODEL:2 * D_MODEL]), gk_ref[...])
        v = _dot(h, w_s[:, 2 * D_MODEL:])
        q_ref[...] = q.astype(BF16)
        k_ref[...] = k.astype(BF16)
        v_ref[...] = v.astype(BF16)

        @pl.when(_is_prompt_tile(step, QKV_TM))
        def _():
            kc_ref[...] = k
            vc_ref[...] = v


def _qkv(x, mods, layer, norm_g, j, w_qkv, q_gain, k_gain):
    tm = QKV_TM
    head_mean = jnp.asarray(
        np.kron(np.eye(MXU_TILE // HEAD_DIM), np.full((HEAD_DIM, HEAD_DIM), 1.0 / HEAD_DIM)), BF16)
    return pl.pallas_call(
        _qkv_kernel,
        grid=(LOAD_STEPS + N_TOK // tm,),
        in_specs=[
            _tok_spec(tm),
            _mod_spec(layer, tm),
            _layer_spec(2 * layer, (1, D_MODEL)),
            _chunk_spec(j, D_MODEL, 3 * D_MODEL),
            _const_spec((1, D_MODEL)),
            _const_spec((1, D_MODEL)),
            _const_spec((MXU_TILE, MXU_TILE)),
        ],
        out_specs=[_tok_spec(tm), _tok_spec(tm), _tok_spec(tm), _prompt_tok_spec(tm), _prompt_tok_spec(tm)],
        out_shape=[
            jax.ShapeDtypeStruct((N_TOK, D_MODEL), BF16),
            jax.ShapeDtypeStruct((N_TOK, D_MODEL), BF16),
            jax.ShapeDtypeStruct((N_TOK, D_MODEL), BF16),
            jax.ShapeDtypeStruct((P_TOK, D_MODEL), F32),
            jax.ShapeDtypeStruct((P_TOK, D_MODEL), F32),
        ],
        scratch_shapes=[pltpu.VMEM((D_MODEL, 3 * D_MODEL), BF16)],
        compiler_params=_params(),
        name="qkv",
    )(x, mods, norm_g, w_qkv,
      jnp.tile(q_gain[j], N_HEADS).reshape(1, D_MODEL), jnp.tile(k_gain[j], N_HEADS).reshape(1, D_MODEL),
      head_mean)


NBR_GROUP_ROWS = GRID_ROWS // 2
NBR_KEY_ROWS = NBR_GROUP_ROWS + WIN_ROWS // 2
NBR_Q = NBR_GROUP_ROWS * GRID_W
NBR_KEYS = NBR_KEY_ROWS * GRID_W


def _band_start(r):
    return min(max(r - WIN_ROWS // 2, 0), GRID_ROWS - WIN_ROWS)


def _key_row_start(r):
    return 0 if r < NBR_GROUP_ROWS else GRID_ROWS - NBR_KEY_ROWS


def _bias_kernel(rpb_ref, o_ref):
    h = pl.program_id(0)
    qc = lax.broadcasted_iota(jnp.int32, (GRID_W, GRID_W), 0)
    kc = lax.broadcasted_iota(jnp.int32, (GRID_W, GRID_W), 1)
    rel_c = jnp.clip(kc - qc + (WIN_COLS - 1), 0, RPB_COLS - 1)
    col_start = jnp.clip(qc - WIN_COLS // 2, 0, GRID_W - WIN_COLS)
    col_ok = (kc >= col_start) & (kc < col_start + WIN_COLS)
    base = h * (RPB_ROWS * RPB_COLS)
    row_tiles = []
    for rel_r in range(RPB_ROWS):
        tile = jnp.zeros((GRID_W, GRID_W), F32)
        for c in range(RPB_COLS):
            tile = jnp.where(rel_c == c, rpb_ref[base + rel_r * RPB_COLS + c], tile)
        row_tiles.append(jnp.where(col_ok, tile, MASK_VALUE))
    masked = jnp.full((GRID_W, GRID_W), MASK_VALUE, F32)
    for r in range(GRID_ROWS):
        for i in range(NBR_KEY_ROWS):
            key_row = _key_row_start(r) + i
            in_window = _band_start(r) <= key_row < _band_start(r) + WIN_ROWS
            tile = row_tiles[key_row - r + WIN_ROWS - 1] if in_window else masked
            o_ref[0, r, :, i * GRID_W:(i + 1) * GRID_W] = tile


def _window_bias(rpb):
    return pl.pallas_call(
        _bias_kernel,
        grid=(N_HEADS,),
        in_specs=[pl.BlockSpec(memory_space=pltpu.SMEM)],
        out_specs=pl.BlockSpec((1, GRID_ROWS, GRID_W, NBR_KEYS), lambda h: (h, 0, 0, 0)),
        out_shape=jax.ShapeDtypeStruct((N_HEADS, GRID_ROWS, GRID_W, NBR_KEYS), F32),
        compiler_params=_params(),
        name="window_bias",
    )(rpb.reshape(N_HEADS * RPB_ROWS * RPB_COLS))


def _head_lane_mask(half):
    lane = lax.broadcasted_iota(jnp.int32, (1, HEAD_PAIR), 1)
    return (lane >= HEAD_DIM) if half else (lane < HEAD_DIM)


def _scaled_head_queries(q2, sel):
    return jnp.where(sel, q2 * ATT_SCALE, jnp.zeros_like(q2))


def _values_and_ones(v2, sel):
    return jnp.where(sel, v2, jnp.ones_like(v2))


def _normalise(o):
    return o * (1.0 / pltpu.roll(o, HEAD_DIM, axis=1))


def _ctx_attn_kernel(q_ref, k_ref, v_ref, o_ref):
    for p in range(N_PAIRS):
        cols = slice(p * HEAD_PAIR, (p + 1) * HEAD_PAIR)
        q2 = q_ref[:, cols]
        k2 = k_ref[:, cols]
        v2 = v_ref[:, cols]
        out = None
        for half in range(2):
            sel = _head_lane_mask(half)
            s = _dot_nt(_scaled_head_queries(q2, sel), k2)
            e = jnp.exp(s - jnp.max(s, axis=-1, keepdims=True))
            o = _dot(e.astype(BF16), v2) / jnp.sum(e, axis=-1, keepdims=True)
            out = o if out is None else jnp.where(sel, o, out)
        o_ref[:, cols] = out.astype(BF16)


def _ctx_attention(q, k, v):
    spec = pl.BlockSpec((SEQ, D_MODEL), lambda b: (b, 0))
    return pl.pallas_call(
        _ctx_attn_kernel,
        grid=(BATCH,),
        in_specs=[spec, spec, spec],
        out_specs=spec,
        out_shape=jax.ShapeDtypeStruct((P_TOK, D_MODEL), BF16),
        compiler_params=_params(),
        name="ctx_attention",
    )(q, k, v)


NBR_SOFTMAX_ROWS = 32


def _nbr_attn_kernel(q_ref, k_ref, v_ref, ck_ref, cv_ref, bias_ref, o_ref):
    ck2 = ck_ref[0].astype(BF16)
    cv2 = cv_ref[0].astype(BF16)
    blocks = [(g, half) for g in range(GRID_ROWS // NBR_GROUP_ROWS) for half in range(2)]

    def scores(g, half):
        r0 = g * NBR_GROUP_ROWS
        key_lo = _key_row_start(r0) * GRID_W
        qm = _scaled_head_queries(q_ref[r0 * GRID_W:r0 * GRID_W + NBR_Q, :], _head_lane_mask(half))
        return _dot_nt(qm, k_ref[key_lo:key_lo + NBR_KEYS, :]), _dot_nt(qm, ck2)

    def attend(g, half, s_loc, s_ctx):
        r0 = g * NBR_GROUP_ROWS
        key_lo = _key_row_start(r0) * GRID_W
        sel = _head_lane_mask(half)
        e_loc, e_ctx = [], []
        for lo in range(0, NBR_Q, NBR_SOFTMAX_ROWS):
            hi = lo + NBR_SOFTMAX_ROWS
            r, q_lo = r0 + lo // GRID_W, lo % GRID_W
            sl = s_loc[lo:hi] + bias_ref[half, r, q_lo:q_lo + NBR_SOFTMAX_ROWS, :]
            sc = s_ctx[lo:hi]
            m = jnp.maximum(jnp.max(sl, axis=-1, keepdims=True), jnp.max(sc, axis=-1, keepdims=True))
            e_loc.append(jnp.exp(sl - m).astype(BF16))
            e_ctx.append(jnp.exp(sc - m).astype(BF16))
        o = _dot(jnp.concatenate(e_loc, axis=0), _values_and_ones(v_ref[key_lo:key_lo + NBR_KEYS, :], sel))
        o = o + _dot(jnp.concatenate(e_ctx, axis=0), _values_and_ones(cv2, sel))
        return _normalise(o)

    pending = scores(*blocks[0])
    outs = {}
    for i, (g, half) in enumerate(blocks):
        s_loc, s_ctx = pending
        if i + 1 < len(blocks):
            pending = scores(*blocks[i + 1])
        outs[(g, half)] = attend(g, half, s_loc, s_ctx)
    for g in range(GRID_ROWS // NBR_GROUP_ROWS):
        rows = slice(g * NBR_Q, (g + 1) * NBR_Q)
        o_ref[rows, :] = jnp.where(_head_lane_mask(0), outs[(g, 0)], outs[(g, 1)]).astype(BF16)


def _nbr_attention(q, k, v, ck, cv, bias):
    first = P_TOK // DEC_SEQ
    tok = pl.BlockSpec((DEC_SEQ, HEAD_PAIR), lambda p, b: (first + b, p))
    ctx = pl.BlockSpec((1, PAST_LEN, HEAD_PAIR), lambda p, b: (b, 0, p))
    return pl.pallas_call(
        _nbr_attn_kernel,
        grid=(N_PAIRS, DEC_BATCH),
        in_specs=[tok, tok, tok, ctx, ctx,
                  pl.BlockSpec((2, GRID_ROWS, GRID_W, NBR_KEYS), lambda p, b: (p, 0, 0, 0))],
        out_specs=pl.BlockSpec((DEC_SEQ, HEAD_PAIR), lambda p, b: (b, p)),
        out_shape=jax.ShapeDtypeStruct((S_TOK, D_MODEL), BF16),
        compiler_params=_params(2),
        name="nbr_attention",
    )(q, k, v, ck, cv, bias)


def kernel(x_prompt, x_sample, cache_k, cache_v, c, c_ctx, norm_g, ada_w, ada_b, a_w_in, a_v_gain, a_ws, a_bs, a_w_out, b_w_qkv, b_q_gain, b_k_gain, b_rpb, b_w_o, c_w_in, c_conv_w, c_conv_b, c_w_out, ff_w1, ff_w2):
    n_a = a_w_in.shape[0]
    cond = jnp.concatenate(
        [c, c_ctx[None, :], jnp.zeros((COND_ROWS - DEC_BATCH - 1, D_MODEL), F32)], axis=0)
    mods = _adaln(cond, ada_w, ada_b).reshape(DEPTH, COND_ROWS, 6, D_MODEL)
    norm_g = norm_g.reshape(2 * DEPTH, 1, D_MODEL)
    a_v_gain = a_v_gain.reshape(n_a, 1, A_HALF)
    a_bs = a_bs.reshape(n_a, A_GROUPS, CHUNK, 1)
    c_conv_b = c_conv_b.reshape(-1, 1, D_MODEL)
    x = (x_prompt.reshape(P_TOK, D_MODEL), x_sample.reshape(S_TOK, D_MODEL))
    new_k, new_v = [], []
    for i in range(DEPTH):
        kind, j = i % 3, i // 3
        if kind == 0:
            x = _gmlp(x, mods, i, norm_g, j, a_w_in, a_v_gain, a_ws, a_bs, a_w_out)
        elif kind == 1:
            q, k, v, k_new, v_new = _qkv(x, mods, i, norm_g, j, b_w_qkv, b_q_gain, b_k_gain)
            new_k.append(k_new.reshape(BATCH, SEQ, N_HEADS, HEAD_DIM))
            new_v.append(v_new.reshape(BATCH, SEQ, N_HEADS, HEAD_DIM))
            o_prompt = _ctx_attention(q, k, v)
            bias = _window_bias(b_rpb[j])
            o_sample = _nbr_attention(
                q, k, v,
                cache_k[:, j].reshape(DEC_BATCH, PAST_LEN, D_MODEL),
                cache_v[:, j].reshape(DEC_BATCH, PAST_LEN, D_MODEL), bias)
            attn = (o_prompt, o_sample, j, b_w_o)
        else:
            x = _conv(x, mods, i, norm_g, j, c_w_in, c_conv_w, c_conv_b, c_w_out)
        x = _ffn(x, mods, i, norm_g, ff_w1, ff_w2, attn=attn if kind == 1 else None,
                 split_out=(i == DEPTH - 1))
    y_prompt = x[0].reshape(BATCH, SEQ, D_MODEL)
    y_sample = x[1].reshape(DEC_BATCH, DEC_SEQ, D_MODEL)
    return (y_prompt, y_sample, jnp.stack(new_k, axis=1), jnp.stack(new_v, axis=1))
```

```python
import functools

import jax
import jax.numpy as jnp
import numpy as np
from jax import lax
from jax.experimental import pallas as pl
from jax.experimental.pallas import tpu as pltpu

D_MODEL = 1024
BATCH = 16
SEQ = 256
DEPTH = 4
DEC_BATCH = 8
DEC_SEQ = 1024
PAST_LEN = 512
GRID_W = 64
GRID_ROWS = DEC_SEQ // GRID_W
CHUNK = 128
A_HALF = 2 * D_MODEL
A_GROUPS = 8
A_GROUP_W = A_HALF // A_GROUPS
N_HEADS = 16
HEAD_DIM = D_MODEL // N_HEADS
WIN_ROWS = 8
WIN_COLS = 16
RPB_ROWS = 2 * WIN_ROWS - 1
RPB_COLS = 2 * WIN_COLS - 1
D_FF = 4 * D_MODEL
EPS = 1e-6
ATT_SCALE = HEAD_DIM ** -0.5
MASK_VALUE = -1e30

P_TOK = BATCH * SEQ
S_TOK = DEC_BATCH * DEC_SEQ
N_TOK = P_TOK + S_TOK
COND_ROWS = 16
CTX_ROW = DEC_BATCH
LOAD_STEPS = 8
HEAD_PAIR = 2 * HEAD_DIM
N_PAIRS = N_HEADS // 2
MXU_TILE = 256

F32 = jnp.float32
BF16 = jnp.bfloat16
VMEM_LIMIT = 56 * 1024 * 1024


def _params(n_axes=1, vmem=VMEM_LIMIT):
    return pltpu.CompilerParams(dimension_semantics=("arbitrary",) * n_axes, vmem_limit_bytes=vmem)


def _dot(a, b):
    return jnp.dot(a, b, preferred_element_type=F32)


def _dot_nt(a, b):
    return lax.dot_general(a, b, (((1,), (1,)), ((), ())), preferred_element_type=F32)


def _rms(x, g):
    return x * lax.rsqrt(jnp.mean(x * x, axis=-1, keepdims=True) + EPS) * g


def _gelu(x):
    return 0.5 * x * (1.0 + jnp.tanh(0.7978845608028654 * (x + 0.044715 * (x * x * x))))


def _tok_spec(tm, tile0_step=LOAD_STEPS):
    return pl.BlockSpec((tm, D_MODEL), lambda s: (jnp.maximum(s - tile0_step, 0), 0))


def _prompt_tok_spec(tm, tile0_step=LOAD_STEPS):
    last = P_TOK // tm - 1
    return pl.BlockSpec((tm, D_MODEL), lambda s: (jnp.clip(s - tile0_step, 0, last), 0))


def _sample_tok_spec(tm, tile0_step=LOAD_STEPS):
    first, last = P_TOK // tm, S_TOK // tm - 1
    return pl.BlockSpec((tm, D_MODEL), lambda s: (jnp.clip(s - tile0_step - first, 0, last), 0))


def _is_prompt_tile(step, tm, tile0_step=LOAD_STEPS):
    return step - tile0_step < P_TOK // tm


def _mod_spec(layer, tm, tile0_step=LOAD_STEPS):
    n_prompt_tiles = P_TOK // tm
    tiles_per_seq = DEC_SEQ // tm

    def index(s):
        t = jnp.maximum(s - tile0_step, 0)
        row = jnp.where(t < n_prompt_tiles, CTX_ROW, jnp.maximum(t - n_prompt_tiles, 0) // tiles_per_seq)
        return (layer, row, 0, 0)

    return pl.BlockSpec((None, 1, 6, D_MODEL), index)


def _chunk_spec(layer, rows, cols):
    return pl.BlockSpec((None, rows // LOAD_STEPS, cols),
                        lambda s: (layer, jnp.minimum(s, LOAD_STEPS - 1), 0))


def _layer_spec(layer, shape):
    zeros = (0,) * len(shape)
    return pl.BlockSpec((None,) + tuple(shape), lambda s: (layer,) + zeros)


def _const_spec(shape):
    zeros = (0,) * len(shape)
    return pl.BlockSpec(shape, lambda s: zeros)


def _load_chunk(step, w_ref, w_scr):
    rows = w_ref.shape[0]
    off = pl.multiple_of(step * rows, rows)
    w_scr[pl.ds(off, rows), :] = w_ref[...].astype(BF16)


ADA_TN = 2048


def _ada_kernel(c_ref, w_ref, b_ref, o_ref):
    c = c_ref[...]
    a = c * (1.0 / (1.0 + jnp.exp(-c)))
    a_hi = a.astype(BF16)
    a_lo = (a - a_hi.astype(F32)).astype(BF16)
    w = w_ref[0]
    w_hi = w.astype(BF16)
    w_lo = (w - w_hi.astype(F32)).astype(BF16)
    o_ref[0] = _dot(a_hi, w_hi) + _dot(a_hi, w_lo) + _dot(a_lo, w_hi) + b_ref[0]


def _adaln(cond, ada_w, ada_b):
    n_out = 6 * D_MODEL
    return pl.pallas_call(
        _ada_kernel,
        grid=(DEPTH, n_out // ADA_TN),
        in_specs=[
            pl.BlockSpec((COND_ROWS, D_MODEL), lambda i, j: (0, 0)),
            pl.BlockSpec((1, D_MODEL, ADA_TN), lambda i, j: (i, 0, j)),
            pl.BlockSpec((1, 1, ADA_TN), lambda i, j: (i, 0, j)),
        ],
        out_specs=pl.BlockSpec((1, COND_ROWS, ADA_TN), lambda i, j: (i, 0, j)),
        out_shape=jax.ShapeDtypeStruct((DEPTH, COND_ROWS, n_out), F32),
        compiler_params=_params(2),
        name="adaln",
    )(cond, ada_w, ada_b.reshape(DEPTH, 1, n_out))


GMLP_TM = 512
GMLP_SUB = 512
GMLP_TC = 1024


def _gmlp_gate_half(h, win_s, v_s):
    ssq = jnp.zeros((GMLP_SUB, 1), F32)
    for c in range(0, A_HALF, GMLP_TC):
        cols = slice(c, c + GMLP_TC)
        v_c = _gelu(_dot(h, win_s[:, A_HALF + c:A_HALF + c + GMLP_TC]))
        ssq = ssq + jnp.sum(v_c * v_c, axis=-1, keepdims=True)
        v_s[:, cols] = v_c
    return lax.rsqrt(ssq * (1.0 / A_HALF) + EPS)


def _gmlp_mix(x, h, inv_rms, mod, vg_ref, ws_ref, bs_ref, win_s, wout_s, v_s):
    acc = jnp.zeros((GMLP_SUB, D_MODEL), F32)
    for c in range(0, A_HALF, GMLP_TC):
        cols = slice(c, c + GMLP_TC)
        v_n = (v_s[:, cols] * inv_rms * vg_ref[:, cols]).astype(BF16)
        u_c = _gelu(_dot(h, win_s[:, cols]))
        gated_cols = []
        for g in range(GMLP_TC // A_GROUP_W):
            gcols = slice(g * A_GROUP_W, (g + 1) * A_GROUP_W)
            w_g = ws_ref[c // A_GROUP_W + g].astype(BF16)
            b_g = bs_ref[c // A_GROUP_W + g]
            gated = []
            for n in range(GMLP_SUB // CHUNK):
                rows = slice(n * CHUNK, (n + 1) * CHUNK)
                s = _dot(w_g, v_n[rows, gcols]) + b_g
                gated.append((u_c[rows, gcols] * s).astype(BF16))
            gated_cols.append(jnp.concatenate(gated, axis=0))
        acc = acc + _dot(jnp.concatenate(gated_cols, axis=1), wout_s[cols, :])
    return x + mod[2:3] * acc


def _gmlp_body(x, mod_ref, g_ref, vg_ref, ws_ref, bs_ref, o_ref, win_s, wout_s, v_s):
    mod = mod_ref[0]
    row_sets = [slice(r, r + GMLP_SUB) for r in range(0, GMLP_TM, GMLP_SUB)]
    hs = [(_rms(x[rows], g_ref[...]) * (1.0 + mod[1:2]) + mod[0:1]).astype(BF16) for rows in row_sets]
    inv = [_gmlp_gate_half(h, win_s, v_s.at[rows]) for rows, h in zip(row_sets, hs)]
    for rows, h, inv_rms in zip(row_sets, hs, inv):
        o_ref[rows, :] = _gmlp_mix(x[rows], h, inv_rms, mod, vg_ref, ws_ref, bs_ref, win_s, wout_s,
                                   v_s.at[rows])


def _gmlp_kernel(x_ref, mod_ref, g_ref, win_ref, vg_ref, ws_ref, bs_ref, wout_ref, o_ref,
                 win_s, wout_s, v_s):
    step = pl.program_id(0)

    @pl.when(step < LOAD_STEPS)
    def _():
        _load_chunk(step, win_ref, win_s)
        _load_chunk(step, wout_ref, wout_s)

    @pl.when(step >= LOAD_STEPS)
    def _():
        _gmlp_body(x_ref[...], mod_ref, g_ref, vg_ref, ws_ref, bs_ref, o_ref, win_s, wout_s, v_s)


def _gmlp_split_in_kernel(xp_ref, xs_ref, mod_ref, g_ref, win_ref, vg_ref, ws_ref, bs_ref, wout_ref,
                          o_ref, win_s, wout_s, v_s):
    step = pl.program_id(0)

    @pl.when(step < LOAD_STEPS)
    def _():
        _load_chunk(step, win_ref, win_s)
        _load_chunk(step, wout_ref, wout_s)

    @pl.when(step >= LOAD_STEPS)
    def _():
        x = jnp.where(_is_prompt_tile(step, GMLP_TM), xp_ref[...], xs_ref[...])
        _gmlp_body(x, mod_ref, g_ref, vg_ref, ws_ref, bs_ref, o_ref, win_s, wout_s, v_s)


def _gmlp(xs, mods, layer, norm_g, j, w_in, v_gain, ws, bs, w_out):
    tm = GMLP_TM
    split_in = isinstance(xs, tuple)
    x_specs = [_prompt_tok_spec(tm), _sample_tok_spec(tm)] if split_in else [_tok_spec(tm)]
    x_args = list(xs) if split_in else [xs]
    return pl.pallas_call(
        _gmlp_split_in_kernel if split_in else _gmlp_kernel,
        grid=(LOAD_STEPS + N_TOK // tm,),
        in_specs=x_specs + [
            _mod_spec(layer, tm),
            _layer_spec(2 * layer, (1, D_MODEL)),
            _chunk_spec(j, D_MODEL, 2 * A_HALF),
            _layer_spec(j, (1, A_HALF)),
            _layer_spec(j, (A_GROUPS, CHUNK, CHUNK)),
            _layer_spec(j, (A_GROUPS, CHUNK, 1)),
            _chunk_spec(j, A_HALF, D_MODEL),
        ],
        out_specs=_tok_spec(tm),
        out_shape=jax.ShapeDtypeStruct((N_TOK, D_MODEL), F32),
        scratch_shapes=[
            pltpu.VMEM((D_MODEL, 2 * A_HALF), BF16),
            pltpu.VMEM((A_HALF, D_MODEL), BF16),
            pltpu.VMEM((tm, A_HALF), F32),
        ],
        compiler_params=_params(),
        name="gmlp",
    )(*x_args, mods, norm_g, w_in, v_gain, ws, bs, w_out)


FFN_TM = 512
FFN_TC = 1024
FFN_CW = D_FF // LOAD_STEPS
FFN_TILE0_STEP = LOAD_STEPS - 1


def _ffn_hidden_in(x, mod, g_ref):
    return (_rms(x, g_ref[...]) * (1.0 + mod[4:5]) + mod[3:4]).astype(BF16)


def _ffn_body(x, mod_ref, g_ref, w1_s, w2_s):
    mod = mod_ref[0]
    h = _ffn_hidden_in(x, mod, g_ref)
    acc = jnp.zeros((FFN_TM, D_MODEL), F32)
    for c in range(D_FF // FFN_TC):
        cols = slice(c * FFN_TC, (c + 1) * FFN_TC)
        hid = jnp.square(jnp.maximum(_dot(h, w1_s[:, cols]), 0.0)).astype(BF16)
        acc = acc + _dot(hid, w2_s[cols, :])
    return x + mod[5:6] * acc


def _ffn_kernel(*refs, proj, split_out):
    refs = list(refs)
    x_ref, mod_ref = refs[:2]
    del refs[:2]
    if proj:
        ap_ref, as_ref, wo_ref = refs[:3]
        del refs[:3]
    g_ref, w1_ref, w2_ref = refs[:3]
    del refs[:3]
    n_out = 2 if split_out else 1
    out_refs = refs[:n_out]
    del refs[:n_out]
    if proj:
        wo_s, x0_s = refs[:2]
        del refs[:2]
    w1_s, w2_s, h0_s, acc0_s = refs
    step = pl.program_id(0)

    def tile_input():
        x = x_ref[...]
        if proj:
            is_prompt = _is_prompt_tile(step, FFN_TM, FFN_TILE0_STEP)
            a = jnp.where(is_prompt, ap_ref[...], as_ref[...])
            x = x + mod_ref[0][2:3] * _dot(a, wo_s[...])
        return x

    for k in range(LOAD_STEPS):
        @pl.when(step == k)
        def _(k=k):
            units = slice(k * FFN_CW, (k + 1) * FFN_CW)
            w1_k = w1_ref[...].astype(BF16)
            w2_k = w2_ref[...].astype(BF16)
            w1_s[:, units] = w1_k
            w2_s[units, :] = w2_k
            mod = mod_ref[0]
            if k == 0:
                if proj:
                    wo_s[...] = wo_ref[...].astype(BF16)
                x0 = tile_input()
                if proj:
                    x0_s[...] = x0
                h0_s[...] = _ffn_hidden_in(x0, mod, g_ref)
            hid = jnp.square(jnp.maximum(_dot(h0_s[...], w1_k), 0.0)).astype(BF16)
            part = _dot(hid, w2_k)
            if k == 0:
                acc0_s[...] = part
            elif k < LOAD_STEPS - 1:
                acc0_s[...] = acc0_s[...] + part
            else:
                x0 = x0_s[...] if proj else x_ref[...]
                out_refs[0][...] = x0 + mod[5:6] * (acc0_s[...] + part)

    @pl.when(step >= LOAD_STEPS)
    def _():
        y = _ffn_body(tile_input(), mod_ref, g_ref, w1_s, w2_s)
        if not split_out:
            out_refs[0][...] = y
        else:
            is_prompt = _is_prompt_tile(step, FFN_TM, FFN_TILE0_STEP)

            @pl.when(is_prompt)
            def _():
                out_refs[0][...] = y

            @pl.when(jnp.logical_not(is_prompt))
            def _():
                out_refs[1][...] = y


def _ffn(x, mods, layer, norm_g, w1, w2, attn=None, split_out=False):
    tm, t0 = FFN_TM, FFN_TILE0_STEP
    proj = attn is not None
    if split_out:
        out_specs = [_prompt_tok_spec(tm, t0), _sample_tok_spec(tm, t0)]
        out_shape = [jax.ShapeDtypeStruct((P_TOK, D_MODEL), F32), jax.ShapeDtypeStruct((S_TOK, D_MODEL), F32)]
    else:
        out_specs = _tok_spec(tm, t0)
        out_shape = jax.ShapeDtypeStruct((N_TOK, D_MODEL), F32)
    in_specs = [_tok_spec(tm, t0), _mod_spec(layer, tm, t0)]
    args = [x, mods]
    scratch = []
    if proj:
        attn_prompt, attn_sample, j, w_o = attn
        in_specs += [_prompt_tok_spec(tm, t0), _sample_tok_spec(tm, t0), _layer_spec(j, (D_MODEL, D_MODEL))]
        args += [attn_prompt, attn_sample, w_o]
        scratch += [pltpu.VMEM((D_MODEL, D_MODEL), BF16), pltpu.VMEM((tm, D_MODEL), F32)]
    last = LOAD_STEPS - 1
    in_specs += [
        _layer_spec(2 * layer + 1, (1, D_MODEL)),
        pl.BlockSpec((None, D_MODEL, FFN_CW), lambda s: (layer, 0, jnp.minimum(s, last))),
        pl.BlockSpec((None, FFN_CW, D_MODEL), lambda s: (layer, jnp.minimum(s, last), 0)),
    ]
    args += [norm_g, w1, w2]
    scratch += [
        pltpu.VMEM((D_MODEL, D_FF), BF16),
        pltpu.VMEM((D_FF, D_MODEL), BF16),
        pltpu.VMEM((tm, D_MODEL), BF16),
        pltpu.VMEM((tm, D_MODEL), F32),
    ]
    return pl.pallas_call(
        functools.partial(_ffn_kernel, proj=proj, split_out=split_out),
        grid=(t0 + N_TOK // tm,),
        in_specs=in_specs,
        out_specs=out_specs,
        out_shape=out_shape,
        scratch_shapes=scratch,
        compiler_params=_params(),
        name="attn_proj_ffn" if proj else "ffn",
    )(*args)


CONV_TM = 1024
CONV_TC = 512


def _conv_kernel(x_ref, mod_ref, g_ref, win_ref, cw_ref, cb_ref, wout_ref, o_ref, win_s, wout_s):
    step = pl.program_id(0)

    @pl.when(step < LOAD_STEPS)
    def _():
        _load_chunk(step, win_ref, win_s)
        _load_chunk(step, wout_ref, wout_s)

    @pl.when(step >= LOAD_STEPS)
    def _():
        seq_len = jnp.where(_is_prompt_tile(step, CONV_TM), SEQ, DEC_SEQ)
        pos = lax.broadcasted_iota(jnp.int32, (CONV_TM, 1), 0) & (seq_len - 1)
        has_prev = pos != 0
        has_next = pos != seq_len - 1
        x = x_ref[...]
        mod = mod_ref[0]
        h = (_rms(x, g_ref[...]) * (1.0 + mod[1:2]) + mod[0:1]).astype(BF16)
        cw = cw_ref[...]
        cb = cb_ref[...]
        acc = jnp.zeros((CONV_TM, D_MODEL), F32)
        for c in range(D_MODEL // CONV_TC):
            lo = c * CONV_TC
            cols = slice(lo, lo + CONV_TC)
            bg = _dot(h, win_s[:, lo:lo + CONV_TC])
            cg = _dot(h, win_s[:, D_MODEL + lo:D_MODEL + lo + CONV_TC])
            xt = _dot(h, win_s[:, 2 * D_MODEL + lo:2 * D_MODEL + lo + CONV_TC])
            z = cg * xt
            z_prev = jnp.where(has_prev, pltpu.roll(z, 1, axis=0), 0.0)
            z_next = jnp.where(has_next, pltpu.roll(z, CONV_TM - 1, axis=0), 0.0)
            zc = cw[0:1, cols] * z_prev + cw[1:2, cols] * z + cw[2:3, cols] * z_next + cb[:, cols]
            acc = acc + _dot((bg * zc).astype(BF16), wout_s[cols, :])
        o_ref[...] = x + mod[2:3] * acc


def _conv(x, mods, layer, norm_g, j, w_in, conv_w, conv_b, w_out):
    tm = CONV_TM
    return pl.pallas_call(
        _conv_kernel,
        grid=(LOAD_STEPS + N_TOK // tm,),
        in_specs=[
            _tok_spec(tm),
            _mod_spec(layer, tm),
            _layer_spec(2 * layer, (1, D_MODEL)),
            _chunk_spec(j, D_MODEL, 3 * D_MODEL),
            _layer_spec(j, (3, D_MODEL)),
            _layer_spec(j, (1, D_MODEL)),
            _chunk_spec(j, D_MODEL, D_MODEL),
        ],
        out_specs=_tok_spec(tm),
        out_shape=jax.ShapeDtypeStruct((N_TOK, D_MODEL), F32),
        scratch_shapes=[
            pltpu.VMEM((D_MODEL, 3 * D_MODEL), BF16),
            pltpu.VMEM((D_MODEL, D_MODEL), BF16),
        ],
        compiler_params=_params(),
        name="sconv",
    )(x, mods, norm_g, w_in, conv_w, conv_b, w_out)


QKV_TM = 512


def _qkv_kernel(x_ref, mod_ref, g_ref, w_ref, gq_ref, gk_ref, hm_ref,
                q_ref, k_ref, v_ref, kc_ref, vc_ref, w_s):
    step = pl.program_id(0)

    @pl.when(step < LOAD_STEPS)
    def _():
        _load_chunk(step, w_ref, w_s)

    @pl.when(step >= LOAD_STEPS)
    def _():
        x = x_ref[...]
        mod = mod_ref[0]
        h = (_rms(x, g_ref[...]) * (1.0 + mod[1:2]) + mod[0:1]).astype(BF16)
        head_mean = hm_ref[...]

        def head_norm(y, gain):
            sq = (y * y).astype(BF16)
            ms = jnp.concatenate(
                [_dot(sq[:, c:c + MXU_TILE], head_mean) for c in range(0, D_MODEL, MXU_TILE)], axis=1)
            return y * lax.rsqrt(ms + EPS) * gain

        q = head_norm(_dot(h, w_s[:, :D_MODEL]), gq_ref[...])
        k = head_norm(_dot(h, w_s[:, D_MODEL:2 * D_MODEL]), gk_ref[...])
        v = _dot(h, w_s[:, 2 * D_MODEL:])
        q_ref[...] = q.astype(BF16)
        k_ref[...] = k.astype(BF16)
        v_ref[...] = v.astype(BF16)

        @pl.when(_is_prompt_tile(step, QKV_TM))
        def _():
            kc_ref[...] = pltpu.einshape("m(hd)->mhd", k, h=N_HEADS)
            vc_ref[...] = pltpu.einshape("m(hd)->mhd", v, h=N_HEADS)


def _qkv(x, mods, layer, norm_g, j, w_qkv, q_gain, k_gain):
    tm = QKV_TM
    head_mean = jnp.asarray(
        np.kron(np.eye(MXU_TILE // HEAD_DIM), np.full((HEAD_DIM, HEAD_DIM), 1.0 / HEAD_DIM)), BF16)
    last_prompt_tile = P_TOK // tm - 1
    cache_spec = pl.BlockSpec(
        (tm, N_HEADS, HEAD_DIM), lambda s: (jnp.clip(s - LOAD_STEPS, 0, last_prompt_tile), 0, 0))
    return pl.pallas_call(
        _qkv_kernel,
        grid=(LOAD_STEPS + N_TOK // tm,),
        in_specs=[
            _tok_spec(tm),
            _mod_spec(layer, tm),
            _layer_spec(2 * layer, (1, D_MODEL)),
            _chunk_spec(j, D_MODEL, 3 * D_MODEL),
            _const_spec((1, D_MODEL)),
            _const_spec((1, D_MODEL)),
            _const_spec((MXU_TILE, MXU_TILE)),
        ],
        out_specs=[_tok_spec(tm), _tok_spec(tm), _tok_spec(tm), cache_spec, cache_spec],
        out_shape=[
            jax.ShapeDtypeStruct((N_TOK, D_MODEL), BF16),
            jax.ShapeDtypeStruct((N_TOK, D_MODEL), BF16),
            jax.ShapeDtypeStruct((N_TOK, D_MODEL), BF16),
            jax.ShapeDtypeStruct((P_TOK, N_HEADS, HEAD_DIM), F32),
            jax.ShapeDtypeStruct((P_TOK, N_HEADS, HEAD_DIM), F32),
        ],
        scratch_shapes=[pltpu.VMEM((D_MODEL, 3 * D_MODEL), BF16)],
        compiler_params=_params(),
        name="qkv",
    )(x, mods, norm_g, w_qkv,
      jnp.tile(q_gain[j], N_HEADS).reshape(1, D_MODEL), jnp.tile(k_gain[j], N_HEADS).reshape(1, D_MODEL),
      head_mean)


NBR_GROUP_ROWS = GRID_ROWS // 2
NBR_KEY_ROWS = NBR_GROUP_ROWS + WIN_ROWS // 2
NBR_Q = NBR_GROUP_ROWS * GRID_W
NBR_KEYS = NBR_KEY_ROWS * GRID_W


def _band_start(r):
    return min(max(r - WIN_ROWS // 2, 0), GRID_ROWS - WIN_ROWS)


def _key_row_start(r):
    return 0 if r < NBR_GROUP_ROWS else GRID_ROWS - NBR_KEY_ROWS


def _bias_kernel(rpb_ref, o_ref):
    h = pl.program_id(0)
    qc = lax.broadcasted_iota(jnp.int32, (GRID_W, GRID_W), 0)
    kc = lax.broadcasted_iota(jnp.int32, (GRID_W, GRID_W), 1)
    rel_c = jnp.clip(kc - qc + (WIN_COLS - 1), 0, RPB_COLS - 1)
    col_start = jnp.clip(qc - WIN_COLS // 2, 0, GRID_W - WIN_COLS)
    col_ok = (kc >= col_start) & (kc < col_start + WIN_COLS)
    base = h * (RPB_ROWS * RPB_COLS)
    row_tiles = []
    for rel_r in range(RPB_ROWS):
        tile = jnp.zeros((GRID_W, GRID_W), F32)
        for c in range(RPB_COLS):
            tile = jnp.where(rel_c == c, rpb_ref[base + rel_r * RPB_COLS + c], tile)
        row_tiles.append(jnp.where(col_ok, tile, MASK_VALUE))
    masked = jnp.full((GRID_W, GRID_W), MASK_VALUE, F32)
    for r in range(GRID_ROWS):
        for i in range(NBR_KEY_ROWS):
            key_row = _key_row_start(r) + i
            in_window = _band_start(r) <= key_row < _band_start(r) + WIN_ROWS
            tile = row_tiles[key_row - r + WIN_ROWS - 1] if in_window else masked
            o_ref[0, r, :, i * GRID_W:(i + 1) * GRID_W] = tile


def _window_bias(rpb):
    return pl.pallas_call(
        _bias_kernel,
        grid=(N_HEADS,),
        in_specs=[pl.BlockSpec(memory_space=pltpu.SMEM)],
        out_specs=pl.BlockSpec((1, GRID_ROWS, GRID_W, NBR_KEYS), lambda h: (h, 0, 0, 0)),
        out_shape=jax.ShapeDtypeStruct((N_HEADS, GRID_ROWS, GRID_W, NBR_KEYS), F32),
        compiler_params=_params(),
        name="window_bias",
    )(rpb.reshape(N_HEADS * RPB_ROWS * RPB_COLS))


def _head_lane_mask(half):
    lane = lax.broadcasted_iota(jnp.int32, (1, HEAD_PAIR), 1)
    return (lane >= HEAD_DIM) if half else (lane < HEAD_DIM)


def _scaled_head_queries(q2, sel):
    return jnp.where(sel, q2 * ATT_SCALE, jnp.zeros_like(q2))


def _values_and_ones(v2, sel):
    return jnp.where(sel, v2, jnp.ones_like(v2))


def _normalise(o):
    return o * (1.0 / pltpu.roll(o, HEAD_DIM, axis=1))


def _ctx_attn_kernel(q_ref, k_ref, v_ref, o_ref):
    for p in range(N_PAIRS):
        cols = slice(p * HEAD_PAIR, (p + 1) * HEAD_PAIR)
        q2 = q_ref[:, cols]
        k2 = k_ref[:, cols]
        v2 = v_ref[:, cols]
        out = None
        for half in range(2):
            sel = _head_lane_mask(half)
            s = _dot_nt(_scaled_head_queries(q2, sel), k2)
            e = jnp.exp(s - jnp.max(s, axis=-1, keepdims=True))
            o = _dot(e.astype(BF16), v2) / jnp.sum(e, axis=-1, keepdims=True)
            out = o if out is None else jnp.where(sel, o, out)
        o_ref[:, cols] = out.astype(BF16)


def _ctx_attention(q, k, v):
    spec = pl.BlockSpec((SEQ, D_MODEL), lambda b: (b, 0))
    return pl.pallas_call(
        _ctx_attn_kernel,
        grid=(BATCH,),
        in_specs=[spec, spec, spec],
        out_specs=spec,
        out_shape=jax.ShapeDtypeStruct((P_TOK, D_MODEL), BF16),
        compiler_params=_params(),
        name="ctx_attention",
    )(q, k, v)


NBR_SOFTMAX_ROWS = 32


def _nbr_attn_kernel(q_ref, k_ref, v_ref, ck_ref, cv_ref, bias_ref, o_ref):
    ck2 = ck_ref[0].astype(BF16)
    cv2 = cv_ref[0].astype(BF16)
    blocks = [(g, half) for g in range(GRID_ROWS // NBR_GROUP_ROWS) for half in range(2)]

    def scores(g, half):
        r0 = g * NBR_GROUP_ROWS
        key_lo = _key_row_start(r0) * GRID_W
        qm = _scaled_head_queries(q_ref[r0 * GRID_W:r0 * GRID_W + NBR_Q, :], _head_lane_mask(half))
        return _dot_nt(qm, k_ref[key_lo:key_lo + NBR_KEYS, :]), _dot_nt(qm, ck2)

    def attend(g, half, s_loc, s_ctx):
        r0 = g * NBR_GROUP_ROWS
        key_lo = _key_row_start(r0) * GRID_W
        sel = _head_lane_mask(half)
        e_loc, e_ctx = [], []
        for lo in range(0, NBR_Q, NBR_SOFTMAX_ROWS):
            hi = lo + NBR_SOFTMAX_ROWS
            r, q_lo = r0 + lo // GRID_W, lo % GRID_W
            sl = s_loc[lo:hi] + bias_ref[half, r, q_lo:q_lo + NBR_SOFTMAX_ROWS, :]
            sc = s_ctx[lo:hi]
            m = jnp.maximum(jnp.max(sl, axis=-1, keepdims=True), jnp.max(sc, axis=-1, keepdims=True))
            e_loc.append(jnp.exp(sl - m).astype(BF16))
            e_ctx.append(jnp.exp(sc - m).astype(BF16))
        o = _dot(jnp.concatenate(e_loc, axis=0), _values_and_ones(v_ref[key_lo:key_lo + NBR_KEYS, :], sel))
        o = o + _dot(jnp.concatenate(e_ctx, axis=0), _values_and_ones(cv2, sel))
        return _normalise(o)

    pending = scores(*blocks[0])
    outs = {}
    for i, (g, half) in enumerate(blocks):
        s_loc, s_ctx = pending
        if i + 1 < len(blocks):
            pending = scores(*blocks[i + 1])
        outs[(g, half)] = attend(g, half, s_loc, s_ctx)
    for g in range(GRID_ROWS // NBR_GROUP_ROWS):
        rows = slice(g * NBR_Q, (g + 1) * NBR_Q)
        o_ref[rows, :] = jnp.where(_head_lane_mask(0), outs[(g, 0)], outs[(g, 1)]).astype(BF16)


def _nbr_attention(q, k, v, ck, cv, bias):
    first = P_TOK // DEC_SEQ
    tok = pl.BlockSpec((DEC_SEQ, HEAD_PAIR), lambda p, b: (first + b, p))
    ctx = pl.BlockSpec((1, PAST_LEN, HEAD_PAIR), lambda p, b: (b, 0, p))
    return pl.pallas_call(
        _nbr_attn_kernel,
        grid=(N_PAIRS, DEC_BATCH),
        in_specs=[tok, tok, tok, ctx, ctx,
                  pl.BlockSpec((2, GRID_ROWS, GRID_W, NBR_KEYS), lambda p, b: (p, 0, 0, 0))],
        out_specs=pl.BlockSpec((DEC_SEQ, HEAD_PAIR), lambda p, b: (b, p)),
        out_shape=jax.ShapeDtypeStruct((S_TOK, D_MODEL), BF16),
        compiler_params=_params(2),
        name="nbr_attention",
    )(q, k, v, ck, cv, bias)


def kernel(x_prompt, x_sample, cache_k, cache_v, c, c_ctx, norm_g, ada_w, ada_b, a_w_in, a_v_gain, a_ws, a_bs, a_w_out, b_w_qkv, b_q_gain, b_k_gain, b_rpb, b_w_o, c_w_in, c_conv_w, c_conv_b, c_w_out, ff_w1, ff_w2):
    n_a = a_w_in.shape[0]
    cond = jnp.concatenate(
        [c, c_ctx[None, :], jnp.zeros((COND_ROWS - DEC_BATCH - 1, D_MODEL), F32)], axis=0)
    mods = _adaln(cond, ada_w, ada_b).reshape(DEPTH, COND_ROWS, 6, D_MODEL)
    norm_g = norm_g.reshape(2 * DEPTH, 1, D_MODEL)
    a_v_gain = a_v_gain.reshape(n_a, 1, A_HALF)
    a_bs = a_bs.reshape(n_a, A_GROUPS, CHUNK, 1)
    c_conv_b = c_conv_b.reshape(-1, 1, D_MODEL)
    x = (x_prompt.reshape(P_TOK, D_MODEL), x_sample.reshape(S_TOK, D_MODEL))
    new_k, new_v = [], []
    for i in range(DEPTH):
        kind, j = i % 3, i // 3
        if kind == 0:
            x = _gmlp(x, mods, i, norm_g, j, a_w_in, a_v_gain, a_ws, a_bs, a_w_out)
        elif kind == 1:
            q, k, v, k_new, v_new = _qkv(x, mods, i, norm_g, j, b_w_qkv, b_q_gain, b_k_gain)
            new_k.append(k_new.reshape(BATCH, SEQ, N_HEADS, HEAD_DIM))
            new_v.append(v_new.reshape(BATCH, SEQ, N_HEADS, HEAD_DIM))
            o_prompt = _ctx_attention(q, k, v)
            bias = _window_bias(b_rpb[j])
            o_sample = _nbr_attention(
                q, k, v,
                cache_k[:, j].reshape(DEC_BATCH, PAST_LEN, D_MODEL),
                cache_v[:, j].reshape(DEC_BATCH, PAST_LEN, D_MODEL), bias)
            attn = (o_prompt, o_sample, j, b_w_o)
        else:
            x = _conv(x, mods, i, norm_g, j, c_w_in, c_conv_w, c_conv_b, c_w_out)
        x = _ffn(x, mods, i, norm_g, ff_w1, ff_w2, attn=attn if kind == 1 else None,
                 split_out=(i == DEPTH - 1))
    y_prompt = x[0].reshape(BATCH, SEQ, D_MODEL)
    y_sample = x[1].reshape(DEC_BATCH, DEC_SEQ, D_MODEL)
    return (y_prompt, y_sample, jnp.stack(new_k, axis=1), jnp.stack(new_v, axis=1))
```

```python
import functools

import jax
import jax.numpy as jnp
import numpy as np
from jax import lax
from jax.experimental import pallas as pl
from jax.experimental.pallas import tpu as pltpu

D_MODEL = 1024
BATCH = 16
SEQ = 256
DEPTH = 4
DEC_BATCH = 8
DEC_SEQ = 1024
PAST_LEN = 512
GRID_W = 64
GRID_ROWS = DEC_SEQ // GRID_W
CHUNK = 128
A_HALF = 2 * D_MODEL
A_GROUPS = 8
A_GROUP_W = A_HALF // A_GROUPS
N_HEADS = 16
HEAD_DIM = D_MODEL // N_HEADS
WIN_ROWS = 8
WIN_COLS = 16
RPB_ROWS = 2 * WIN_ROWS - 1
RPB_COLS = 2 * WIN_COLS - 1
D_FF = 4 * D_MODEL
EPS = 1e-6
ATT_SCALE = HEAD_DIM ** -0.5
MASK_VALUE = -1e30

P_TOK = BATCH * SEQ
S_TOK = DEC_BATCH * DEC_SEQ
N_TOK = P_TOK + S_TOK
COND_ROWS = 16
CTX_ROW = DEC_BATCH
LOAD_STEPS = 8
HEAD_PAIR = 2 * HEAD_DIM
N_PAIRS = N_HEADS // 2
MXU_TILE = 256

F32 = jnp.float32
BF16 = jnp.bfloat16
VMEM_LIMIT = 56 * 1024 * 1024


def _params(n_axes=1, vmem=VMEM_LIMIT):
    return pltpu.CompilerParams(dimension_semantics=("arbitrary",) * n_axes, vmem_limit_bytes=vmem)


def _dot(a, b):
    return jnp.dot(a, b, preferred_element_type=F32)


def _dot_nt(a, b):
    return lax.dot_general(a, b, (((1,), (1,)), ((), ())), preferred_element_type=F32)


def _rms(x, g):
    return x * lax.rsqrt(jnp.mean(x * x, axis=-1, keepdims=True) + EPS) * g


def _gelu(x):
    return 0.5 * x * (1.0 + jnp.tanh(0.7978845608028654 * (x + 0.044715 * (x * x * x))))


def _tok_spec(tm, tile0_step=LOAD_STEPS):
    return pl.BlockSpec((tm, D_MODEL), lambda s: (jnp.maximum(s - tile0_step, 0), 0))


def _prompt_tok_spec(tm, tile0_step=LOAD_STEPS):
    last = P_TOK // tm - 1
    return pl.BlockSpec((tm, D_MODEL), lambda s: (jnp.clip(s - tile0_step, 0, last), 0))


def _sample_tok_spec(tm, tile0_step=LOAD_STEPS):
    first, last = P_TOK // tm, S_TOK // tm - 1
    return pl.BlockSpec((tm, D_MODEL), lambda s: (jnp.clip(s - tile0_step - first, 0, last), 0))


def _is_prompt_tile(step, tm, tile0_step=LOAD_STEPS):
    return step - tile0_step < P_TOK // tm


def _mod_spec(layer, tm, tile0_step=LOAD_STEPS):
    n_prompt_tiles = P_TOK // tm
    tiles_per_seq = DEC_SEQ // tm

    def index(s):
        t = jnp.maximum(s - tile0_step, 0)
        row = jnp.where(t < n_prompt_tiles, CTX_ROW, jnp.maximum(t - n_prompt_tiles, 0) // tiles_per_seq)
        return (layer, row, 0, 0)

    return pl.BlockSpec((None, 1, 6, D_MODEL), index)


def _chunk_spec(layer, rows, cols):
    return pl.BlockSpec((None, rows // LOAD_STEPS, cols),
                        lambda s: (layer, jnp.minimum(s, LOAD_STEPS - 1), 0))


def _layer_spec(layer, shape):
    zeros = (0,) * len(shape)
    return pl.BlockSpec((None,) + tuple(shape), lambda s: (layer,) + zeros)


def _const_spec(shape):
    zeros = (0,) * len(shape)
    return pl.BlockSpec(shape, lambda s: zeros)


def _load_chunk(step, w_ref, w_scr):
    rows = w_ref.shape[0]
    off = pl.multiple_of(step * rows, rows)
    w_scr[pl.ds(off, rows), :] = w_ref[...].astype(BF16)


ADA_TN = 2048


def _ada_kernel(c_ref, w_ref, b_ref, o_ref):
    c = c_ref[...]
    a = c * (1.0 / (1.0 + jnp.exp(-c)))
    a_hi = a.astype(BF16)
    a_lo = (a - a_hi.astype(F32)).astype(BF16)
    w = w_ref[0]
    w_hi = w.astype(BF16)
    w_lo = (w - w_hi.astype(F32)).astype(BF16)
    o_ref[0] = _dot(a_hi, w_hi) + _dot(a_hi, w_lo) + _dot(a_lo, w_hi) + b_ref[0]


def _adaln(cond, ada_w, ada_b):
    n_out = 6 * D_MODEL
    return pl.pallas_call(
        _ada_kernel,
        grid=(DEPTH, n_out // ADA_TN),
        in_specs=[
            pl.BlockSpec((COND_ROWS, D_MODEL), lambda i, j: (0, 0)),
            pl.BlockSpec((1, D_MODEL, ADA_TN), lambda i, j: (i, 0, j)),
            pl.BlockSpec((1, 1, ADA_TN), lambda i, j: (i, 0, j)),
        ],
        out_specs=pl.BlockSpec((1, COND_ROWS, ADA_TN), lambda i, j: (i, 0, j)),
        out_shape=jax.ShapeDtypeStruct((DEPTH, COND_ROWS, n_out), F32),
        compiler_params=_params(2),
        name="adaln",
    )(cond, ada_w, ada_b.reshape(DEPTH, 1, n_out))


GMLP_TM = 512
GMLP_SUB = 512
GMLP_TC = 1024


def _gmlp_gate_half(h, win_s, v_s):
    ssq = jnp.zeros((GMLP_SUB, 1), F32)
    for c in range(0, A_HALF, GMLP_TC):
        cols = slice(c, c + GMLP_TC)
        v_c = _gelu(_dot(h, win_s[:, A_HALF + c:A_HALF + c + GMLP_TC]))
        ssq = ssq + jnp.sum(v_c * v_c, axis=-1, keepdims=True)
        v_s[:, cols] = v_c
    return lax.rsqrt(ssq * (1.0 / A_HALF) + EPS)


def _gmlp_mix(x, h, inv_rms, mod, vg_ref, ws_ref, bs_ref, win_s, wout_s, v_s):
    acc = jnp.zeros((GMLP_SUB, D_MODEL), F32)
    for c in range(0, A_HALF, GMLP_TC):
        cols = slice(c, c + GMLP_TC)
        v_n = (v_s[:, cols] * inv_rms * vg_ref[:, cols]).astype(BF16)
        u_c = _gelu(_dot(h, win_s[:, cols]))
        gated_cols = []
        for g in range(GMLP_TC // A_GROUP_W):
            gcols = slice(g * A_GROUP_W, (g + 1) * A_GROUP_W)
            w_g = ws_ref[c // A_GROUP_W + g].astype(BF16)
            b_g = bs_ref[c // A_GROUP_W + g]
            gated = []
            for n in range(GMLP_SUB // CHUNK):
                rows = slice(n * CHUNK, (n + 1) * CHUNK)
                s = _dot(w_g, v_n[rows, gcols]) + b_g
                gated.append((u_c[rows, gcols] * s).astype(BF16))
            gated_cols.append(jnp.concatenate(gated, axis=0))
        acc = acc + _dot(jnp.concatenate(gated_cols, axis=1), wout_s[cols, :])
    return x + mod[2:3] * acc


def _gmlp_body(x, mod_ref, g_ref, vg_ref, ws_ref, bs_ref, o_ref, win_s, wout_s, v_s):
    mod = mod_ref[0]
    row_sets = [slice(r, r + GMLP_SUB) for r in range(0, GMLP_TM, GMLP_SUB)]
    hs = [(_rms(x[rows], g_ref[...]) * (1.0 + mod[1:2]) + mod[0:1]).astype(BF16) for rows in row_sets]
    inv = [_gmlp_gate_half(h, win_s, v_s.at[rows]) for rows, h in zip(row_sets, hs)]
    for rows, h, inv_rms in zip(row_sets, hs, inv):
        o_ref[rows, :] = _gmlp_mix(x[rows], h, inv_rms, mod, vg_ref, ws_ref, bs_ref, win_s, wout_s,
                                   v_s.at[rows])


def _gmlp_kernel(x_ref, mod_ref, g_ref, win_ref, vg_ref, ws_ref, bs_ref, wout_ref, o_ref,
                 win_s, wout_s, v_s):
    step = pl.program_id(0)

    @pl.when(step < LOAD_STEPS)
    def _():
        _load_chunk(step, win_ref, win_s)
        _load_chunk(step, wout_ref, wout_s)

    @pl.when(step >= LOAD_STEPS)
    def _():
        _gmlp_body(x_ref[...], mod_ref, g_ref, vg_ref, ws_ref, bs_ref, o_ref, win_s, wout_s, v_s)


def _gmlp_split_in_kernel(xp_ref, xs_ref, mod_ref, g_ref, win_ref, vg_ref, ws_ref, bs_ref, wout_ref,
                          o_ref, win_s, wout_s, v_s):
    step = pl.program_id(0)

    @pl.when(step < LOAD_STEPS)
    def _():
        _load_chunk(step, win_ref, win_s)
        _load_chunk(step, wout_ref, wout_s)

    @pl.when(step >= LOAD_STEPS)
    def _():
        x = jnp.where(_is_prompt_tile(step, GMLP_TM), xp_ref[...], xs_ref[...])
        _gmlp_body(x, mod_ref, g_ref, vg_ref, ws_ref, bs_ref, o_ref, win_s, wout_s, v_s)


def _gmlp(xs, mods, layer, norm_g, j, w_in, v_gain, ws, bs, w_out):
    tm = GMLP_TM
    split_in = isinstance(xs, tuple)
    x_specs = [_prompt_tok_spec(tm), _sample_tok_spec(tm)] if split_in else [_tok_spec(tm)]
    x_args = list(xs) if split_in else [xs]
    return pl.pallas_call(
        _gmlp_split_in_kernel if split_in else _gmlp_kernel,
        grid=(LOAD_STEPS + N_TOK // tm,),
        in_specs=x_specs + [
            _mod_spec(layer, tm),
            _layer_spec(2 * layer, (1, D_MODEL)),
            _chunk_spec(j, D_MODEL, 2 * A_HALF),
            _layer_spec(j, (1, A_HALF)),
            _layer_spec(j, (A_GROUPS, CHUNK, CHUNK)),
            _layer_spec(j, (A_GROUPS, CHUNK, 1)),
            _chunk_spec(j, A_HALF, D_MODEL),
        ],
        out_specs=_tok_spec(tm),
        out_shape=jax.ShapeDtypeStruct((N_TOK, D_MODEL), F32),
        scratch_shapes=[
            pltpu.VMEM((D_MODEL, 2 * A_HALF), BF16),
            pltpu.VMEM((A_HALF, D_MODEL), BF16),
            pltpu.VMEM((tm, A_HALF), F32),
        ],
        compiler_params=_params(),
        name="gmlp",
    )(*x_args, mods, norm_g, w_in, v_gain, ws, bs, w_out)


FFN_TM = 512
FFN_TC = 1024
FFN_CW = D_FF // LOAD_STEPS
FFN_TILE0_STEP = LOAD_STEPS - 1


def _ffn_hidden_in(x, mod, g_ref):
    return (_rms(x, g_ref[...]) * (1.0 + mod[4:5]) + mod[3:4]).astype(BF16)


def _ffn_body(x, mod_ref, g_ref, w1_s, w2_s):
    mod = mod_ref[0]
    h = _ffn_hidden_in(x, mod, g_ref)
    acc = jnp.zeros((FFN_TM, D_MODEL), F32)
    for c in range(D_FF // FFN_TC):
        cols = slice(c * FFN_TC, (c + 1) * FFN_TC)
        hid = jnp.square(jnp.maximum(_dot(h, w1_s[:, cols]), 0.0)).astype(BF16)
        acc = acc + _dot(hid, w2_s[cols, :])
    return x + mod[5:6] * acc


def _ffn_kernel(*refs, proj, split_out):
    refs = list(refs)
    x_ref, mod_ref = refs[:2]
    del refs[:2]
    if proj:
        ap_ref, as_ref, wo_ref = refs[:3]
        del refs[:3]
    g_ref, w1_ref, w2_ref = refs[:3]
    del refs[:3]
    n_out = 2 if split_out else 1
    out_refs = refs[:n_out]
    del refs[:n_out]
    if proj:
        wo_s, x0_s = refs[:2]
        del refs[:2]
    w1_s, w2_s, h0_s, acc0_s = refs
    step = pl.program_id(0)

    def tile_input():
        x = x_ref[...]
        if proj:
            is_prompt = _is_prompt_tile(step, FFN_TM, FFN_TILE0_STEP)
            a = jnp.where(is_prompt, ap_ref[...], as_ref[...])
            x = x + mod_ref[0][2:3] * _dot(a, wo_s[...])
        return x

    for k in range(LOAD_STEPS):
        @pl.when(step == k)
        def _(k=k):
            units = slice(k * FFN_CW, (k + 1) * FFN_CW)
            w1_k = w1_ref[...].astype(BF16)
            w2_k = w2_ref[...].astype(BF16)
            w1_s[:, units] = w1_k
            w2_s[units, :] = w2_k
            mod = mod_ref[0]
            if k == 0:
                if proj:
                    wo_s[...] = wo_ref[...].astype(BF16)
                x0 = tile_input()
                if proj:
                    x0_s[...] = x0
                h0_s[...] = _ffn_hidden_in(x0, mod, g_ref)
            hid = jnp.square(jnp.maximum(_dot(h0_s[...], w1_k), 0.0)).astype(BF16)
            part = _dot(hid, w2_k)
            if k == 0:
                acc0_s[...] = part
            elif k < LOAD_STEPS - 1:
                acc0_s[...] = acc0_s[...] + part
            else:
                x0 = x0_s[...] if proj else x_ref[...]
                out_refs[0][...] = x0 + mod[5:6] * (acc0_s[...] + part)

    @pl.when(step >= LOAD_STEPS)
    def _():
        y = _ffn_body(tile_input(), mod_ref, g_ref, w1_s, w2_s)
        if not split_out:
            out_refs[0][...] = y
        else:
            is_prompt = _is_prompt_tile(step, FFN_TM, FFN_TILE0_STEP)

            @pl.when(is_prompt)
            def _():
                out_refs[0][...] = y

            @pl.when(jnp.logical_not(is_prompt))
            def _():
                out_refs[1][...] = y


def _ffn(x, mods, layer, norm_g, w1, w2, attn=None, split_out=False):
    tm, t0 = FFN_TM, FFN_TILE0_STEP
    proj = attn is not None
    if split_out:
        out_specs = [_prompt_tok_spec(tm, t0), _sample_tok_spec(tm, t0)]
        out_shape = [jax.ShapeDtypeStruct((P_TOK, D_MODEL), F32), jax.ShapeDtypeStruct((S_TOK, D_MODEL), F32)]
    else:
        out_specs = _tok_spec(tm, t0)
        out_shape = jax.ShapeDtypeStruct((N_TOK, D_MODEL), F32)
    in_specs = [_tok_spec(tm, t0), _mod_spec(layer, tm, t0)]
    args = [x, mods]
    scratch = []
    if proj:
        attn_prompt, attn_sample, j, w_o = attn
        in_specs += [_prompt_tok_spec(tm, t0), _sample_tok_spec(tm, t0), _layer_spec(j, (D_MODEL, D_MODEL))]
        args += [attn_prompt, attn_sample, w_o]
        scratch += [pltpu.VMEM((D_MODEL, D_MODEL), BF16), pltpu.VMEM((tm, D_MODEL), F32)]
    last = LOAD_STEPS - 1
    in_specs += [
        _layer_spec(2 * layer + 1, (1, D_MODEL)),
        pl.BlockSpec((None, D_MODEL, FFN_CW), lambda s: (layer, 0, jnp.minimum(s, last))),
        pl.BlockSpec((None, FFN_CW, D_MODEL), lambda s: (layer, jnp.minimum(s, last), 0)),
    ]
    args += [norm_g, w1, w2]
    scratch += [
        pltpu.VMEM((D_MODEL, D_FF), BF16),
        pltpu.VMEM((D_FF, D_MODEL), BF16),
        pltpu.VMEM((tm, D_MODEL), BF16),
        pltpu.VMEM((tm, D_MODEL), F32),
    ]
    return pl.pallas_call(
        functools.partial(_ffn_kernel, proj=proj, split_out=split_out),
        grid=(t0 + N_TOK // tm,),
        in_specs=in_specs,
        out_specs=out_specs,
        out_shape=out_shape,
        scratch_shapes=scratch,
        compiler_params=_params(),
        name="attn_proj_ffn" if proj else "ffn",
    )(*args)


CONV_TM = 1024
CONV_TC = 512


def _conv_kernel(x_ref, mod_ref, g_ref, win_ref, cw_ref, cb_ref, wout_ref, o_ref, win_s, wout_s):
    step = pl.program_id(0)

    @pl.when(step < LOAD_STEPS)
    def _():
        _load_chunk(step, win_ref, win_s)
        _load_chunk(step, wout_ref, wout_s)

    @pl.when(step >= LOAD_STEPS)
    def _():
        seq_len = jnp.where(_is_prompt_tile(step, CONV_TM), SEQ, DEC_SEQ)
        pos = lax.broadcasted_iota(jnp.int32, (CONV_TM, 1), 0) & (seq_len - 1)
        has_prev = pos != 0
        has_next = pos != seq_len - 1
        x = x_ref[...]
        mod = mod_ref[0]
        h = (_rms(x, g_ref[...]) * (1.0 + mod[1:2]) + mod[0:1]).astype(BF16)
        cw = cw_ref[...]
        cb = cb_ref[...]
        acc = jnp.zeros((CONV_TM, D_MODEL), F32)
        for c in range(D_MODEL // CONV_TC):
            lo = c * CONV_TC
            cols = slice(lo, lo + CONV_TC)
            bg = _dot(h, win_s[:, lo:lo + CONV_TC])
            cg = _dot(h, win_s[:, D_MODEL + lo:D_MODEL + lo + CONV_TC])
            xt = _dot(h, win_s[:, 2 * D_MODEL + lo:2 * D_MODEL + lo + CONV_TC])
            z = cg * xt
            z_prev = jnp.where(has_prev, pltpu.roll(z, 1, axis=0), 0.0)
            z_next = jnp.where(has_next, pltpu.roll(z, CONV_TM - 1, axis=0), 0.0)
            zc = cw[0:1, cols] * z_prev + cw[1:2, cols] * z + cw[2:3, cols] * z_next + cb[:, cols]
            acc = acc + _dot((bg * zc).astype(BF16), wout_s[cols, :])
        o_ref[...] = x + mod[2:3] * acc


def _conv(x, mods, layer, norm_g, j, w_in, conv_w, conv_b, w_out):
    tm = CONV_TM
    return pl.pallas_call(
        _conv_kernel,
        grid=(LOAD_STEPS + N_TOK // tm,),
        in_specs=[
            _tok_spec(tm),
            _mod_spec(layer, tm),
            _layer_spec(2 * layer, (1, D_MODEL)),
            _chunk_spec(j, D_MODEL, 3 * D_MODEL),
            _layer_spec(j, (3, D_MODEL)),
            _layer_spec(j, (1, D_MODEL)),
            _chunk_spec(j, D_MODEL, D_MODEL),
        ],
        out_specs=_tok_spec(tm),
        out_shape=jax.ShapeDtypeStruct((N_TOK, D_MODEL), F32),
        scratch_shapes=[
            pltpu.VMEM((D_MODEL, 3 * D_MODEL), BF16),
            pltpu.VMEM((D_MODEL, D_MODEL), BF16),
        ],
        compiler_params=_params(),
        name="sconv",
    )(x, mods, norm_g, w_in, conv_w, conv_b, w_out)


QKV_TM = 512


def _qkv_kernel(x_ref, mod_ref, g_ref, w_ref, gq_ref, gk_ref, hm_ref,
                q_ref, k_ref, v_ref, kc_ref, vc_ref, w_s):
    step = pl.program_id(0)

    @pl.when(step < LOAD_STEPS)
    def _():
        _load_chunk(step, w_ref, w_s)

    @pl.when(step >= LOAD_STEPS)
    def _():
        x = x_ref[...]
        mod = mod_ref[0]
        h = (_rms(x, g_ref[...]) * (1.0 + mod[1:2]) + mod[0:1]).astype(BF16)
        head_mean = hm_ref[...]

        def head_norm(y, gain):
            sq = (y * y).astype(BF16)
            ms = jnp.concatenate(
                [_dot(sq[:, c:c + MXU_TILE], head_mean) for c in range(0, D_MODEL, MXU_TILE)], axis=1)
            return y * lax.rsqrt(ms + EPS) * gain

        q = head_norm(_dot(h, w_s[:, :D_MODEL]), gq_ref[...])
        k = head_norm(_dot(h, w_s[:, D_MODEL:2 * D_MODEL]), gk_ref[...])
        v = _dot(h, w_s[:, 2 * D_MODEL:])
        q_ref[...] = q.astype(BF16)
        k_ref[...] = k.astype(BF16)
        v_ref[...] = v.astype(BF16)

        @pl.when(_is_prompt_tile(step, QKV_TM))
        def _():
            kc_ref[...] = pltpu.einshape("m(hd)->mhd", k, h=N_HEADS)
            vc_ref[...] = pltpu.einshape("m(hd)->mhd", v, h=N_HEADS)


def _qkv(x, mods, layer, norm_g, j, w_qkv, q_gain, k_gain):
    tm = QKV_TM
    head_mean = jnp.asarray(
        np.kron(np.eye(MXU_TILE // HEAD_DIM), np.full((HEAD_DIM, HEAD_DIM), 1.0 / HEAD_DIM)), BF16)
    last_prompt_tile = P_TOK // tm - 1
    cache_spec = pl.BlockSpec(
        (tm, N_HEADS, HEAD_DIM), lambda s: (jnp.clip(s - LOAD_STEPS, 0, last_prompt_tile), 0, 0))
    return pl.pallas_call(
        _qkv_kernel,
        grid=(LOAD_STEPS + N_TOK // tm,),
        in_specs=[
            _tok_spec(tm),
            _mod_spec(layer, tm),
            _layer_spec(2 * layer, (1, D_MODEL)),
            _chunk_spec(j, D_MODEL, 3 * D_MODEL),
            _const_spec((1, D_MODEL)),
            _const_spec((1, D_MODEL)),
            _const_spec((MXU_TILE, MXU_TILE)),
        ],
        out_specs=[_tok_spec(tm), _tok_spec(tm), _tok_spec(tm), cache_spec, cache_spec],
        out_shape=[
            jax.ShapeDtypeStruct((N_TOK, D_MODEL), BF16),
            jax.ShapeDtypeStruct((N_TOK, D_MODEL), BF16),
            jax.ShapeDtypeStruct((N_TOK, D_MODEL), BF16),
            jax.ShapeDtypeStruct((P_TOK, N_HEADS, HEAD_DIM), F32),
            jax.ShapeDtypeStruct((P_TOK, N_HEADS, HEAD_DIM), F32),
        ],
        scratch_shapes=[pltpu.VMEM((D_MODEL, 3 * D_MODEL), BF16)],
        compiler_params=_params(),
        name="qkv",
    )(x, mods, norm_g, w_qkv,
      jnp.tile(q_gain[j], N_HEADS).reshape(1, D_MODEL), jnp.tile(k_gain[j], N_HEADS).reshape(1, D_MODEL),
      head_mean)


def _ctx_rows_kernel(ck_ref, cv_ref, ok_ref, ov_ref):
    ok_ref[...] = pltpu.einshape("mhd->m(hd)", ck_ref[...]).astype(BF16)
    ov_ref[...] = pltpu.einshape("mhd->m(hd)", cv_ref[...]).astype(BF16)


def _ctx_rows(cache_k, cache_v, j):
    in_spec = pl.BlockSpec((None, None, PAST_LEN, N_HEADS, HEAD_DIM), lambda b: (b, j, 0, 0, 0))
    out_spec = pl.BlockSpec((None, PAST_LEN, D_MODEL), lambda b: (b, 0, 0))
    out = jax.ShapeDtypeStruct((DEC_BATCH, PAST_LEN, D_MODEL), BF16)
    return pl.pallas_call(
        _ctx_rows_kernel,
        grid=(DEC_BATCH,),
        in_specs=[in_spec, in_spec],
        out_specs=[out_spec, out_spec],
        out_shape=[out, out],
        compiler_params=_params(),
        name="ctx_rows",
    )(cache_k, cache_v)


NBR_GROUP_ROWS = GRID_ROWS // 2
NBR_KEY_ROWS = NBR_GROUP_ROWS + WIN_ROWS // 2
NBR_Q = NBR_GROUP_ROWS * GRID_W
NBR_KEYS = NBR_KEY_ROWS * GRID_W


def _band_start(r):
    return min(max(r - WIN_ROWS // 2, 0), GRID_ROWS - WIN_ROWS)


def _key_row_start(r):
    return 0 if r < NBR_GROUP_ROWS else GRID_ROWS - NBR_KEY_ROWS


def _bias_kernel(rpb_ref, o_ref):
    h = pl.program_id(0)
    qc = lax.broadcasted_iota(jnp.int32, (GRID_W, GRID_W), 0)
    kc = lax.broadcasted_iota(jnp.int32, (GRID_W, GRID_W), 1)
    rel_c = jnp.clip(kc - qc + (WIN_COLS - 1), 0, RPB_COLS - 1)
    col_start = jnp.clip(qc - WIN_COLS // 2, 0, GRID_W - WIN_COLS)
    col_ok = (kc >= col_start) & (kc < col_start + WIN_COLS)
    base = h * (RPB_ROWS * RPB_COLS)
    row_tiles = []
    for rel_r in range(RPB_ROWS):
        tile = jnp.zeros((GRID_W, GRID_W), F32)
        for c in range(RPB_COLS):
            tile = jnp.where(rel_c == c, rpb_ref[base + rel_r * RPB_COLS + c], tile)
        row_tiles.append(jnp.where(col_ok, tile, MASK_VALUE))
    masked = jnp.full((GRID_W, GRID_W), MASK_VALUE, F32)
    for r in range(GRID_ROWS):
        for i in range(NBR_KEY_ROWS):
            key_row = _key_row_start(r) + i
            in_window = _band_start(r) <= key_row < _band_start(r) + WIN_ROWS
            tile = row_tiles[key_row - r + WIN_ROWS - 1] if in_window else masked
            o_ref[0, r, :, i * GRID_W:(i + 1) * GRID_W] = tile


def _window_bias(rpb):
    return pl.pallas_call(
        _bias_kernel,
        grid=(N_HEADS,),
        in_specs=[pl.BlockSpec(memory_space=pltpu.SMEM)],
        out_specs=pl.BlockSpec((1, GRID_ROWS, GRID_W, NBR_KEYS), lambda h: (h, 0, 0, 0)),
        out_shape=jax.ShapeDtypeStruct((N_HEADS, GRID_ROWS, GRID_W, NBR_KEYS), F32),
        compiler_params=_params(),
        name="window_bias",
    )(rpb.reshape(N_HEADS * RPB_ROWS * RPB_COLS))


def _head_lane_mask(half):
    lane = lax.broadcasted_iota(jnp.int32, (1, HEAD_PAIR), 1)
    return (lane >= HEAD_DIM) if half else (lane < HEAD_DIM)


def _scaled_head_queries(q2, sel):
    return jnp.where(sel, q2 * ATT_SCALE, jnp.zeros_like(q2))


def _values_and_ones(v2, sel):
    return jnp.where(sel, v2, jnp.ones_like(v2))


def _normalise(o):
    return o * (1.0 / pltpu.roll(o, HEAD_DIM, axis=1))


def _ctx_attn_kernel(q_ref, k_ref, v_ref, o_ref):
    for p in range(N_PAIRS):
        cols = slice(p * HEAD_PAIR, (p + 1) * HEAD_PAIR)
        q2 = q_ref[:, cols]
        k2 = k_ref[:, cols]
        v2 = v_ref[:, cols]
        out = None
        for half in range(2):
            sel = _head_lane_mask(half)
            s = _dot_nt(_scaled_head_queries(q2, sel), k2)
            e = jnp.exp(s - jnp.max(s, axis=-1, keepdims=True))
            o = _dot(e.astype(BF16), v2) / jnp.sum(e, axis=-1, keepdims=True)
            out = o if out is None else jnp.where(sel, o, out)
        o_ref[:, cols] = out.astype(BF16)


def _ctx_attention(q, k, v):
    spec = pl.BlockSpec((SEQ, D_MODEL), lambda b: (b, 0))
    return pl.pallas_call(
        _ctx_attn_kernel,
        grid=(BATCH,),
        in_specs=[spec, spec, spec],
        out_specs=spec,
        out_shape=jax.ShapeDtypeStruct((P_TOK, D_MODEL), BF16),
        compiler_params=_params(),
        name="ctx_attention",
    )(q, k, v)


NBR_SOFTMAX_ROWS = 32


def _nbr_attn_kernel(q_ref, k_ref, v_ref, ck_ref, cv_ref, bias_ref, o_ref):
    ck2 = ck_ref[0]
    cv2 = cv_ref[0]
    blocks = [(g, half) for g in range(GRID_ROWS // NBR_GROUP_ROWS) for half in range(2)]

    def scores(g, half):
        r0 = g * NBR_GROUP_ROWS
        key_lo = _key_row_start(r0) * GRID_W
        qm = _scaled_head_queries(q_ref[r0 * GRID_W:r0 * GRID_W + NBR_Q, :], _head_lane_mask(half))
        return _dot_nt(qm, k_ref[key_lo:key_lo + NBR_KEYS, :]), _dot_nt(qm, ck2)

    def attend(g, half, s_loc, s_ctx):
        r0 = g * NBR_GROUP_ROWS
        key_lo = _key_row_start(r0) * GRID_W
        sel = _head_lane_mask(half)
        e_loc, e_ctx = [], []
        for lo in range(0, NBR_Q, NBR_SOFTMAX_ROWS):
            hi = lo + NBR_SOFTMAX_ROWS
            r, q_lo = r0 + lo // GRID_W, lo % GRID_W
            sl = s_loc[lo:hi] + bias_ref[half, r, q_lo:q_lo + NBR_SOFTMAX_ROWS, :]
            sc = s_ctx[lo:hi]
            m = jnp.maximum(jnp.max(sl, axis=-1, keepdims=True), jnp.max(sc, axis=-1, keepdims=True))
            e_loc.append(jnp.exp(sl - m).astype(BF16))
            e_ctx.append(jnp.exp(sc - m).astype(BF16))
        o = _dot(jnp.concatenate(e_loc, axis=0), _values_and_ones(v_ref[key_lo:key_lo + NBR_KEYS, :], sel))
        o = o + _dot(jnp.concatenate(e_ctx, axis=0), _values_and_ones(cv2, sel))
        return _normalise(o)

    pending = scores(*blocks[0])
    outs = {}
    for i, (g, half) in enumerate(blocks):
        s_loc, s_ctx = pending
        if i + 1 < len(blocks):
            pending = scores(*blocks[i + 1])
        outs[(g, half)] = attend(g, half, s_loc, s_ctx)
    for g in range(GRID_ROWS // NBR_GROUP_ROWS):
        rows = slice(g * NBR_Q, (g + 1) * NBR_Q)
        o_ref[rows, :] = jnp.where(_head_lane_mask(0), outs[(g, 0)], outs[(g, 1)]).astype(BF16)


def _nbr_attention(q, k, v, ck, cv, bias):
    first = P_TOK // DEC_SEQ
    tok = pl.BlockSpec((DEC_SEQ, HEAD_PAIR), lambda p, b: (first + b, p))
    ctx = pl.BlockSpec((1, PAST_LEN, HEAD_PAIR), lambda p, b: (b, 0, p))
    return pl.pallas_call(
        _nbr_attn_kernel,
        grid=(N_PAIRS, DEC_BATCH),
        in_specs=[tok, tok, tok, ctx, ctx,
                  pl.BlockSpec((2, GRID_ROWS, GRID_W, NBR_KEYS), lambda p, b: (p, 0, 0, 0))],
        out_specs=pl.BlockSpec((DEC_SEQ, HEAD_PAIR), lambda p, b: (b, p)),
        out_shape=jax.ShapeDtypeStruct((S_TOK, D_MODEL), BF16),
        compiler_params=_params(2),
        name="nbr_attention",
    )(q, k, v, ck, cv, bias)


def kernel(x_prompt, x_sample, cache_k, cache_v, c, c_ctx, norm_g, ada_w, ada_b, a_w_in, a_v_gain, a_ws, a_bs, a_w_out, b_w_qkv, b_q_gain, b_k_gain, b_rpb, b_w_o, c_w_in, c_conv_w, c_conv_b, c_w_out, ff_w1, ff_w2):
    n_a = a_w_in.shape[0]
    cond = jnp.concatenate(
        [c, c_ctx[None, :], jnp.zeros((COND_ROWS - DEC_BATCH - 1, D_MODEL), F32)], axis=0)
    mods = _adaln(cond, ada_w, ada_b).reshape(DEPTH, COND_ROWS, 6, D_MODEL)
    norm_g = norm_g.reshape(2 * DEPTH, 1, D_MODEL)
    a_v_gain = a_v_gain.reshape(n_a, 1, A_HALF)
    a_bs = a_bs.reshape(n_a, A_GROUPS, CHUNK, 1)
    c_conv_b = c_conv_b.reshape(-1, 1, D_MODEL)
    x = (x_prompt.reshape(P_TOK, D_MODEL), x_sample.reshape(S_TOK, D_MODEL))
    new_k, new_v = [], []
    for i in range(DEPTH):
        kind, j = i % 3, i // 3
        if kind == 0:
            x = _gmlp(x, mods, i, norm_g, j, a_w_in, a_v_gain, a_ws, a_bs, a_w_out)
        elif kind == 1:
            q, k, v, k_new, v_new = _qkv(x, mods, i, norm_g, j, b_w_qkv, b_q_gain, b_k_gain)
            new_k.append(k_new.reshape(BATCH, SEQ, N_HEADS, HEAD_DIM))
            new_v.append(v_new.reshape(BATCH, SEQ, N_HEADS, HEAD_DIM))
            o_prompt = _ctx_attention(q, k, v)
            bias = _window_bias(b_rpb[j])
            ctx_k, ctx_v = _ctx_rows(cache_k, cache_v, j)
            o_sample = _nbr_attention(q, k, v, ctx_k, ctx_v, bias)
            attn = (o_prompt, o_sample, j, b_w_o)
        else:
            x = _conv(x, mods, i, norm_g, j, c_w_in, c_conv_w, c_conv_b, c_w_out)
        x = _ffn(x, mods, i, norm_g, ff_w1, ff_w2, attn=attn if kind == 1 else None,
                 split_out=(i == DEPTH - 1))
    y_prompt = x[0].reshape(BATCH, SEQ, D_MODEL)
    y_sample = x[1].reshape(DEC_BATCH, DEC_SEQ, D_MODEL)
    return (y_prompt, y_sample, jnp.stack(new_k, axis=1), jnp.stack(new_v, axis=1))
```

```python
import functools

import jax
import jax.numpy as jnp
import numpy as np
from jax import lax
from jax.experimental import pallas as pl
from jax.experimental.pallas import tpu as pltpu

D_MODEL = 1024
BATCH = 16
SEQ = 256
DEPTH = 4
DEC_BATCH = 8
DEC_SEQ = 1024
PAST_LEN = 512
GRID_W = 64
GRID_ROWS = DEC_SEQ // GRID_W
CHUNK = 128
A_HALF = 2 * D_MODEL
A_GROUPS = 8
A_GROUP_W = A_HALF // A_GROUPS
N_HEADS = 16
HEAD_DIM = D_MODEL // N_HEADS
WIN_ROWS = 8
WIN_COLS = 16
RPB_ROWS = 2 * WIN_ROWS - 1
RPB_COLS = 2 * WIN_COLS - 1
D_FF = 4 * D_MODEL
EPS = 1e-6
ATT_SCALE = HEAD_DIM ** -0.5
LOG2E = 1.4426950408889634
QK_PRESCALE = ATT_SCALE * LOG2E
MASK_VALUE = -1e30

P_TOK = BATCH * SEQ
S_TOK = DEC_BATCH * DEC_SEQ
N_TOK = P_TOK + S_TOK
COND_ROWS = 16
CTX_ROW = DEC_BATCH
LOAD_STEPS = 8
HEAD_PAIR = 2 * HEAD_DIM
N_PAIRS = N_HEADS // 2
MXU_TILE = 256

F32 = jnp.float32
BF16 = jnp.bfloat16
VMEM_LIMIT = 56 * 1024 * 1024


def _params(n_axes=1, vmem=VMEM_LIMIT):
    return pltpu.CompilerParams(dimension_semantics=("arbitrary",) * n_axes, vmem_limit_bytes=vmem)


def _dot(a, b):
    return jnp.dot(a, b, preferred_element_type=F32)


def _dot_nt(a, b):
    return lax.dot_general(a, b, (((1,), (1,)), ((), ())), preferred_element_type=F32)


def _rms(x, g):
    return x * lax.rsqrt(jnp.mean(x * x, axis=-1, keepdims=True) + EPS) * g


def _gelu(x):
    return 0.5 * x * (1.0 + jnp.tanh(0.7978845608028654 * (x + 0.044715 * (x * x * x))))


def _tok_spec(tm, tile0_step=LOAD_STEPS):
    return pl.BlockSpec((tm, D_MODEL), lambda s: (jnp.maximum(s - tile0_step, 0), 0))


def _prompt_tok_spec(tm, tile0_step=LOAD_STEPS):
    last = P_TOK // tm - 1
    return pl.BlockSpec((tm, D_MODEL), lambda s: (jnp.clip(s - tile0_step, 0, last), 0))


def _sample_tok_spec(tm, tile0_step=LOAD_STEPS):
    first, last = P_TOK // tm, S_TOK // tm - 1
    return pl.BlockSpec((tm, D_MODEL), lambda s: (jnp.clip(s - tile0_step - first, 0, last), 0))


def _is_prompt_tile(step, tm, tile0_step=LOAD_STEPS):
    return step - tile0_step < P_TOK // tm


def _mod_spec(layer, tm, tile0_step=LOAD_STEPS):
    n_prompt_tiles = P_TOK // tm
    tiles_per_seq = DEC_SEQ // tm

    def index(s):
        t = jnp.maximum(s - tile0_step, 0)
        row = jnp.where(t < n_prompt_tiles, CTX_ROW, jnp.maximum(t - n_prompt_tiles, 0) // tiles_per_seq)
        return (layer, row, 0, 0)

    return pl.BlockSpec((None, 1, 6, D_MODEL), index)


def _chunk_spec(layer, rows, cols):
    return pl.BlockSpec((None, rows // LOAD_STEPS, cols),
                        lambda s: (layer, jnp.minimum(s, LOAD_STEPS - 1), 0))


def _layer_spec(layer, shape):
    zeros = (0,) * len(shape)
    return pl.BlockSpec((None,) + tuple(shape), lambda s: (layer,) + zeros)


def _const_spec(shape):
    zeros = (0,) * len(shape)
    return pl.BlockSpec(shape, lambda s: zeros)


def _load_chunk(step, w_ref, w_scr):
    rows = w_ref.shape[0]
    off = pl.multiple_of(step * rows, rows)
    w_scr[pl.ds(off, rows), :] = w_ref[...].astype(BF16)


ADA_TN = 2048


def _ada_kernel(c_ref, w_ref, b_ref, o_ref):
    c = c_ref[...]
    a = c * (1.0 / (1.0 + jnp.exp(-c)))
    a_hi = a.astype(BF16)
    a_lo = (a - a_hi.astype(F32)).astype(BF16)
    w = w_ref[0]
    w_hi = w.astype(BF16)
    w_lo = (w - w_hi.astype(F32)).astype(BF16)
    o_ref[0] = _dot(a_hi, w_hi) + _dot(a_hi, w_lo) + _dot(a_lo, w_hi) + b_ref[0]


def _adaln(cond, ada_w, ada_b):
    n_out = 6 * D_MODEL
    return pl.pallas_call(
        _ada_kernel,
        grid=(DEPTH, n_out // ADA_TN),
        in_specs=[
            pl.BlockSpec((COND_ROWS, D_MODEL), lambda i, j: (0, 0)),
            pl.BlockSpec((1, D_MODEL, ADA_TN), lambda i, j: (i, 0, j)),
            pl.BlockSpec((1, 1, ADA_TN), lambda i, j: (i, 0, j)),
        ],
        out_specs=pl.BlockSpec((1, COND_ROWS, ADA_TN), lambda i, j: (i, 0, j)),
        out_shape=jax.ShapeDtypeStruct((DEPTH, COND_ROWS, n_out), F32),
        compiler_params=_params(2),
        name="adaln",
    )(cond, ada_w, ada_b.reshape(DEPTH, 1, n_out))


GMLP_TM = 512
GMLP_SUB = 512
GMLP_TC = 1024


def _gmlp_gate_half(h, win_s, v_s):
    ssq = jnp.zeros((GMLP_SUB, 1), F32)
    for c in range(0, A_HALF, GMLP_TC):
        cols = slice(c, c + GMLP_TC)
        v_c = _gelu(_dot(h, win_s[:, A_HALF + c:A_HALF + c + GMLP_TC]))
        ssq = ssq + jnp.sum(v_c * v_c, axis=-1, keepdims=True)
        v_s[:, cols] = v_c
    return lax.rsqrt(ssq * (1.0 / A_HALF) + EPS)


def _gmlp_mix(x, h, inv_rms, mod, vg_ref, ws_ref, bs_ref, win_s, wout_s, v_s):
    acc = jnp.zeros((GMLP_SUB, D_MODEL), F32)
    for c in range(0, A_HALF, GMLP_TC):
        cols = slice(c, c + GMLP_TC)
        v_n = (v_s[:, cols] * inv_rms * vg_ref[:, cols]).astype(BF16)
        u_c = _gelu(_dot(h, win_s[:, cols]))
        gated_cols = []
        for g in range(GMLP_TC // A_GROUP_W):
            gcols = slice(g * A_GROUP_W, (g + 1) * A_GROUP_W)
            w_g = ws_ref[c // A_GROUP_W + g].astype(BF16)
            b_g = bs_ref[c // A_GROUP_W + g]
            gated = []
            for n in range(GMLP_SUB // CHUNK):
                rows = slice(n * CHUNK, (n + 1) * CHUNK)
                s = _dot(w_g, v_n[rows, gcols]) + b_g
                gated.append((u_c[rows, gcols] * s).astype(BF16))
            gated_cols.append(jnp.concatenate(gated, axis=0))
        acc = acc + _dot(jnp.concatenate(gated_cols, axis=1), wout_s[cols, :])
    return x + mod[2:3] * acc


def _gmlp_body(x, mod_ref, g_ref, vg_ref, ws_ref, bs_ref, o_ref, win_s, wout_s, v_s):
    mod = mod_ref[0]
    row_sets = [slice(r, r + GMLP_SUB) for r in range(0, GMLP_TM, GMLP_SUB)]
    hs = [(_rms(x[rows], g_ref[...]) * (1.0 + mod[1:2]) + mod[0:1]).astype(BF16) for rows in row_sets]
    inv = [_gmlp_gate_half(h, win_s, v_s.at[rows]) for rows, h in zip(row_sets, hs)]
    for rows, h, inv_rms in zip(row_sets, hs, inv):
        o_ref[rows, :] = _gmlp_mix(x[rows], h, inv_rms, mod, vg_ref, ws_ref, bs_ref, win_s, wout_s,
                                   v_s.at[rows])


def _gmlp_kernel(x_ref, mod_ref, g_ref, win_ref, vg_ref, ws_ref, bs_ref, wout_ref, o_ref,
                 win_s, wout_s, v_s):
    step = pl.program_id(0)

    @pl.when(step < LOAD_STEPS)
    def _():
        _load_chunk(step, win_ref, win_s)
        _load_chunk(step, wout_ref, wout_s)

    @pl.when(step >= LOAD_STEPS)
    def _():
        _gmlp_body(x_ref[...], mod_ref, g_ref, vg_ref, ws_ref, bs_ref, o_ref, win_s, wout_s, v_s)


def _gmlp_split_in_kernel(xp_ref, xs_ref, mod_ref, g_ref, win_ref, vg_ref, ws_ref, bs_ref, wout_ref,
                          o_ref, win_s, wout_s, v_s):
    step = pl.program_id(0)

    @pl.when(step < LOAD_STEPS)
    def _():
        _load_chunk(step, win_ref, win_s)
        _load_chunk(step, wout_ref, wout_s)

    @pl.when(step >= LOAD_STEPS)
    def _():
        x = jnp.where(_is_prompt_tile(step, GMLP_TM), xp_ref[...], xs_ref[...])
        _gmlp_body(x, mod_ref, g_ref, vg_ref, ws_ref, bs_ref, o_ref, win_s, wout_s, v_s)


def _gmlp(xs, mods, layer, norm_g, j, w_in, v_gain, ws, bs, w_out):
    tm = GMLP_TM
    split_in = isinstance(xs, tuple)
    x_specs = [_prompt_tok_spec(tm), _sample_tok_spec(tm)] if split_in else [_tok_spec(tm)]
    x_args = list(xs) if split_in else [xs]
    return pl.pallas_call(
        _gmlp_split_in_kernel if split_in else _gmlp_kernel,
        grid=(LOAD_STEPS + N_TOK // tm,),
        in_specs=x_specs + [
            _mod_spec(layer, tm),
            _layer_spec(2 * layer, (1, D_MODEL)),
            _chunk_spec(j, D_MODEL, 2 * A_HALF),
            _layer_spec(j, (1, A_HALF)),
            _layer_spec(j, (A_GROUPS, CHUNK, CHUNK)),
            _layer_spec(j, (A_GROUPS, CHUNK, 1)),
            _chunk_spec(j, A_HALF, D_MODEL),
        ],
        out_specs=_tok_spec(tm),
        out_shape=jax.ShapeDtypeStruct((N_TOK, D_MODEL), F32),
        scratch_shapes=[
            pltpu.VMEM((D_MODEL, 2 * A_HALF), BF16),
            pltpu.VMEM((A_HALF, D_MODEL), BF16),
            pltpu.VMEM((tm, A_HALF), F32),
        ],
        compiler_params=_params(),
        name="gmlp",
    )(*x_args, mods, norm_g, w_in, v_gain, ws, bs, w_out)


FFN_TM = 512
FFN_TC = 1024
FFN_CW = D_FF // LOAD_STEPS
FFN_TILE0_STEP = LOAD_STEPS - 1


def _ffn_hidden_in(x, mod, g_ref):
    return (_rms(x, g_ref[...]) * (1.0 + mod[4:5]) + mod[3:4]).astype(BF16)


def _ffn_body(x, mod_ref, g_ref, w1_s, w2_s):
    mod = mod_ref[0]
    h = _ffn_hidden_in(x, mod, g_ref)
    acc = jnp.zeros((FFN_TM, D_MODEL), F32)
    for c in range(D_FF // FFN_TC):
        cols = slice(c * FFN_TC, (c + 1) * FFN_TC)
        hid = jnp.square(jnp.maximum(_dot(h, w1_s[:, cols]), 0.0)).astype(BF16)
        acc = acc + _dot(hid, w2_s[cols, :])
    return x + mod[5:6] * acc


def _ffn_kernel(*refs, proj, split_out):
    refs = list(refs)
    x_ref, mod_ref = refs[:2]
    del refs[:2]
    if proj:
        ap_ref, as_ref, wo_ref = refs[:3]
        del refs[:3]
    g_ref, w1_ref, w2_ref = refs[:3]
    del refs[:3]
    n_out = 2 if split_out else 1
    out_refs = refs[:n_out]
    del refs[:n_out]
    if proj:
        wo_s, x0_s = refs[:2]
        del refs[:2]
    w1_s, w2_s, h0_s, acc0_s = refs
    step = pl.program_id(0)

    def tile_input():
        x = x_ref[...]
        if proj:
            is_prompt = _is_prompt_tile(step, FFN_TM, FFN_TILE0_STEP)
            a = jnp.where(is_prompt, ap_ref[...], as_ref[...])
            x = x + mod_ref[0][2:3] * _dot(a, wo_s[...])
        return x

    for k in range(LOAD_STEPS):
        @pl.when(step == k)
        def _(k=k):
            units = slice(k * FFN_CW, (k + 1) * FFN_CW)
            w1_k = w1_ref[...].astype(BF16)
            w2_k = w2_ref[...].astype(BF16)
            w1_s[:, units] = w1_k
            w2_s[units, :] = w2_k
            mod = mod_ref[0]
            if k == 0:
                if proj:
                    wo_s[...] = wo_ref[...].astype(BF16)
                x0 = tile_input()
                if proj:
                    x0_s[...] = x0
                h0_s[...] = _ffn_hidden_in(x0, mod, g_ref)
            hid = jnp.square(jnp.maximum(_dot(h0_s[...], w1_k), 0.0)).astype(BF16)
            part = _dot(hid, w2_k)
            if k == 0:
                acc0_s[...] = part
            elif k < LOAD_STEPS - 1:
                acc0_s[...] = acc0_s[...] + part
            else:
                x0 = x0_s[...] if proj else x_ref[...]
                out_refs[0][...] = x0 + mod[5:6] * (acc0_s[...] + part)

    @pl.when(step >= LOAD_STEPS)
    def _():
        y = _ffn_body(tile_input(), mod_ref, g_ref, w1_s, w2_s)
        if not split_out:
            out_refs[0][...] = y
        else:
            is_prompt = _is_prompt_tile(step, FFN_TM, FFN_TILE0_STEP)

            @pl.when(is_prompt)
            def _():
                out_refs[0][...] = y

            @pl.when(jnp.logical_not(is_prompt))
            def _():
                out_refs[1][...] = y


def _ffn(x, mods, layer, norm_g, w1, w2, attn=None, split_out=False):
    tm, t0 = FFN_TM, FFN_TILE0_STEP
    proj = attn is not None
    if split_out:
        out_specs = [_prompt_tok_spec(tm, t0), _sample_tok_spec(tm, t0)]
        out_shape = [jax.ShapeDtypeStruct((P_TOK, D_MODEL), F32), jax.ShapeDtypeStruct((S_TOK, D_MODEL), F32)]
    else:
        out_specs = _tok_spec(tm, t0)
        out_shape = jax.ShapeDtypeStruct((N_TOK, D_MODEL), F32)
    in_specs = [_tok_spec(tm, t0), _mod_spec(layer, tm, t0)]
    args = [x, mods]
    scratch = []
    if proj:
        attn_prompt, attn_sample, j, w_o = attn
        in_specs += [_prompt_tok_spec(tm, t0), _sample_tok_spec(tm, t0), _layer_spec(j, (D_MODEL, D_MODEL))]
        args += [attn_prompt, attn_sample, w_o]
        scratch += [pltpu.VMEM((D_MODEL, D_MODEL), BF16), pltpu.VMEM((tm, D_MODEL), F32)]
    last = LOAD_STEPS - 1
    in_specs += [
        _layer_spec(2 * layer + 1, (1, D_MODEL)),
        pl.BlockSpec((None, D_MODEL, FFN_CW), lambda s: (layer, 0, jnp.minimum(s, last))),
        pl.BlockSpec((None, FFN_CW, D_MODEL), lambda s: (layer, jnp.minimum(s, last), 0)),
    ]
    args += [norm_g, w1, w2]
    scratch += [
        pltpu.VMEM((D_MODEL, D_FF), BF16),
        pltpu.VMEM((D_FF, D_MODEL), BF16),
        pltpu.VMEM((tm, D_MODEL), BF16),
        pltpu.VMEM((tm, D_MODEL), F32),
    ]
    return pl.pallas_call(
        functools.partial(_ffn_kernel, proj=proj, split_out=split_out),
        grid=(t0 + N_TOK // tm,),
        in_specs=in_specs,
        out_specs=out_specs,
        out_shape=out_shape,
        scratch_shapes=scratch,
        compiler_params=_params(),
        name="attn_proj_ffn" if proj else "ffn",
    )(*args)


CONV_TM = 1024
CONV_TC = 512


def _conv_kernel(x_ref, mod_ref, g_ref, win_ref, cw_ref, cb_ref, wout_ref, o_ref, win_s, wout_s):
    step = pl.program_id(0)

    @pl.when(step < LOAD_STEPS)
    def _():
        _load_chunk(step, win_ref, win_s)
        _load_chunk(step, wout_ref, wout_s)

    @pl.when(step >= LOAD_STEPS)
    def _():
        seq_len = jnp.where(_is_prompt_tile(step, CONV_TM), SEQ, DEC_SEQ)
        pos = lax.broadcasted_iota(jnp.int32, (CONV_TM, 1), 0) & (seq_len - 1)
        has_prev = pos != 0
        has_next = pos != seq_len - 1
        x = x_ref[...]
        mod = mod_ref[0]
        h = (_rms(x, g_ref[...]) * (1.0 + mod[1:2]) + mod[0:1]).astype(BF16)
        cw = cw_ref[...]
        cb = cb_ref[...]
        acc = jnp.zeros((CONV_TM, D_MODEL), F32)
        for c in range(D_MODEL // CONV_TC):
            lo = c * CONV_TC
            cols = slice(lo, lo + CONV_TC)
            bg = _dot(h, win_s[:, lo:lo + CONV_TC])
            cg = _dot(h, win_s[:, D_MODEL + lo:D_MODEL + lo + CONV_TC])
            xt = _dot(h, win_s[:, 2 * D_MODEL + lo:2 * D_MODEL + lo + CONV_TC])
            z = cg * xt
            z_prev = jnp.where(has_prev, pltpu.roll(z, 1, axis=0), 0.0)
            z_next = jnp.where(has_next, pltpu.roll(z, CONV_TM - 1, axis=0), 0.0)
            zc = cw[0:1, cols] * z_prev + cw[1:2, cols] * z + cw[2:3, cols] * z_next + cb[:, cols]
            acc = acc + _dot((bg * zc).astype(BF16), wout_s[cols, :])
        o_ref[...] = x + mod[2:3] * acc


def _conv(x, mods, layer, norm_g, j, w_in, conv_w, conv_b, w_out):
    tm = CONV_TM
    return pl.pallas_call(
        _conv_kernel,
        grid=(LOAD_STEPS + N_TOK // tm,),
        in_specs=[
            _tok_spec(tm),
            _mod_spec(layer, tm),
            _layer_spec(2 * layer, (1, D_MODEL)),
            _chunk_spec(j, D_MODEL, 3 * D_MODEL),
            _layer_spec(j, (3, D_MODEL)),
            _layer_spec(j, (1, D_MODEL)),
            _chunk_spec(j, D_MODEL, D_MODEL),
        ],
        out_specs=_tok_spec(tm),
        out_shape=jax.ShapeDtypeStruct((N_TOK, D_MODEL), F32),
        scratch_shapes=[
            pltpu.VMEM((D_MODEL, 3 * D_MODEL), BF16),
            pltpu.VMEM((D_MODEL, D_MODEL), BF16),
        ],
        compiler_params=_params(),
        name="sconv",
    )(x, mods, norm_g, w_in, conv_w, conv_b, w_out)


QKV_TM = 512


def _qkv_kernel(x_ref, mod_ref, g_ref, w_ref, gq_ref, gk_ref, hm_ref,
                q_ref, k_ref, v_ref, kc_ref, vc_ref, w_s):
    step = pl.program_id(0)

    @pl.when(step < LOAD_STEPS)
    def _():
        _load_chunk(step, w_ref, w_s)

    @pl.when(step >= LOAD_STEPS)
    def _():
        x = x_ref[...]
        mod = mod_ref[0]
        h = (_rms(x, g_ref[...]) * (1.0 + mod[1:2]) + mod[0:1]).astype(BF16)
        head_mean = hm_ref[...]

        def head_norm(y, gain):
            sq = (y * y).astype(BF16)
            ms = jnp.concatenate(
                [_dot(sq[:, c:c + MXU_TILE], head_mean) for c in range(0, D_MODEL, MXU_TILE)], axis=1)
            return y * lax.rsqrt(ms + EPS) * gain

        q = head_norm(_dot(h, w_s[:, :D_MODEL]), gq_ref[...])
        k = head_norm(_dot(h, w_s[:, D_MODEL:2 * D_MODEL]), gk_ref[...])
        v = _dot(h, w_s[:, 2 * D_MODEL:])
        q_ref[...] = (q * QK_PRESCALE).astype(BF16)
        k_ref[...] = k.astype(BF16)
        v_ref[...] = v.astype(BF16)

        @pl.when(_is_prompt_tile(step, QKV_TM))
        def _():
            kc_ref[...] = pltpu.einshape("m(hd)->mhd", k, h=N_HEADS)
            vc_ref[...] = pltpu.einshape("m(hd)->mhd", v, h=N_HEADS)


def _qkv(x, mods, layer, norm_g, j, w_qkv, q_gain, k_gain):
    tm = QKV_TM
    head_mean = jnp.asarray(
        np.kron(np.eye(MXU_TILE // HEAD_DIM), np.full((HEAD_DIM, HEAD_DIM), 1.0 / HEAD_DIM)), BF16)
    last_prompt_tile = P_TOK // tm - 1
    cache_spec = pl.BlockSpec(
        (tm, N_HEADS, HEAD_DIM), lambda s: (jnp.clip(s - LOAD_STEPS, 0, last_prompt_tile), 0, 0))
    return pl.pallas_call(
        _qkv_kernel,
        grid=(LOAD_STEPS + N_TOK // tm,),
        in_specs=[
            _tok_spec(tm),
            _mod_spec(layer, tm),
            _layer_spec(2 * layer, (1, D_MODEL)),
            _chunk_spec(j, D_MODEL, 3 * D_MODEL),
            _const_spec((1, D_MODEL)),
            _const_spec((1, D_MODEL)),
            _const_spec((MXU_TILE, MXU_TILE)),
        ],
        out_specs=[_tok_spec(tm), _tok_spec(tm), _tok_spec(tm), cache_spec, cache_spec],
        out_shape=[
            jax.ShapeDtypeStruct((N_TOK, D_MODEL), BF16),
            jax.ShapeDtypeStruct((N_TOK, D_MODEL), BF16),
            jax.ShapeDtypeStruct((N_TOK, D_MODEL), BF16),
            jax.ShapeDtypeStruct((P_TOK, N_HEADS, HEAD_DIM), F32),
            jax.ShapeDtypeStruct((P_TOK, N_HEADS, HEAD_DIM), F32),
        ],
        scratch_shapes=[pltpu.VMEM((D_MODEL, 3 * D_MODEL), BF16)],
        compiler_params=_params(),
        name="qkv",
    )(x, mods, norm_g, w_qkv,
      jnp.tile(q_gain[j], N_HEADS).reshape(1, D_MODEL), jnp.tile(k_gain[j], N_HEADS).reshape(1, D_MODEL),
      head_mean)


NBR_GROUP_ROWS = GRID_ROWS // 2
NBR_KEY_ROWS = NBR_GROUP_ROWS + WIN_ROWS // 2
NBR_Q = NBR_GROUP_ROWS * GRID_W
NBR_KEYS = NBR_KEY_ROWS * GRID_W


def _band_start(r):
    return min(max(r - WIN_ROWS // 2, 0), GRID_ROWS - WIN_ROWS)


def _key_row_start(r):
    return 0 if r < NBR_GROUP_ROWS else GRID_ROWS - NBR_KEY_ROWS


def _bias_kernel(rpb_ref, o_ref):
    h = pl.program_id(0)
    qc = lax.broadcasted_iota(jnp.int32, (GRID_W, GRID_W), 0)
    kc = lax.broadcasted_iota(jnp.int32, (GRID_W, GRID_W), 1)
    rel_c = jnp.clip(kc - qc + (WIN_COLS - 1), 0, RPB_COLS - 1)
    col_start = jnp.clip(qc - WIN_COLS // 2, 0, GRID_W - WIN_COLS)
    col_ok = (kc >= col_start) & (kc < col_start + WIN_COLS)
    base = h * (RPB_ROWS * RPB_COLS)
    row_tiles = []
    for rel_r in range(RPB_ROWS):
        tile = jnp.zeros((GRID_W, GRID_W), F32)
        for c in range(RPB_COLS):
            tile = jnp.where(rel_c == c, rpb_ref[base + rel_r * RPB_COLS + c], tile)
        row_tiles.append(jnp.where(col_ok, tile * LOG2E, MASK_VALUE))
    masked = jnp.full((GRID_W, GRID_W), MASK_VALUE, F32)
    for r in range(GRID_ROWS):
        for i in range(NBR_KEY_ROWS):
            key_row = _key_row_start(r) + i
            in_window = _band_start(r) <= key_row < _band_start(r) + WIN_ROWS
            tile = row_tiles[key_row - r + WIN_ROWS - 1] if in_window else masked
            o_ref[0, r, :, i * GRID_W:(i + 1) * GRID_W] = tile


def _window_bias(rpb):
    return pl.pallas_call(
        _bias_kernel,
        grid=(N_HEADS,),
        in_specs=[pl.BlockSpec(memory_space=pltpu.SMEM)],
        out_specs=pl.BlockSpec((1, GRID_ROWS, GRID_W, NBR_KEYS), lambda h: (h, 0, 0, 0)),
        out_shape=jax.ShapeDtypeStruct((N_HEADS, GRID_ROWS, GRID_W, NBR_KEYS), F32),
        compiler_params=_params(),
        name="window_bias",
    )(rpb.reshape(N_HEADS * RPB_ROWS * RPB_COLS))


def _head_lane_mask(half):
    lane = lax.broadcasted_iota(jnp.int32, (1, HEAD_PAIR), 1)
    return (lane >= HEAD_DIM) if half else (lane < HEAD_DIM)


def _scaled_head_queries(q2, sel):
    return jnp.where(sel, q2, jnp.zeros_like(q2))


def _values_and_ones(v2, sel):
    return jnp.where(sel, v2, jnp.ones_like(v2))


def _normalise(o):
    return o * (1.0 / pltpu.roll(o, HEAD_DIM, axis=1))


def _ctx_attn_kernel(q_ref, k_ref, v_ref, o_ref):
    for p in range(N_PAIRS):
        cols = slice(p * HEAD_PAIR, (p + 1) * HEAD_PAIR)
        q2 = q_ref[:, cols]
        k2 = k_ref[:, cols]
        v2 = v_ref[:, cols]
        out = None
        for half in range(2):
            sel = _head_lane_mask(half)
            s = _dot_nt(_scaled_head_queries(q2, sel), k2)
            e = jnp.exp2(s - jnp.max(s, axis=-1, keepdims=True))
            o = _dot(e.astype(BF16), v2) / jnp.sum(e, axis=-1, keepdims=True)
            out = o if out is None else jnp.where(sel, o, out)
        o_ref[:, cols] = out.astype(BF16)


def _ctx_attention(q, k, v):
    spec = pl.BlockSpec((SEQ, D_MODEL), lambda b: (b, 0))
    return pl.pallas_call(
        _ctx_attn_kernel,
        grid=(BATCH,),
        in_specs=[spec, spec, spec],
        out_specs=spec,
        out_shape=jax.ShapeDtypeStruct((P_TOK, D_MODEL), BF16),
        compiler_params=_params(),
        name="ctx_attention",
    )(q, k, v)


NBR_SOFTMAX_ROWS = 32


def _nbr_attn_kernel(q_ref, k_ref, v_ref, ck_ref, cv_ref, bias_ref, o_ref):
    ck2 = ck_ref[0].astype(BF16)
    cv2 = cv_ref[0].astype(BF16)
    blocks = [(g, half) for g in range(GRID_ROWS // NBR_GROUP_ROWS) for half in range(2)]

    def scores(g, half):
        r0 = g * NBR_GROUP_ROWS
        key_lo = _key_row_start(r0) * GRID_W
        qm = _scaled_head_queries(q_ref[r0 * GRID_W:r0 * GRID_W + NBR_Q, :], _head_lane_mask(half))
        bias = bias_ref[half, r0:r0 + NBR_GROUP_ROWS].reshape(NBR_Q, NBR_KEYS)
        return _dot_nt(qm, k_ref[key_lo:key_lo + NBR_KEYS, :]) + bias, _dot_nt(qm, ck2)

    def attend(g, half, s_loc, s_ctx):
        r0 = g * NBR_GROUP_ROWS
        key_lo = _key_row_start(r0) * GRID_W
        sel = _head_lane_mask(half)
        e_loc, e_ctx = [], []
        for lo in range(0, NBR_Q, NBR_SOFTMAX_ROWS):
            hi = lo + NBR_SOFTMAX_ROWS
            sl = s_loc[lo:hi]
            sc = s_ctx[lo:hi]
            m = jnp.maximum(jnp.max(sl, axis=-1, keepdims=True), jnp.max(sc, axis=-1, keepdims=True))
            e_loc.append(jnp.exp2(sl - m).astype(BF16))
            e_ctx.append(jnp.exp2(sc - m).astype(BF16))
        o = _dot(jnp.concatenate(e_loc, axis=0), _values_and_ones(v_ref[key_lo:key_lo + NBR_KEYS, :], sel))
        o = o + _dot(jnp.concatenate(e_ctx, axis=0), _values_and_ones(cv2, sel))
        return _normalise(o)

    pending = scores(*blocks[0])
    outs = {}
    for i, (g, half) in enumerate(blocks):
        s_loc, s_ctx = pending
        if i + 1 < len(blocks):
            pending = scores(*blocks[i + 1])
        outs[(g, half)] = attend(g, half, s_loc, s_ctx)
    for g in range(GRID_ROWS // NBR_GROUP_ROWS):
        rows = slice(g * NBR_Q, (g + 1) * NBR_Q)
        o_ref[rows, :] = jnp.where(_head_lane_mask(0), outs[(g, 0)], outs[(g, 1)]).astype(BF16)


def _nbr_attention(q, k, v, ck, cv, bias):
    first = P_TOK // DEC_SEQ
    tok = pl.BlockSpec((DEC_SEQ, HEAD_PAIR), lambda p, b: (first + b, p))
    ctx = pl.BlockSpec((1, PAST_LEN, HEAD_PAIR), lambda p, b: (b, 0, p))
    return pl.pallas_call(
        _nbr_attn_kernel,
        grid=(N_PAIRS, DEC_BATCH),
        in_specs=[tok, tok, tok, ctx, ctx,
                  pl.BlockSpec((2, GRID_ROWS, GRID_W, NBR_KEYS), lambda p, b: (p, 0, 0, 0))],
        out_specs=pl.BlockSpec((DEC_SEQ, HEAD_PAIR), lambda p, b: (b, p)),
        out_shape=jax.ShapeDtypeStruct((S_TOK, D_MODEL), BF16),
        compiler_params=_params(2),
        name="nbr_attention",
    )(q, k, v, ck, cv, bias)


def kernel(x_prompt, x_sample, cache_k, cache_v, c, c_ctx, norm_g, ada_w, ada_b, a_w_in, a_v_gain, a_ws, a_bs, a_w_out, b_w_qkv, b_q_gain, b_k_gain, b_rpb, b_w_o, c_w_in, c_conv_w, c_conv_b, c_w_out, ff_w1, ff_w2):
    n_a = a_w_in.shape[0]
    cond = jnp.concatenate(
        [c, c_ctx[None, :], jnp.zeros((COND_ROWS - DEC_BATCH - 1, D_MODEL), F32)], axis=0)
    mods = _adaln(cond, ada_w, ada_b).reshape(DEPTH, COND_ROWS, 6, D_MODEL)
    norm_g = norm_g.reshape(2 * DEPTH, 1, D_MODEL)
    a_v_gain = a_v_gain.reshape(n_a, 1, A_HALF)
    a_bs = a_bs.reshape(n_a, A_GROUPS, CHUNK, 1)
    c_conv_b = c_conv_b.reshape(-1, 1, D_MODEL)
    x = (x_prompt.reshape(P_TOK, D_MODEL), x_sample.reshape(S_TOK, D_MODEL))
    new_k, new_v = [], []
    for i in range(DEPTH):
        kind, j = i % 3, i // 3
        if kind == 0:
            x = _gmlp(x, mods, i, norm_g, j, a_w_in, a_v_gain, a_ws, a_bs, a_w_out)
        elif kind == 1:
            q, k, v, k_new, v_new = _qkv(x, mods, i, norm_g, j, b_w_qkv, b_q_gain, b_k_gain)
            new_k.append(k_new.reshape(BATCH, SEQ, N_HEADS, HEAD_DIM))
            new_v.append(v_new.reshape(BATCH, SEQ, N_HEADS, HEAD_DIM))
            o_prompt = _ctx_attention(q, k, v)
            bias = _window_bias(b_rpb[j])
            o_sample = _nbr_attention(
                q, k, v,
                cache_k[:, j].reshape(DEC_BATCH, PAST_LEN, D_MODEL),
                cache_v[:, j].reshape(DEC_BATCH, PAST_LEN, D_MODEL), bias)
            attn = (o_prompt, o_sample, j, b_w_o)
        else:
            x = _conv(x, mods, i, norm_g, j, c_w_in, c_conv_w, c_conv_b, c_w_out)
        x = _ffn(x, mods, i, norm_g, ff_w1, ff_w2, attn=attn if kind == 1 else None,
                 split_out=(i == DEPTH - 1))
    y_prompt = x[0].reshape(BATCH, SEQ, D_MODEL)
    y_sample = x[1].reshape(DEC_BATCH, DEC_SEQ, D_MODEL)
    return (y_prompt, y_sample, jnp.stack(new_k, axis=1), jnp.stack(new_v, axis=1))
```

```python
import functools

import jax
import jax.numpy as jnp
import numpy as np
from jax import lax
from jax.experimental import pallas as pl
from jax.experimental.pallas import tpu as pltpu

D_MODEL = 1024
BATCH = 16
SEQ = 256
DEPTH = 4
DEC_BATCH = 8
DEC_SEQ = 1024
PAST_LEN = 512
GRID_W = 64
GRID_ROWS = DEC_SEQ // GRID_W
CHUNK = 128
A_HALF = 2 * D_MODEL
A_GROUPS = 8
A_GROUP_W = A_HALF // A_GROUPS
N_HEADS = 16
HEAD_DIM = D_MODEL // N_HEADS
WIN_ROWS = 8
WIN_COLS = 16
RPB_ROWS = 2 * WIN_ROWS - 1
RPB_COLS = 2 * WIN_COLS - 1
D_FF = 4 * D_MODEL
EPS = 1e-6
ATT_SCALE = HEAD_DIM ** -0.5
MASK_VALUE = -1e30

P_TOK = BATCH * SEQ
S_TOK = DEC_BATCH * DEC_SEQ
N_TOK = P_TOK + S_TOK
COND_ROWS = 16
CTX_ROW = DEC_BATCH
LOAD_STEPS = 8
HEAD_PAIR = 2 * HEAD_DIM
N_PAIRS = N_HEADS // 2
MXU_TILE = 256

F32 = jnp.float32
BF16 = jnp.bfloat16
VMEM_LIMIT = 56 * 1024 * 1024


def _params(n_axes=1, vmem=VMEM_LIMIT):
    return pltpu.CompilerParams(dimension_semantics=("arbitrary",) * n_axes, vmem_limit_bytes=vmem)


def _dot(a, b):
    return jnp.dot(a, b, preferred_element_type=F32)


def _dot_nt(a, b):
    return lax.dot_general(a, b, (((1,), (1,)), ((), ())), preferred_element_type=F32)


def _rms(x, g):
    return x * lax.rsqrt(jnp.mean(x * x, axis=-1, keepdims=True) + EPS) * g


def _gelu(x):
    return 0.5 * x * (1.0 + jnp.tanh(0.7978845608028654 * (x + 0.044715 * (x * x * x))))


def _tok_spec(tm, tile0_step=LOAD_STEPS):
    return pl.BlockSpec((tm, D_MODEL), lambda s: (jnp.maximum(s - tile0_step, 0), 0))


def _prompt_tok_spec(tm, tile0_step=LOAD_STEPS):
    last = P_TOK // tm - 1
    return pl.BlockSpec((tm, D_MODEL), lambda s: (jnp.clip(s - tile0_step, 0, last), 0))


def _sample_tok_spec(tm, tile0_step=LOAD_STEPS):
    first, last = P_TOK // tm, S_TOK // tm - 1
    return pl.BlockSpec((tm, D_MODEL), lambda s: (jnp.clip(s - tile0_step - first, 0, last), 0))


def _is_prompt_tile(step, tm, tile0_step=LOAD_STEPS):
    return step - tile0_step < P_TOK // tm


def _mod_spec(layer, tm, tile0_step=LOAD_STEPS):
    n_prompt_tiles = P_TOK // tm
    tiles_per_seq = DEC_SEQ // tm

    def index(s):
        t = jnp.maximum(s - tile0_step, 0)
        row = jnp.where(t < n_prompt_tiles, CTX_ROW, jnp.maximum(t - n_prompt_tiles, 0) // tiles_per_seq)
        return (layer, row, 0, 0)

    return pl.BlockSpec((None, 1, 6, D_MODEL), index)


def _chunk_spec(layer, rows, cols):
    return pl.BlockSpec((None, rows // LOAD_STEPS, cols),
                        lambda s: (layer, jnp.minimum(s, LOAD_STEPS - 1), 0))


def _layer_spec(layer, shape):
    zeros = (0,) * len(shape)
    return pl.BlockSpec((None,) + tuple(shape), lambda s: (layer,) + zeros)


def _const_spec(shape):
    zeros = (0,) * len(shape)
    return pl.BlockSpec(shape, lambda s: zeros)


def _load_chunk(step, w_ref, w_scr):
    rows = w_ref.shape[0]
    off = pl.multiple_of(step * rows, rows)
    w_scr[pl.ds(off, rows), :] = w_ref[...].astype(BF16)


ADA_TN = 2048


def _ada_kernel(c_ref, w_ref, b_ref, o_ref):
    c = c_ref[...]
    a = c * (1.0 / (1.0 + jnp.exp(-c)))
    a_hi = a.astype(BF16)
    a_lo = (a - a_hi.astype(F32)).astype(BF16)
    w = w_ref[0]
    w_hi = w.astype(BF16)
    w_lo = (w - w_hi.astype(F32)).astype(BF16)
    o_ref[0] = _dot(a_hi, w_hi) + _dot(a_hi, w_lo) + _dot(a_lo, w_hi) + b_ref[0]


def _adaln(cond, ada_w, ada_b):
    n_out = 6 * D_MODEL
    return pl.pallas_call(
        _ada_kernel,
        grid=(DEPTH, n_out // ADA_TN),
        in_specs=[
            pl.BlockSpec((COND_ROWS, D_MODEL), lambda i, j: (0, 0)),
            pl.BlockSpec((1, D_MODEL, ADA_TN), lambda i, j: (i, 0, j)),
            pl.BlockSpec((1, 1, ADA_TN), lambda i, j: (i, 0, j)),
        ],
        out_specs=pl.BlockSpec((1, COND_ROWS, ADA_TN), lambda i, j: (i, 0, j)),
        out_shape=jax.ShapeDtypeStruct((DEPTH, COND_ROWS, n_out), F32),
        compiler_params=_params(2),
        name="adaln",
    )(cond, ada_w, ada_b.reshape(DEPTH, 1, n_out))


GMLP_TM = 512
GMLP_TC = 1024


def _gmlp_gate_half(h, win_s, v_s):
    ssq = jnp.zeros((GMLP_TM, 1), F32)
    for c in range(0, A_HALF, GMLP_TC):
        cols = slice(c, c + GMLP_TC)
        v_c = _gelu(_dot(h, win_s[:, A_HALF + c:A_HALF + c + GMLP_TC]))
        ssq = ssq + jnp.sum(v_c * v_c, axis=-1, keepdims=True)
        v_s[:, cols] = v_c
    return lax.rsqrt(ssq * (1.0 / A_HALF) + EPS)


def _gmlp_mix(h, inv_rms, vg_ref, ws_ref, bs_ref, win_s, wout_s, v_s):
    acc = jnp.zeros((GMLP_TM, D_MODEL), F32)
    for c in range(0, A_HALF, GMLP_TC):
        cols = slice(c, c + GMLP_TC)
        v_n = (v_s[:, cols] * inv_rms * vg_ref[:, cols]).astype(BF16)
        u_c = _gelu(_dot(h, win_s[:, cols]))
        gated_cols = []
        for g in range(GMLP_TC // A_GROUP_W):
            gcols = slice(g * A_GROUP_W, (g + 1) * A_GROUP_W)
            w_g = ws_ref[c // A_GROUP_W + g].astype(BF16)
            b_g = bs_ref[c // A_GROUP_W + g]
            gated = []
            for n in range(GMLP_TM // CHUNK):
                rows = slice(n * CHUNK, (n + 1) * CHUNK)
                s = _dot(w_g, v_n[rows, gcols]) + b_g
                gated.append((u_c[rows, gcols] * s).astype(BF16))
            gated_cols.append(jnp.concatenate(gated, axis=0))
        acc = acc + _dot(jnp.concatenate(gated_cols, axis=1), wout_s[cols, :])
    return acc


def _gmlp_body(x, mod_ref, g_ref, vg_ref, ws_ref, bs_ref, o_ref, win_s, wout_s, v_s):
    mod = mod_ref[0]
    h = (_rms(x, g_ref[...]) * (1.0 + mod[1:2]) + mod[0:1]).astype(BF16)
    inv_rms = _gmlp_gate_half(h, win_s, v_s)
    o_ref[...] = x + mod[2:3] * _gmlp_mix(h, inv_rms, vg_ref, ws_ref, bs_ref, win_s, wout_s, v_s)


def _gmlp_kernel(x_ref, mod_ref, g_ref, win_ref, vg_ref, ws_ref, bs_ref, wout_ref, o_ref,
                 win_s, wout_s, v_s):
    step = pl.program_id(0)

    @pl.when(step < LOAD_STEPS)
    def _():
        _load_chunk(step, win_ref, win_s)
        _load_chunk(step, wout_ref, wout_s)

    @pl.when(step >= LOAD_STEPS)
    def _():
        _gmlp_body(x_ref[...], mod_ref, g_ref, vg_ref, ws_ref, bs_ref, o_ref, win_s, wout_s, v_s)


def _gmlp_split_in_kernel(xp_ref, xs_ref, mod_ref, g_ref, win_ref, vg_ref, ws_ref, bs_ref, wout_ref,
                          o_ref, win_s, wout_s, v_s):
    step = pl.program_id(0)

    @pl.when(step < LOAD_STEPS)
    def _():
        _load_chunk(step, win_ref, win_s)
        _load_chunk(step, wout_ref, wout_s)

    @pl.when(step >= LOAD_STEPS)
    def _():
        x = jnp.where(_is_prompt_tile(step, GMLP_TM), xp_ref[...], xs_ref[...])
        _gmlp_body(x, mod_ref, g_ref, vg_ref, ws_ref, bs_ref, o_ref, win_s, wout_s, v_s)


def _gmlp(xs, mods, layer, norm_g, j, w_in, v_gain, ws, bs, w_out):
    tm = GMLP_TM
    split_in = isinstance(xs, tuple)
    x_specs = [_prompt_tok_spec(tm), _sample_tok_spec(tm)] if split_in else [_tok_spec(tm)]
    x_args = list(xs) if split_in else [xs]
    return pl.pallas_call(
        _gmlp_split_in_kernel if split_in else _gmlp_kernel,
        grid=(LOAD_STEPS + N_TOK // tm,),
        in_specs=x_specs + [
            _mod_spec(layer, tm),
            _layer_spec(2 * layer, (1, D_MODEL)),
            _chunk_spec(j, D_MODEL, 2 * A_HALF),
            _layer_spec(j, (1, A_HALF)),
            _layer_spec(j, (A_GROUPS, CHUNK, CHUNK)),
            _layer_spec(j, (A_GROUPS, CHUNK, 1)),
            _chunk_spec(j, A_HALF, D_MODEL),
        ],
        out_specs=_tok_spec(tm),
        out_shape=jax.ShapeDtypeStruct((N_TOK, D_MODEL), F32),
        scratch_shapes=[
            pltpu.VMEM((D_MODEL, 2 * A_HALF), BF16),
            pltpu.VMEM((A_HALF, D_MODEL), BF16),
            pltpu.VMEM((tm, A_HALF), F32),
        ],
        compiler_params=_params(),
        name="gmlp",
    )(*x_args, mods, norm_g, w_in, v_gain, ws, bs, w_out)


FFN_TM = 512
FFN_TC = 1024
FFN_CW = D_FF // LOAD_STEPS
FFN_TILE0_STEP = LOAD_STEPS - 1


def _ffn_hidden_in(x, mod, g_ref):
    return (_rms(x, g_ref[...]) * (1.0 + mod[4:5]) + mod[3:4]).astype(BF16)


def _ffn_body(x, mod_ref, g_ref, w1_s, w2_s):
    mod = mod_ref[0]
    h = _ffn_hidden_in(x, mod, g_ref)
    acc = jnp.zeros((FFN_TM, D_MODEL), F32)
    for c in range(D_FF // FFN_TC):
        cols = slice(c * FFN_TC, (c + 1) * FFN_TC)
        hid = jnp.square(jnp.maximum(_dot(h, w1_s[:, cols]), 0.0)).astype(BF16)
        acc = acc + _dot(hid, w2_s[cols, :])
    return x + mod[5:6] * acc


def _ffn_kernel(*refs, proj, split_out):
    refs = list(refs)
    x_ref, mod_ref = refs[:2]
    del refs[:2]
    if proj:
        ap_ref, as_ref, wo_ref = refs[:3]
        del refs[:3]
    g_ref, w1_ref, w2_ref = refs[:3]
    del refs[:3]
    n_out = 2 if split_out else 1
    out_refs = refs[:n_out]
    del refs[:n_out]
    if proj:
        wo_s, x0_s = refs[:2]
        del refs[:2]
    w1_s, w2_s, h0_s, acc0_s = refs
    step = pl.program_id(0)

    def tile_input():
        x = x_ref[...]
        if proj:
            is_prompt = _is_prompt_tile(step, FFN_TM, FFN_TILE0_STEP)
            a = jnp.where(is_prompt, ap_ref[...], as_ref[...])
            x = x + mod_ref[0][2:3] * _dot(a, wo_s[...])
        return x

    for k in range(LOAD_STEPS):
        @pl.when(step == k)
        def _(k=k):
            units = slice(k * FFN_CW, (k + 1) * FFN_CW)
            w1_k = w1_ref[...].astype(BF16)
            w2_k = w2_ref[...].astype(BF16)
            w1_s[:, units] = w1_k
            w2_s[units, :] = w2_k
            mod = mod_ref[0]
            if k == 0:
                if proj:
                    wo_s[...] = wo_ref[...].astype(BF16)
                x0 = tile_input()
                if proj:
                    x0_s[...] = x0
                h0_s[...] = _ffn_hidden_in(x0, mod, g_ref)
            hid = jnp.square(jnp.maximum(_dot(h0_s[...], w1_k), 0.0)).astype(BF16)
            part = _dot(hid, w2_k)
            if k == 0:
                acc0_s[...] = part
            elif k < LOAD_STEPS - 1:
                acc0_s[...] = acc0_s[...] + part
            else:
                x0 = x0_s[...] if proj else x_ref[...]
                out_refs[0][...] = x0 + mod[5:6] * (acc0_s[...] + part)

    @pl.when(step >= LOAD_STEPS)
    def _():
        y = _ffn_body(tile_input(), mod_ref, g_ref, w1_s, w2_s)
        if not split_out:
            out_refs[0][...] = y
        else:
            is_prompt = _is_prompt_tile(step, FFN_TM, FFN_TILE0_STEP)

            @pl.when(is_prompt)
            def _():
                out_refs[0][...] = y

            @pl.when(jnp.logical_not(is_prompt))
            def _():
                out_refs[1][...] = y


def _ffn(x, mods, layer, norm_g, w1, w2, attn=None, split_out=False):
    tm, t0 = FFN_TM, FFN_TILE0_STEP
    proj = attn is not None
    if split_out:
        out_specs = [_prompt_tok_spec(tm, t0), _sample_tok_spec(tm, t0)]
        out_shape = [jax.ShapeDtypeStruct((P_TOK, D_MODEL), F32), jax.ShapeDtypeStruct((S_TOK, D_MODEL), F32)]
    else:
        out_specs = _tok_spec(tm, t0)
        out_shape = jax.ShapeDtypeStruct((N_TOK, D_MODEL), F32)
    in_specs = [_tok_spec(tm, t0), _mod_spec(layer, tm, t0)]
    args = [x, mods]
    scratch = []
    if proj:
        attn_prompt, attn_sample, j, w_o = attn
        in_specs += [_prompt_tok_spec(tm, t0), _sample_tok_spec(tm, t0), _layer_spec(j, (D_MODEL, D_MODEL))]
        args += [attn_prompt, attn_sample, w_o]
        scratch += [pltpu.VMEM((D_MODEL, D_MODEL), BF16), pltpu.VMEM((tm, D_MODEL), F32)]
    last = LOAD_STEPS - 1
    in_specs += [
        _layer_spec(2 * layer + 1, (1, D_MODEL)),
        pl.BlockSpec((None, D_MODEL, FFN_CW), lambda s: (layer, 0, jnp.minimum(s, last))),
        pl.BlockSpec((None, FFN_CW, D_MODEL), lambda s: (layer, jnp.minimum(s, last), 0)),
    ]
    args += [norm_g, w1, w2]
    scratch += [
        pltpu.VMEM((D_MODEL, D_FF), BF16),
        pltpu.VMEM((D_FF, D_MODEL), BF16),
        pltpu.VMEM((tm, D_MODEL), BF16),
        pltpu.VMEM((tm, D_MODEL), F32),
    ]
    return pl.pallas_call(
        functools.partial(_ffn_kernel, proj=proj, split_out=split_out),
        grid=(t0 + N_TOK // tm,),
        in_specs=in_specs,
        out_specs=out_specs,
        out_shape=out_shape,
        scratch_shapes=scratch,
        compiler_params=_params(),
        name="attn_proj_ffn" if proj else "ffn",
    )(*args)


CONV_TM = 1024
CONV_TC = 512


def _conv_kernel(x_ref, mod_ref, g_ref, win_ref, cw_ref, cb_ref, wout_ref, o_ref, win_s, wout_s):
    step = pl.program_id(0)

    @pl.when(step < LOAD_STEPS)
    def _():
        _load_chunk(step, win_ref, win_s)
        _load_chunk(step, wout_ref, wout_s)

    @pl.when(step >= LOAD_STEPS)
    def _():
        seq_len = jnp.where(_is_prompt_tile(step, CONV_TM), SEQ, DEC_SEQ)
        pos = lax.broadcasted_iota(jnp.int32, (CONV_TM, 1), 0) & (seq_len - 1)
        has_prev = pos != 0
        has_next = pos != seq_len - 1
        x = x_ref[...]
        mod = mod_ref[0]
        h = (_rms(x, g_ref[...]) * (1.0 + mod[1:2]) + mod[0:1]).astype(BF16)
        cw = cw_ref[...]
        cb = cb_ref[...]
        acc = jnp.zeros((CONV_TM, D_MODEL), F32)
        for c in range(D_MODEL // CONV_TC):
            lo = c * CONV_TC
            cols = slice(lo, lo + CONV_TC)
            bg = _dot(h, win_s[:, lo:lo + CONV_TC])
            cg = _dot(h, win_s[:, D_MODEL + lo:D_MODEL + lo + CONV_TC])
            xt = _dot(h, win_s[:, 2 * D_MODEL + lo:2 * D_MODEL + lo + CONV_TC])
            z = cg * xt
            z_prev = jnp.where(has_prev, pltpu.roll(z, 1, axis=0), 0.0)
            z_next = jnp.where(has_next, pltpu.roll(z, CONV_TM - 1, axis=0), 0.0)
            zc = cw[0:1, cols] * z_prev + cw[1:2, cols] * z + cw[2:3, cols] * z_next + cb[:, cols]
            acc = acc + _dot((bg * zc).astype(BF16), wout_s[cols, :])
        o_ref[...] = x + mod[2:3] * acc


def _conv(x, mods, layer, norm_g, j, w_in, conv_w, conv_b, w_out):
    tm = CONV_TM
    return pl.pallas_call(
        _conv_kernel,
        grid=(LOAD_STEPS + N_TOK // tm,),
        in_specs=[
            _tok_spec(tm),
            _mod_spec(layer, tm),
            _layer_spec(2 * layer, (1, D_MODEL)),
            _chunk_spec(j, D_MODEL, 3 * D_MODEL),
            _layer_spec(j, (3, D_MODEL)),
            _layer_spec(j, (1, D_MODEL)),
            _chunk_spec(j, D_MODEL, D_MODEL),
        ],
        out_specs=_tok_spec(tm),
        out_shape=jax.ShapeDtypeStruct((N_TOK, D_MODEL), F32),
        scratch_shapes=[
            pltpu.VMEM((D_MODEL, 3 * D_MODEL), BF16),
            pltpu.VMEM((D_MODEL, D_MODEL), BF16),
        ],
        compiler_params=_params(),
        name="sconv",
    )(x, mods, norm_g, w_in, conv_w, conv_b, w_out)


QKV_TM = 512


def _qkv_kernel(x_ref, mod_ref, g_ref, w_ref, gq_ref, gk_ref, hm_ref,
                q_ref, k_ref, v_ref, kc_ref, vc_ref, w_s):
    step = pl.program_id(0)

    @pl.when(step < LOAD_STEPS)
    def _():
        _load_chunk(step, w_ref, w_s)

    @pl.when(step >= LOAD_STEPS)
    def _():
        x = x_ref[...]
        mod = mod_ref[0]
        h = (_rms(x, g_ref[...]) * (1.0 + mod[1:2]) + mod[0:1]).astype(BF16)
        head_mean = hm_ref[...]

        def head_norm(y, gain):
            sq = (y * y).astype(BF16)
            ms = jnp.concatenate(
                [_dot(sq[:, c:c + MXU_TILE], head_mean) for c in range(0, D_MODEL, MXU_TILE)], axis=1)
            return y * lax.rsqrt(ms + EPS) * gain

        q = head_norm(_dot(h, w_s[:, :D_MODEL]), gq_ref[...])
        k = head_norm(_dot(h, w_s[:, D_MODEL:2 * D_MODEL]), gk_ref[...])
        v = _dot(h, w_s[:, 2 * D_MODEL:])
        q_ref[...] = q.astype(BF16)
        k_ref[...] = k.astype(BF16)
        v_ref[...] = v.astype(BF16)

        @pl.when(_is_prompt_tile(step, QKV_TM))
        def _():
            kc_ref[...] = pltpu.einshape("m(hd)->mhd", k, h=N_HEADS)
            vc_ref[...] = pltpu.einshape("m(hd)->mhd", v, h=N_HEADS)


def _qkv(x, mods, layer, norm_g, j, w_qkv, q_gain, k_gain):
    tm = QKV_TM
    head_mean = jnp.asarray(
        np.kron(np.eye(MXU_TILE // HEAD_DIM), np.full((HEAD_DIM, HEAD_DIM), 1.0 / HEAD_DIM)), BF16)
    last_prompt_tile = P_TOK // tm - 1
    cache_spec = pl.BlockSpec(
        (tm, N_HEADS, HEAD_DIM), lambda s: (jnp.clip(s - LOAD_STEPS, 0, last_prompt_tile), 0, 0))
    return pl.pallas_call(
        _qkv_kernel,
        grid=(LOAD_STEPS + N_TOK // tm,),
        in_specs=[
            _tok_spec(tm),
            _mod_spec(layer, tm),
            _layer_spec(2 * layer, (1, D_MODEL)),
            _chunk_spec(j, D_MODEL, 3 * D_MODEL),
            _const_spec((1, D_MODEL)),
            _const_spec((1, D_MODEL)),
            _const_spec((MXU_TILE, MXU_TILE)),
        ],
        out_specs=[_tok_spec(tm), _tok_spec(tm), _tok_spec(tm), cache_spec, cache_spec],
        out_shape=[
            jax.ShapeDtypeStruct((N_TOK, D_MODEL), BF16),
            jax.ShapeDtypeStruct((N_TOK, D_MODEL), BF16),
            jax.ShapeDtypeStruct((N_TOK, D_MODEL), BF16),
            jax.ShapeDtypeStruct((P_TOK, N_HEADS, HEAD_DIM), F32),
            jax.ShapeDtypeStruct((P_TOK, N_HEADS, HEAD_DIM), F32),
        ],
        scratch_shapes=[pltpu.VMEM((D_MODEL, 3 * D_MODEL), BF16)],
        compiler_params=_params(),
        name="qkv",
    )(x, mods, norm_g, w_qkv,
      jnp.tile(q_gain[j], N_HEADS).reshape(1, D_MODEL), jnp.tile(k_gain[j], N_HEADS).reshape(1, D_MODEL),
      head_mean)


NBR_GROUP_ROWS = GRID_ROWS // 2
NBR_KEY_ROWS = NBR_GROUP_ROWS + WIN_ROWS // 2
NBR_Q = NBR_GROUP_ROWS * GRID_W
NBR_KEYS = NBR_KEY_ROWS * GRID_W


def _band_start(r):
    return min(max(r - WIN_ROWS // 2, 0), GRID_ROWS - WIN_ROWS)


def _key_row_start(r):
    return 0 if r < NBR_GROUP_ROWS else GRID_ROWS - NBR_KEY_ROWS


def _bias_kernel(rpb_ref, o_ref):
    h = pl.program_id(0)
    qc = lax.broadcasted_iota(jnp.int32, (GRID_W, GRID_W), 0)
    kc = lax.broadcasted_iota(jnp.int32, (GRID_W, GRID_W), 1)
    rel_c = jnp.clip(kc - qc + (WIN_COLS - 1), 0, RPB_COLS - 1)
    col_start = jnp.clip(qc - WIN_COLS // 2, 0, GRID_W - WIN_COLS)
    col_ok = (kc >= col_start) & (kc < col_start + WIN_COLS)
    base = h * (RPB_ROWS * RPB_COLS)
    row_tiles = []
    for rel_r in range(RPB_ROWS):
        tile = jnp.zeros((GRID_W, GRID_W), F32)
        for c in range(RPB_COLS):
            tile = jnp.where(rel_c == c, rpb_ref[base + rel_r * RPB_COLS + c], tile)
        row_tiles.append(jnp.where(col_ok, tile, MASK_VALUE))
    masked = jnp.full((GRID_W, GRID_W), MASK_VALUE, F32)
    for r in range(GRID_ROWS):
        for i in range(NBR_KEY_ROWS):
            key_row = _key_row_start(r) + i
            in_window = _band_start(r) <= key_row < _band_start(r) + WIN_ROWS
            tile = row_tiles[key_row - r + WIN_ROWS - 1] if in_window else masked
            o_ref[0, r, :, i * GRID_W:(i + 1) * GRID_W] = tile


def _window_bias(rpb):
    return pl.pallas_call(
        _bias_kernel,
        grid=(N_HEADS,),
        in_specs=[pl.BlockSpec(memory_space=pltpu.SMEM)],
        out_specs=pl.BlockSpec((1, GRID_ROWS, GRID_W, NBR_KEYS), lambda h: (h, 0, 0, 0)),
        out_shape=jax.ShapeDtypeStruct((N_HEADS, GRID_ROWS, GRID_W, NBR_KEYS), F32),
        compiler_params=_params(),
        name="window_bias",
    )(rpb.reshape(N_HEADS * RPB_ROWS * RPB_COLS))


def _head_lane_mask(half):
    lane = lax.broadcasted_iota(jnp.int32, (1, HEAD_PAIR), 1)
    return (lane >= HEAD_DIM) if half else (lane < HEAD_DIM)


def _scaled_head_queries(q2, sel):
    return jnp.where(sel, q2 * ATT_SCALE, jnp.zeros_like(q2))


def _values_and_ones(v2, sel):
    return jnp.where(sel, v2, jnp.ones_like(v2))


def _normalise(o):
    return o * (1.0 / pltpu.roll(o, HEAD_DIM, axis=1))


def _ctx_attn_kernel(q_ref, k_ref, v_ref, o_ref):
    for p in range(N_PAIRS):
        cols = slice(p * HEAD_PAIR, (p + 1) * HEAD_PAIR)
        q2 = q_ref[:, cols]
        k2 = k_ref[:, cols]
        v2 = v_ref[:, cols]
        out = None
        for half in range(2):
            sel = _head_lane_mask(half)
            s = _dot_nt(_scaled_head_queries(q2, sel), k2)
            e = jnp.exp(s - jnp.max(s, axis=-1, keepdims=True))
            o = _dot(e.astype(BF16), v2) / jnp.sum(e, axis=-1, keepdims=True)
            out = o if out is None else jnp.where(sel, o, out)
        o_ref[:, cols] = out.astype(BF16)


def _ctx_attention(q, k, v):
    spec = pl.BlockSpec((SEQ, D_MODEL), lambda b: (b, 0))
    return pl.pallas_call(
        _ctx_attn_kernel,
        grid=(BATCH,),
        in_specs=[spec, spec, spec],
        out_specs=spec,
        out_shape=jax.ShapeDtypeStruct((P_TOK, D_MODEL), BF16),
        compiler_params=_params(),
        name="ctx_attention",
    )(q, k, v)


NBR_SOFTMAX_ROWS = 32


def _nbr_attn_kernel(q_ref, k_ref, v_ref, ck_ref, cv_ref, bias_ref, o_ref):
    ck2 = ck_ref[0].astype(BF16)
    cv2 = cv_ref[0].astype(BF16)
    blocks = [(g, half) for g in range(GRID_ROWS // NBR_GROUP_ROWS) for half in range(2)]

    def scores(g, half):
        r0 = g * NBR_GROUP_ROWS
        key_lo = _key_row_start(r0) * GRID_W
        qm = _scaled_head_queries(q_ref[r0 * GRID_W:r0 * GRID_W + NBR_Q, :], _head_lane_mask(half))
        return _dot_nt(qm, k_ref[key_lo:key_lo + NBR_KEYS, :]), _dot_nt(qm, ck2)

    def attend(g, half, s_loc, s_ctx):
        r0 = g * NBR_GROUP_ROWS
        key_lo = _key_row_start(r0) * GRID_W
        sel = _head_lane_mask(half)
        e_loc, e_ctx = [], []
        for lo in range(0, NBR_Q, NBR_SOFTMAX_ROWS):
            hi = lo + NBR_SOFTMAX_ROWS
            r, q_lo = r0 + lo // GRID_W, lo % GRID_W
            sl = s_loc[lo:hi] + bias_ref[half, r, q_lo:q_lo + NBR_SOFTMAX_ROWS, :]
            sc = s_ctx[lo:hi]
            m = jnp.maximum(jnp.max(sl, axis=-1, keepdims=True), jnp.max(sc, axis=-1, keepdims=True))
            e_loc.append(jnp.exp(sl - m).astype(BF16))
            e_ctx.append(jnp.exp(sc - m).astype(BF16))
        o = _dot(jnp.concatenate(e_loc, axis=0), _values_and_ones(v_ref[key_lo:key_lo + NBR_KEYS, :], sel))
        o = o + _dot(jnp.concatenate(e_ctx, axis=0), _values_and_ones(cv2, sel))
        return _normalise(o)

    pending = scores(*blocks[0])
    outs = {}
    for i, (g, half) in enumerate(blocks):
        s_loc, s_ctx = pending
        if i + 1 < len(blocks):
            pending = scores(*blocks[i + 1])
        outs[(g, half)] = attend(g, half, s_loc, s_ctx)
    for g in range(GRID_ROWS // NBR_GROUP_ROWS):
        rows = slice(g * NBR_Q, (g + 1) * NBR_Q)
        o_ref[rows, :] = jnp.where(_head_lane_mask(0), outs[(g, 0)], outs[(g, 1)]).astype(BF16)


def _nbr_attention(q, k, v, ck, cv, bias):
    first = P_TOK // DEC_SEQ
    tok = pl.BlockSpec((DEC_SEQ, HEAD_PAIR), lambda p, b: (first + b, p))
    ctx = pl.BlockSpec((1, PAST_LEN, HEAD_PAIR), lambda p, b: (b, 0, p))
    return pl.pallas_call(
        _nbr_attn_kernel,
        grid=(N_PAIRS, DEC_BATCH),
        in_specs=[tok, tok, tok, ctx, ctx,
                  pl.BlockSpec((2, GRID_ROWS, GRID_W, NBR_KEYS), lambda p, b: (p, 0, 0, 0))],
        out_specs=pl.BlockSpec((DEC_SEQ, HEAD_PAIR), lambda p, b: (b, p)),
        out_shape=jax.ShapeDtypeStruct((S_TOK, D_MODEL), BF16),
        compiler_params=_params(2),
        name="nbr_attention",
    )(q, k, v, ck, cv, bias)


def kernel(x_prompt, x_sample, cache_k, cache_v, c, c_ctx, norm_g, ada_w, ada_b, a_w_in, a_v_gain, a_ws, a_bs, a_w_out, b_w_qkv, b_q_gain, b_k_gain, b_rpb, b_w_o, c_w_in, c_conv_w, c_conv_b, c_w_out, ff_w1, ff_w2):
    n_a = a_w_in.shape[0]
    cond = jnp.concatenate(
        [c, c_ctx[None, :], jnp.zeros((COND_ROWS - DEC_BATCH - 1, D_MODEL), F32)], axis=0)
    mods = _adaln(cond, ada_w, ada_b).reshape(DEPTH, COND_ROWS, 6, D_MODEL)
    norm_g = norm_g.reshape(2 * DEPTH, 1, D_MODEL)
    a_v_gain = a_v_gain.reshape(n_a, 1, A_HALF)
    a_bs = a_bs.reshape(n_a, A_GROUPS, CHUNK, 1)
    c_conv_b = c_conv_b.reshape(-1, 1, D_MODEL)
    x = (x_prompt.reshape(P_TOK, D_MODEL), x_sample.reshape(S_TOK, D_MODEL))
    new_k, new_v = [], []
    for i in range(DEPTH):
        kind, j = i % 3, i // 3
        attn = None
        if kind == 0:
            x = _gmlp(x, mods, i, norm_g, j, a_w_in, a_v_gain, a_ws, a_bs, a_w_out)
        elif kind == 1:
            q, k, v, k_new, v_new = _qkv(x, mods, i, norm_g, j, b_w_qkv, b_q_gain, b_k_gain)
            new_k.append(k_new.reshape(BATCH, SEQ, N_HEADS, HEAD_DIM))
            new_v.append(v_new.reshape(BATCH, SEQ, N_HEADS, HEAD_DIM))
            o_prompt = _ctx_attention(q, k, v)
            bias = _window_bias(b_rpb[j])
            o_sample = _nbr_attention(
                q, k, v,
                cache_k[:, j].reshape(DEC_BATCH, PAST_LEN, D_MODEL),
                cache_v[:, j].reshape(DEC_BATCH, PAST_LEN, D_MODEL), bias)
            attn = (o_prompt, o_sample, j, b_w_o)
        else:
            x = _conv(x, mods, i, norm_g, j, c_w_in, c_conv_w, c_conv_b, c_w_out)
        x = _ffn(x, mods, i, norm_g, ff_w1, ff_w2, attn=attn, split_out=(i == DEPTH - 1))
    y_prompt = x[0].reshape(BATCH, SEQ, D_MODEL)
    y_sample = x[1].reshape(DEC_BATCH, DEC_SEQ, D_MODEL)
    return (y_prompt, y_sample, jnp.stack(new_k, axis=1), jnp.stack(new_v, axis=1))
```

```python
import functools

import jax
import jax.numpy as jnp
import numpy as np
from jax import lax
from jax.experimental import pallas as pl
from jax.experimental.pallas import tpu as pltpu

D_MODEL = 1024
BATCH = 16
SEQ = 256
DEPTH = 4
DEC_BATCH = 8
DEC_SEQ = 1024
PAST_LEN = 512
GRID_W = 64
GRID_ROWS = DEC_SEQ // GRID_W
CHUNK = 128
A_HALF = 2 * D_MODEL
A_GROUPS = 8
A_GROUP_W = A_HALF // A_GROUPS
N_HEADS = 16
HEAD_DIM = D_MODEL // N_HEADS
WIN_ROWS = 8
WIN_COLS = 16
RPB_ROWS = 2 * WIN_ROWS - 1
RPB_COLS = 2 * WIN_COLS - 1
D_FF = 4 * D_MODEL
EPS = 1e-6
ATT_SCALE = HEAD_DIM ** -0.5
MASK_VALUE = -1e30

P_TOK = BATCH * SEQ
S_TOK = DEC_BATCH * DEC_SEQ
N_TOK = P_TOK + S_TOK
COND_ROWS = 16
CTX_ROW = DEC_BATCH
LOAD_STEPS = 8
HEAD_PAIR = 2 * HEAD_DIM
N_PAIRS = N_HEADS // 2
MXU_TILE = 256

F32 = jnp.float32
BF16 = jnp.bfloat16
VMEM_LIMIT = 56 * 1024 * 1024


def _params(n_axes=1, vmem=VMEM_LIMIT):
    return pltpu.CompilerParams(dimension_semantics=("arbitrary",) * n_axes, vmem_limit_bytes=vmem)


def _dot(a, b):
    return jnp.dot(a, b, preferred_element_type=F32)


def _dot_nt(a, b):
    return lax.dot_general(a, b, (((1,), (1,)), ((), ())), preferred_element_type=F32)


def _rms(x, g):
    return x * lax.rsqrt(jnp.mean(x * x, axis=-1, keepdims=True) + EPS) * g


def _gelu(x):
    return 0.5 * x * (1.0 + jnp.tanh(0.7978845608028654 * (x + 0.044715 * (x * x * x))))


def _tok_spec(tm, tile0_step=LOAD_STEPS):
    return pl.BlockSpec((tm, D_MODEL), lambda s: (jnp.maximum(s - tile0_step, 0), 0))


def _prompt_tok_spec(tm, tile0_step=LOAD_STEPS):
    last = P_TOK // tm - 1
    return pl.BlockSpec((tm, D_MODEL), lambda s: (jnp.clip(s - tile0_step, 0, last), 0))


def _sample_tok_spec(tm, tile0_step=LOAD_STEPS):
    first, last = P_TOK // tm, S_TOK // tm - 1
    return pl.BlockSpec((tm, D_MODEL), lambda s: (jnp.clip(s - tile0_step - first, 0, last), 0))


def _is_prompt_tile(step, tm, tile0_step=LOAD_STEPS):
    return step - tile0_step < P_TOK // tm


def _mod_spec(layer, tm, tile0_step=LOAD_STEPS):
    n_prompt_tiles = P_TOK // tm
    tiles_per_seq = DEC_SEQ // tm

    def index(s):
        t = jnp.maximum(s - tile0_step, 0)
        row = jnp.where(t < n_prompt_tiles, CTX_ROW, jnp.maximum(t - n_prompt_tiles, 0) // tiles_per_seq)
        return (layer, row, 0, 0)

    return pl.BlockSpec((None, 1, 6, D_MODEL), index)


def _chunk_spec(layer, rows, cols):
    return pl.BlockSpec((None, rows // LOAD_STEPS, cols),
                        lambda s: (layer, jnp.minimum(s, LOAD_STEPS - 1), 0))


def _layer_spec(layer, shape):
    zeros = (0,) * len(shape)
    return pl.BlockSpec((None,) + tuple(shape), lambda s: (layer,) + zeros)


def _const_spec(shape):
    zeros = (0,) * len(shape)
    return pl.BlockSpec(shape, lambda s: zeros)


def _load_chunk(step, w_ref, w_scr):
    rows = w_ref.shape[0]
    off = pl.multiple_of(step * rows, rows)
    w_scr[pl.ds(off, rows), :] = w_ref[...].astype(BF16)


ADA_TN = 3072
ADA_TC = 1024


def _ada_kernel(c_ref, w_ref, b_ref, o_ref):
    c = c_ref[...]
    a = c * (1.0 / (1.0 + jnp.exp(-c)))
    a_hi = a.astype(BF16)
    a_lo = (a - a_hi.astype(F32)).astype(BF16)
    for lo in range(0, ADA_TN, ADA_TC):
        w = w_ref[0, :, lo:lo + ADA_TC]
        w_hi = w.astype(BF16)
        w_lo = (w - w_hi.astype(F32)).astype(BF16)
        o_ref[0, :, lo:lo + ADA_TC] = (
            _dot(a_hi, w_hi) + _dot(a_hi, w_lo) + _dot(a_lo, w_hi) + b_ref[0, :, lo:lo + ADA_TC])


def _adaln(cond, ada_w, ada_b):
    n_out = 6 * D_MODEL
    return pl.pallas_call(
        _ada_kernel,
        grid=(DEPTH, n_out // ADA_TN),
        in_specs=[
            pl.BlockSpec((COND_ROWS, D_MODEL), lambda i, j: (0, 0)),
            pl.BlockSpec((1, D_MODEL, ADA_TN), lambda i, j: (i, 0, j)),
            pl.BlockSpec((1, 1, ADA_TN), lambda i, j: (i, 0, j)),
        ],
        out_specs=pl.BlockSpec((1, COND_ROWS, ADA_TN), lambda i, j: (i, 0, j)),
        out_shape=jax.ShapeDtypeStruct((DEPTH, COND_ROWS, n_out), F32),
        compiler_params=_params(2),
        name="adaln",
    )(cond, ada_w, ada_b.reshape(DEPTH, 1, n_out))


GMLP_TM = 512
GMLP_TC = 1024


def _gmlp_gate_half(h, win_s, v_s):
    ssq = jnp.zeros((GMLP_TM, 1), F32)
    for c in range(0, A_HALF, GMLP_TC):
        cols = slice(c, c + GMLP_TC)
        v_c = _gelu(_dot(h, win_s[:, A_HALF + c:A_HALF + c + GMLP_TC]))
        ssq = ssq + jnp.sum(v_c * v_c, axis=-1, keepdims=True)
        v_s[:, cols] = v_c
    return lax.rsqrt(ssq * (1.0 / A_HALF) + EPS)


def _gmlp_mix(h, inv_rms, vg_ref, ws_ref, bs_ref, win_s, wout_s, v_s):
    acc = jnp.zeros((GMLP_TM, D_MODEL), F32)
    for c in range(0, A_HALF, GMLP_TC):
        cols = slice(c, c + GMLP_TC)
        v_n = (v_s[:, cols] * inv_rms * vg_ref[:, cols]).astype(BF16)
        u_c = _gelu(_dot(h, win_s[:, cols]))
        gated_cols = []
        for g in range(GMLP_TC // A_GROUP_W):
            gcols = slice(g * A_GROUP_W, (g + 1) * A_GROUP_W)
            w_g = ws_ref[c // A_GROUP_W + g].astype(BF16)
            b_g = bs_ref[c // A_GROUP_W + g]
            gated = []
            for n in range(GMLP_TM // CHUNK):
                rows = slice(n * CHUNK, (n + 1) * CHUNK)
                s = _dot(w_g, v_n[rows, gcols]) + b_g
                gated.append((u_c[rows, gcols] * s).astype(BF16))
            gated_cols.append(jnp.concatenate(gated, axis=0))
        acc = acc + _dot(jnp.concatenate(gated_cols, axis=1), wout_s[cols, :])
    return acc


def _gmlp_body(x, mod_ref, g_ref, vg_ref, ws_ref, bs_ref, o_ref, win_s, wout_s, v_s):
    mod = mod_ref[0]
    h = (_rms(x, g_ref[...]) * (1.0 + mod[1:2]) + mod[0:1]).astype(BF16)
    inv_rms = _gmlp_gate_half(h, win_s, v_s)
    o_ref[...] = x + mod[2:3] * _gmlp_mix(h, inv_rms, vg_ref, ws_ref, bs_ref, win_s, wout_s, v_s)


def _gmlp_kernel(x_ref, mod_ref, g_ref, win_ref, vg_ref, ws_ref, bs_ref, wout_ref, o_ref,
                 win_s, wout_s, v_s):
    step = pl.program_id(0)

    @pl.when(step < LOAD_STEPS)
    def _():
        _load_chunk(step, win_ref, win_s)
        _load_chunk(step, wout_ref, wout_s)

    @pl.when(step >= LOAD_STEPS)
    def _():
        _gmlp_body(x_ref[...], mod_ref, g_ref, vg_ref, ws_ref, bs_ref, o_ref, win_s, wout_s, v_s)


def _gmlp_split_in_kernel(xp_ref, xs_ref, mod_ref, g_ref, win_ref, vg_ref, ws_ref, bs_ref, wout_ref,
                          o_ref, win_s, wout_s, v_s):
    step = pl.program_id(0)

    @pl.when(step < LOAD_STEPS)
    def _():
        _load_chunk(step, win_ref, win_s)
        _load_chunk(step, wout_ref, wout_s)

    @pl.when(step >= LOAD_STEPS)
    def _():
        x = jnp.where(_is_prompt_tile(step, GMLP_TM), xp_ref[...], xs_ref[...])
        _gmlp_body(x, mod_ref, g_ref, vg_ref, ws_ref, bs_ref, o_ref, win_s, wout_s, v_s)


def _gmlp(xs, mods, layer, norm_g, j, w_in, v_gain, ws, bs, w_out):
    tm = GMLP_TM
    split_in = isinstance(xs, tuple)
    x_specs = [_prompt_tok_spec(tm), _sample_tok_spec(tm)] if split_in else [_tok_spec(tm)]
    x_args = list(xs) if split_in else [xs]
    return pl.pallas_call(
        _gmlp_split_in_kernel if split_in else _gmlp_kernel,
        grid=(LOAD_STEPS + N_TOK // tm,),
        in_specs=x_specs + [
            _mod_spec(layer, tm),
            _layer_spec(2 * layer, (1, D_MODEL)),
            _chunk_spec(j, D_MODEL, 2 * A_HALF),
            _layer_spec(j, (1, A_HALF)),
            _layer_spec(j, (A_GROUPS, CHUNK, CHUNK)),
            _layer_spec(j, (A_GROUPS, CHUNK, 1)),
            _chunk_spec(j, A_HALF, D_MODEL),
        ],
        out_specs=_tok_spec(tm),
        out_shape=jax.ShapeDtypeStruct((N_TOK, D_MODEL), F32),
        scratch_shapes=[
            pltpu.VMEM((D_MODEL, 2 * A_HALF), BF16),
            pltpu.VMEM((A_HALF, D_MODEL), BF16),
            pltpu.VMEM((tm, A_HALF), F32),
        ],
        compiler_params=_params(),
        name="gmlp",
    )(*x_args, mods, norm_g, w_in, v_gain, ws, bs, w_out)


FFN_TM = 512
FFN_TC = 1024
FFN_CW = D_FF // LOAD_STEPS
FFN_TILE0_STEP = LOAD_STEPS - 1


def _ffn_hidden_in(x, mod, g_ref):
    return (_rms(x, g_ref[...]) * (1.0 + mod[4:5]) + mod[3:4]).astype(BF16)


def _ffn_body(x, mod_ref, g_ref, w1_s, w2_s):
    mod = mod_ref[0]
    h = _ffn_hidden_in(x, mod, g_ref)
    acc = jnp.zeros((FFN_TM, D_MODEL), F32)
    for c in range(D_FF // FFN_TC):
        cols = slice(c * FFN_TC, (c + 1) * FFN_TC)
        hid = jnp.square(jnp.maximum(_dot(h, w1_s[:, cols]), 0.0)).astype(BF16)
        acc = acc + _dot(hid, w2_s[cols, :])
    return x + mod[5:6] * acc


def _ffn_kernel(*refs, proj, split_out):
    refs = list(refs)
    x_ref, mod_ref = refs[:2]
    del refs[:2]
    if proj:
        ap_ref, as_ref, wo_ref = refs[:3]
        del refs[:3]
    g_ref, w1_ref, w2_ref = refs[:3]
    del refs[:3]
    n_out = 2 if split_out else 1
    out_refs = refs[:n_out]
    del refs[:n_out]
    if proj:
        wo_s, x0_s = refs[:2]
        del refs[:2]
    w1_s, w2_s, h0_s, acc0_s = refs
    step = pl.program_id(0)

    def tile_input():
        x = x_ref[...]
        if proj:
            is_prompt = _is_prompt_tile(step, FFN_TM, FFN_TILE0_STEP)
            a = jnp.where(is_prompt, ap_ref[...], as_ref[...])
            x = x + mod_ref[0][2:3] * _dot(a, wo_s[...])
        return x

    for k in range(LOAD_STEPS):
        @pl.when(step == k)
        def _(k=k):
            units = slice(k * FFN_CW, (k + 1) * FFN_CW)
            w1_k = w1_ref[...].astype(BF16)
            w2_k = w2_ref[...].astype(BF16)
            w1_s[:, units] = w1_k
            w2_s[units, :] = w2_k
            mod = mod_ref[0]
            if k == 0:
                if proj:
                    wo_s[...] = wo_ref[...].astype(BF16)
                x0 = tile_input()
                if proj:
                    x0_s[...] = x0
                h0_s[...] = _ffn_hidden_in(x0, mod, g_ref)
            hid = jnp.square(jnp.maximum(_dot(h0_s[...], w1_k), 0.0)).astype(BF16)
            part = _dot(hid, w2_k)
            if k == 0:
                acc0_s[...] = part
            elif k < LOAD_STEPS - 1:
                acc0_s[...] = acc0_s[...] + part
            else:
                x0 = x0_s[...] if proj else x_ref[...]
                out_refs[0][...] = x0 + mod[5:6] * (acc0_s[...] + part)

    @pl.when(step >= LOAD_STEPS)
    def _():
        y = _ffn_body(tile_input(), mod_ref, g_ref, w1_s, w2_s)
        if not split_out:
            out_refs[0][...] = y
        else:
            is_prompt = _is_prompt_tile(step, FFN_TM, FFN_TILE0_STEP)

            @pl.when(is_prompt)
            def _():
                out_refs[0][...] = y

            @pl.when(jnp.logical_not(is_prompt))
            def _():
                out_refs[1][...] = y


def _ffn(x, mods, layer, norm_g, w1, w2, attn=None, split_out=False):
    tm, t0 = FFN_TM, FFN_TILE0_STEP
    proj = attn is not None
    if split_out:
        out_specs = [_prompt_tok_spec(tm, t0), _sample_tok_spec(tm, t0)]
        out_shape = [jax.ShapeDtypeStruct((P_TOK, D_MODEL), F32), jax.ShapeDtypeStruct((S_TOK, D_MODEL), F32)]
    else:
        out_specs = _tok_spec(tm, t0)
        out_shape = jax.ShapeDtypeStruct((N_TOK, D_MODEL), F32)
    in_specs = [_tok_spec(tm, t0), _mod_spec(layer, tm, t0)]
    args = [x, mods]
    scratch = []
    if proj:
        attn_prompt, attn_sample, j, w_o = attn
        in_specs += [_prompt_tok_spec(tm, t0), _sample_tok_spec(tm, t0), _layer_spec(j, (D_MODEL, D_MODEL))]
        args += [attn_prompt, attn_sample, w_o]
        scratch += [pltpu.VMEM((D_MODEL, D_MODEL), BF16), pltpu.VMEM((tm, D_MODEL), F32)]
    last = LOAD_STEPS - 1
    in_specs += [
        _layer_spec(2 * layer + 1, (1, D_MODEL)),
        pl.BlockSpec((None, D_MODEL, FFN_CW), lambda s: (layer, 0, jnp.minimum(s, last))),
        pl.BlockSpec((None, FFN_CW, D_MODEL), lambda s: (layer, jnp.minimum(s, last), 0)),
    ]
    args += [norm_g, w1, w2]
    scratch += [
        pltpu.VMEM((D_MODEL, D_FF), BF16),
        pltpu.VMEM((D_FF, D_MODEL), BF16),
        pltpu.VMEM((tm, D_MODEL), BF16),
        pltpu.VMEM((tm, D_MODEL), F32),
    ]
    return pl.pallas_call(
        functools.partial(_ffn_kernel, proj=proj, split_out=split_out),
        grid=(t0 + N_TOK // tm,),
        in_specs=in_specs,
        out_specs=out_specs,
        out_shape=out_shape,
        scratch_shapes=scratch,
        compiler_params=_params(),
        name="attn_proj_ffn" if proj else "ffn",
    )(*args)


CONV_TM = 1024
CONV_TC = 1024


def _conv_kernel(x_ref, mod_ref, g_ref, win_ref, cw_ref, cb_ref, wout_ref, o_ref, win_s, wout_s):
    step = pl.program_id(0)

    @pl.when(step < LOAD_STEPS)
    def _():
        _load_chunk(step, win_ref, win_s)
        _load_chunk(step, wout_ref, wout_s)

    @pl.when(step >= LOAD_STEPS)
    def _():
        seq_len = jnp.where(_is_prompt_tile(step, CONV_TM), SEQ, DEC_SEQ)
        pos = lax.broadcasted_iota(jnp.int32, (CONV_TM, 1), 0) & (seq_len - 1)
        has_prev = pos != 0
        has_next = pos != seq_len - 1
        x = x_ref[...]
        mod = mod_ref[0]
        h = (_rms(x, g_ref[...]) * (1.0 + mod[1:2]) + mod[0:1]).astype(BF16)
        cw = cw_ref[...]
        cb = cb_ref[...]
        acc = jnp.zeros((CONV_TM, D_MODEL), F32)
        for c in range(D_MODEL // CONV_TC):
            lo = c * CONV_TC
            cols = slice(lo, lo + CONV_TC)
            bg = _dot(h, win_s[:, lo:lo + CONV_TC])
            cg = _dot(h, win_s[:, D_MODEL + lo:D_MODEL + lo + CONV_TC])
            xt = _dot(h, win_s[:, 2 * D_MODEL + lo:2 * D_MODEL + lo + CONV_TC])
            z = cg * xt
            z_prev = jnp.where(has_prev, pltpu.roll(z, 1, axis=0), 0.0)
            z_next = jnp.where(has_next, pltpu.roll(z, CONV_TM - 1, axis=0), 0.0)
            zc = cw[0:1, cols] * z_prev + cw[1:2, cols] * z + cw[2:3, cols] * z_next + cb[:, cols]
            acc = acc + _dot((bg * zc).astype(BF16), wout_s[cols, :])
        o_ref[...] = x + mod[2:3] * acc


def _conv(x, mods, layer, norm_g, j, w_in, conv_w, conv_b, w_out):
    tm = CONV_TM
    return pl.pallas_call(
        _conv_kernel,
        grid=(LOAD_STEPS + N_TOK // tm,),
        in_specs=[
            _tok_spec(tm),
            _mod_spec(layer, tm),
            _layer_spec(2 * layer, (1, D_MODEL)),
            _chunk_spec(j, D_MODEL, 3 * D_MODEL),
            _layer_spec(j, (3, D_MODEL)),
            _layer_spec(j, (1, D_MODEL)),
            _chunk_spec(j, D_MODEL, D_MODEL),
        ],
        out_specs=_tok_spec(tm),
        out_shape=jax.ShapeDtypeStruct((N_TOK, D_MODEL), F32),
        scratch_shapes=[
            pltpu.VMEM((D_MODEL, 3 * D_MODEL), BF16),
            pltpu.VMEM((D_MODEL, D_MODEL), BF16),
        ],
        compiler_params=_params(),
        name="sconv",
    )(x, mods, norm_g, w_in, conv_w, conv_b, w_out)


QKV_TM = 512


def _qkv_kernel(x_ref, mod_ref, g_ref, w_ref, gq_ref, gk_ref, hm_ref,
                q_ref, k_ref, v_ref, kc_ref, vc_ref, w_s):
    step = pl.program_id(0)

    @pl.when(step < LOAD_STEPS)
    def _():
        _load_chunk(step, w_ref, w_s)

    @pl.when(step >= LOAD_STEPS)
    def _():
        x = x_ref[...]
        mod = mod_ref[0]
        h = (_rms(x, g_ref[...]) * (1.0 + mod[1:2]) + mod[0:1]).astype(BF16)
        head_mean = hm_ref[...]

        def head_norm(y, gain):
            sq = (y * y).astype(BF16)
            ms = jnp.concatenate(
                [_dot(sq[:, c:c + MXU_TILE], head_mean) for c in range(0, D_MODEL, MXU_TILE)], axis=1)
            return y * lax.rsqrt(ms + EPS) * gain

        q = head_norm(_dot(h, w_s[:, :D_MODEL]), gq_ref[...])
        k = head_norm(_dot(h, w_s[:, D_MODEL:2 * D_MODEL]), gk_ref[...])
        v = _dot(h, w_s[:, 2 * D_MODEL:])
        q_ref[...] = q.astype(BF16)
        k_ref[...] = k.astype(BF16)
        v_ref[...] = v.astype(BF16)

        @pl.when(_is_prompt_tile(step, QKV_TM))
        def _():
            kc_ref[...] = pltpu.einshape("m(hd)->mhd", k, h=N_HEADS)
            vc_ref[...] = pltpu.einshape("m(hd)->mhd", v, h=N_HEADS)


def _qkv(x, mods, layer, norm_g, j, w_qkv, q_gain, k_gain):
    tm = QKV_TM
    head_mean = jnp.asarray(
        np.kron(np.eye(MXU_TILE // HEAD_DIM), np.full((HEAD_DIM, HEAD_DIM), 1.0 / HEAD_DIM)), BF16)
    last_prompt_tile = P_TOK // tm - 1
    cache_spec = pl.BlockSpec(
        (tm, N_HEADS, HEAD_DIM), lambda s: (jnp.clip(s - LOAD_STEPS, 0, last_prompt_tile), 0, 0))
    return pl.pallas_call(
        _qkv_kernel,
        grid=(LOAD_STEPS + N_TOK // tm,),
        in_specs=[
            _tok_spec(tm),
            _mod_spec(layer, tm),
            _layer_spec(2 * layer, (1, D_MODEL)),
            _chunk_spec(j, D_MODEL, 3 * D_MODEL),
            _const_spec((1, D_MODEL)),
            _const_spec((1, D_MODEL)),
            _const_spec((MXU_TILE, MXU_TILE)),
        ],
        out_specs=[_tok_spec(tm), _tok_spec(tm), _tok_spec(tm), cache_spec, cache_spec],
        out_shape=[
            jax.ShapeDtypeStruct((N_TOK, D_MODEL), BF16),
            jax.ShapeDtypeStruct((N_TOK, D_MODEL), BF16),
            jax.ShapeDtypeStruct((N_TOK, D_MODEL), BF16),
            jax.ShapeDtypeStruct((P_TOK, N_HEADS, HEAD_DIM), F32),
            jax.ShapeDtypeStruct((P_TOK, N_HEADS, HEAD_DIM), F32),
        ],
        scratch_shapes=[pltpu.VMEM((D_MODEL, 3 * D_MODEL), BF16)],
        compiler_params=_params(),
        name="qkv",
    )(x, mods, norm_g, w_qkv,
      jnp.tile(q_gain[j], N_HEADS).reshape(1, D_MODEL), jnp.tile(k_gain[j], N_HEADS).reshape(1, D_MODEL),
      head_mean)


NBR_GROUP_ROWS = GRID_ROWS // 2
NBR_KEY_ROWS = NBR_GROUP_ROWS + WIN_ROWS // 2
NBR_Q = NBR_GROUP_ROWS * GRID_W
NBR_KEYS = NBR_KEY_ROWS * GRID_W
MASKED_TILE = RPB_ROWS
N_BIAS_TILES = RPB_ROWS + 1


def _band_start(r):
    return min(max(r - WIN_ROWS // 2, 0), GRID_ROWS - WIN_ROWS)


def _key_row_start(r):
    return 0 if r < NBR_GROUP_ROWS else GRID_ROWS - NBR_KEY_ROWS


def _bias_tile_index(r, i):
    key_row = _key_row_start(r) + i
    in_window = _band_start(r) <= key_row < _band_start(r) + WIN_ROWS
    return key_row - r + WIN_ROWS - 1 if in_window else MASKED_TILE


def _build_bias_tiles(rpb_ref, tiles_s, pair):
    qc = lax.broadcasted_iota(jnp.int32, (GRID_W, HEAD_PAIR), 0)
    kc = lax.broadcasted_iota(jnp.int32, (GRID_W, HEAD_PAIR), 1) & (GRID_W - 1)
    rel_c = jnp.clip(kc - qc + (WIN_COLS - 1), 0, RPB_COLS - 1)
    col_start = jnp.clip(qc - WIN_COLS // 2, 0, GRID_W - WIN_COLS)
    col_ok = (kc >= col_start) & (kc < col_start + WIN_COLS)
    for half in range(2):
        base = (2 * pair + half) * (RPB_ROWS * RPB_COLS)
        for rel_r in range(RPB_ROWS):
            tile = jnp.zeros((GRID_W, HEAD_PAIR), F32)
            for c in range(RPB_COLS):
                tile = jnp.where(rel_c == c, rpb_ref[base + rel_r * RPB_COLS + c], tile)
            tiles_s[half, rel_r] = jnp.where(col_ok, tile, MASK_VALUE)
        tiles_s[half, MASKED_TILE] = jnp.full((GRID_W, HEAD_PAIR), MASK_VALUE, F32)


def _head_lane_mask(half):
    lane = lax.broadcasted_iota(jnp.int32, (1, HEAD_PAIR), 1)
    return (lane >= HEAD_DIM) if half else (lane < HEAD_DIM)


def _scaled_head_queries(q2, sel):
    return jnp.where(sel, q2 * ATT_SCALE, jnp.zeros_like(q2))


def _values_and_ones(v2, sel):
    return jnp.where(sel, v2, jnp.ones_like(v2))


def _normalise(o):
    return o * (1.0 / pltpu.roll(o, HEAD_DIM, axis=1))


CTX_SEQS = 4


def _ctx_attn_kernel(q_ref, k_ref, v_ref, o_ref):
    for b in range(CTX_SEQS):
        rows = slice(b * SEQ, (b + 1) * SEQ)
        for p in range(N_PAIRS):
            cols = slice(p * HEAD_PAIR, (p + 1) * HEAD_PAIR)
            q2 = q_ref[rows, cols]
            k2 = k_ref[rows, cols]
            v2 = v_ref[rows, cols]
            out = None
            for half in range(2):
                sel = _head_lane_mask(half)
                s = _dot_nt(_scaled_head_queries(q2, sel), k2)
                e = jnp.exp(s - jnp.max(s, axis=-1, keepdims=True))
                o = _dot(e.astype(BF16), v2) / jnp.sum(e, axis=-1, keepdims=True)
                out = o if out is None else jnp.where(sel, o, out)
            o_ref[rows, cols] = out.astype(BF16)


def _ctx_attention(q, k, v):
    spec = pl.BlockSpec((CTX_SEQS * SEQ, D_MODEL), lambda b: (b, 0))
    return pl.pallas_call(
        _ctx_attn_kernel,
        grid=(BATCH // CTX_SEQS,),
        in_specs=[spec, spec, spec],
        out_specs=spec,
        out_shape=jax.ShapeDtypeStruct((P_TOK, D_MODEL), BF16),
        compiler_params=_params(),
        name="ctx_attention",
    )(q, k, v)


NBR_SOFTMAX_ROWS = 32


def _nbr_attn_kernel(rpb_ref, q_ref, k_ref, v_ref, ck_ref, cv_ref, o_ref, tiles_s):
    @pl.when(pl.program_id(1) == 0)
    def _():
        _build_bias_tiles(rpb_ref, tiles_s, pl.program_id(0))

    first_key_row = _head_lane_mask(0)
    ck2 = ck_ref[0].astype(BF16)
    cv2 = cv_ref[0].astype(BF16)
    blocks = [(g, half) for g in range(GRID_ROWS // NBR_GROUP_ROWS) for half in range(2)]

    def bias_rows(half, r, q_lo):
        rows = slice(q_lo, q_lo + NBR_SOFTMAX_ROWS)
        parts = []
        for i in range(0, NBR_KEY_ROWS, 2):
            left, right = _bias_tile_index(r, i), _bias_tile_index(r, i + 1)
            part = tiles_s[half, left, rows, :]
            if right != left:
                part = jnp.where(first_key_row, part, tiles_s[half, right, rows, :])
            parts.append(part)
        return jnp.concatenate(parts, axis=1)

    def scores(g, half):
        r0 = g * NBR_GROUP_ROWS
        key_lo = _key_row_start(r0) * GRID_W
        qm = _scaled_head_queries(q_ref[r0 * GRID_W:r0 * GRID_W + NBR_Q, :], _head_lane_mask(half))
        return _dot_nt(qm, k_ref[key_lo:key_lo + NBR_KEYS, :]), _dot_nt(qm, ck2)

    def attend(g, half, s_loc, s_ctx):
        r0 = g * NBR_GROUP_ROWS
        key_lo = _key_row_start(r0) * GRID_W
        sel = _head_lane_mask(half)
        e_loc, e_ctx = [], []
        for lo in range(0, NBR_Q, NBR_SOFTMAX_ROWS):
            hi = lo + NBR_SOFTMAX_ROWS
            r, q_lo = r0 + lo // GRID_W, lo % GRID_W
            sl = s_loc[lo:hi] + bias_rows(half, r, q_lo)
            sc = s_ctx[lo:hi]
            m = jnp.maximum(jnp.max(sl, axis=-1, keepdims=True), jnp.max(sc, axis=-1, keepdims=True))
            e_loc.append(jnp.exp(sl - m).astype(BF16))
            e_ctx.append(jnp.exp(sc - m).astype(BF16))
        o = _dot(jnp.concatenate(e_loc, axis=0), _values_and_ones(v_ref[key_lo:key_lo + NBR_KEYS, :], sel))
        o = o + _dot(jnp.concatenate(e_ctx, axis=0), _values_and_ones(cv2, sel))
        return _normalise(o)

    pending = scores(*blocks[0])
    outs = {}
    for i, (g, half) in enumerate(blocks):
        s_loc, s_ctx = pending
        if i + 1 < len(blocks):
            pending = scores(*blocks[i + 1])
        outs[(g, half)] = attend(g, half, s_loc, s_ctx)
    for g in range(GRID_ROWS // NBR_GROUP_ROWS):
        rows = slice(g * NBR_Q, (g + 1) * NBR_Q)
        o_ref[rows, :] = jnp.where(_head_lane_mask(0), outs[(g, 0)], outs[(g, 1)]).astype(BF16)


def _nbr_attention(rpb, q, k, v, ck, cv):
    first = P_TOK // DEC_SEQ
    tok = pl.BlockSpec((DEC_SEQ, HEAD_PAIR), lambda p, b: (first + b, p))
    ctx = pl.BlockSpec((1, PAST_LEN, HEAD_PAIR), lambda p, b: (b, 0, p))
    return pl.pallas_call(
        _nbr_attn_kernel,
        grid=(N_PAIRS, DEC_BATCH),
        in_specs=[pl.BlockSpec(memory_space=pltpu.SMEM), tok, tok, tok, ctx, ctx],
        out_specs=pl.BlockSpec((DEC_SEQ, HEAD_PAIR), lambda p, b: (b, p)),
        out_shape=jax.ShapeDtypeStruct((S_TOK, D_MODEL), BF16),
        scratch_shapes=[pltpu.VMEM((2, N_BIAS_TILES, GRID_W, HEAD_PAIR), F32)],
        compiler_params=_params(2),
        name="nbr_attention",
    )(rpb.reshape(N_HEADS * RPB_ROWS * RPB_COLS), q, k, v, ck, cv)


def kernel(x_prompt, x_sample, cache_k, cache_v, c, c_ctx, norm_g, ada_w, ada_b, a_w_in, a_v_gain, a_ws, a_bs, a_w_out, b_w_qkv, b_q_gain, b_k_gain, b_rpb, b_w_o, c_w_in, c_conv_w, c_conv_b, c_w_out, ff_w1, ff_w2):
    n_a = a_w_in.shape[0]
    cond = jnp.concatenate(
        [c, c_ctx[None, :], jnp.zeros((COND_ROWS - DEC_BATCH - 1, D_MODEL), F32)], axis=0)
    mods = _adaln(cond, ada_w, ada_b).reshape(DEPTH, COND_ROWS, 6, D_MODEL)
    norm_g = norm_g.reshape(2 * DEPTH, 1, D_MODEL)
    a_v_gain = a_v_gain.reshape(n_a, 1, A_HALF)
    a_bs = a_bs.reshape(n_a, A_GROUPS, CHUNK, 1)
    c_conv_b = c_conv_b.reshape(-1, 1, D_MODEL)
    x = (x_prompt.reshape(P_TOK, D_MODEL), x_sample.reshape(S_TOK, D_MODEL))
    new_k, new_v = [], []
    for i in range(DEPTH):
        kind, j = i % 3, i // 3
        attn = None
        if kind == 0:
            x = _gmlp(x, mods, i, norm_g, j, a_w_in, a_v_gain, a_ws, a_bs, a_w_out)
        elif kind == 1:
            q, k, v, k_new, v_new = _qkv(x, mods, i, norm_g, j, b_w_qkv, b_q_gain, b_k_gain)
            new_k.append(k_new.reshape(BATCH, SEQ, N_HEADS, HEAD_DIM))
            new_v.append(v_new.reshape(BATCH, SEQ, N_HEADS, HEAD_DIM))
            o_prompt = _ctx_attention(q, k, v)
            o_sample = _nbr_attention(
                b_rpb[j], q, k, v,
                cache_k[:, j].reshape(DEC_BATCH, PAST_LEN, D_MODEL),
                cache_v[:, j].reshape(DEC_BATCH, PAST_LEN, D_MODEL))
            attn = (o_prompt, o_sample, j, b_w_o)
        else:
            x = _conv(x, mods, i, norm_g, j, c_w_in, c_conv_w, c_conv_b, c_w_out)
        x = _ffn(x, mods, i, norm_g, ff_w1, ff_w2, attn=attn, split_out=(i == DEPTH - 1))
    y_prompt = x[0].reshape(BATCH, SEQ, D_MODEL)
    y_sample = x[1].reshape(DEC_BATCH, DEC_SEQ, D_MODEL)
    return (y_prompt, y_sample, jnp.stack(new_k, axis=1), jnp.stack(new_v, axis=1))
```

```python
import functools

import jax
import jax.numpy as jnp
import numpy as np
from jax import lax
from jax.experimental import pallas as pl
from jax.experimental.pallas import tpu as pltpu

D_MODEL = 1024
BATCH = 16
SEQ = 256
DEPTH = 4
DEC_BATCH = 8
DEC_SEQ = 1024
PAST_LEN = 512
GRID_W = 64
GRID_ROWS = DEC_SEQ // GRID_W
CHUNK = 128
A_HALF = 2 * D_MODEL
A_GROUPS = 8
A_GROUP_W = A_HALF // A_GROUPS
N_HEADS = 16
HEAD_DIM = D_MODEL // N_HEADS
WIN_ROWS = 8
WIN_COLS = 16
RPB_ROWS = 2 * WIN_ROWS - 1
RPB_COLS = 2 * WIN_COLS - 1
D_FF = 4 * D_MODEL
EPS = 1e-6
ATT_SCALE = HEAD_DIM ** -0.5
MASK_VALUE = -1e30

P_TOK = BATCH * SEQ
S_TOK = DEC_BATCH * DEC_SEQ
N_TOK = P_TOK + S_TOK
COND_ROWS = 16
CTX_ROW = DEC_BATCH
LOAD_STEPS = 8
HEAD_PAIR = 2 * HEAD_DIM
N_PAIRS = N_HEADS // 2
MXU_TILE = 256

F32 = jnp.float32
BF16 = jnp.bfloat16
VMEM_LIMIT = 56 * 1024 * 1024


def _params(n_axes=1, vmem=VMEM_LIMIT):
    return pltpu.CompilerParams(dimension_semantics=("arbitrary",) * n_axes, vmem_limit_bytes=vmem)


def _dot(a, b):
    return jnp.dot(a, b, preferred_element_type=F32)


def _dot_nt(a, b):
    return lax.dot_general(a, b, (((1,), (1,)), ((), ())), preferred_element_type=F32)


def _rms(x, g):
    return x * lax.rsqrt(jnp.mean(x * x, axis=-1, keepdims=True) + EPS) * g


def _gelu(x):
    return 0.5 * x * (1.0 + jnp.tanh(0.7978845608028654 * (x + 0.044715 * (x * x * x))))


def _tok_spec(tm, tile0_step=LOAD_STEPS):
    return pl.BlockSpec((tm, D_MODEL), lambda s: (jnp.maximum(s - tile0_step, 0), 0))


def _prompt_tok_spec(tm, tile0_step=LOAD_STEPS):
    last = P_TOK // tm - 1
    return pl.BlockSpec((tm, D_MODEL), lambda s: (jnp.clip(s - tile0_step, 0, last), 0))


def _sample_tok_spec(tm, tile0_step=LOAD_STEPS):
    first, last = P_TOK // tm, S_TOK // tm - 1
    return pl.BlockSpec((tm, D_MODEL), lambda s: (jnp.clip(s - tile0_step - first, 0, last), 0))


def _is_prompt_tile(step, tm, tile0_step=LOAD_STEPS):
    return step - tile0_step < P_TOK // tm


def _mod_spec(layer, tm, tile0_step=LOAD_STEPS):
    n_prompt_tiles = P_TOK // tm
    tiles_per_seq = DEC_SEQ // tm

    def index(s):
        t = jnp.maximum(s - tile0_step, 0)
        row = jnp.where(t < n_prompt_tiles, CTX_ROW, jnp.maximum(t - n_prompt_tiles, 0) // tiles_per_seq)
        return (layer, row, 0, 0)

    return pl.BlockSpec((None, 1, 6, D_MODEL), index)


def _chunk_spec(layer, rows, cols):
    return pl.BlockSpec((None, rows // LOAD_STEPS, cols),
                        lambda s: (layer, jnp.minimum(s, LOAD_STEPS - 1), 0))


def _layer_spec(layer, shape):
    zeros = (0,) * len(shape)
    return pl.BlockSpec((None,) + tuple(shape), lambda s: (layer,) + zeros)


def _const_spec(shape):
    zeros = (0,) * len(shape)
    return pl.BlockSpec(shape, lambda s: zeros)


def _load_chunk(step, w_ref, w_scr):
    rows = w_ref.shape[0]
    off = pl.multiple_of(step * rows, rows)
    w_scr[pl.ds(off, rows), :] = w_ref[...].astype(BF16)


ADA_TN = 3072
ADA_TC = 1024


def _ada_kernel(c_ref, w_ref, b_ref, o_ref):
    c = c_ref[...]
    a = c * (1.0 / (1.0 + jnp.exp(-c)))
    a_hi = a.astype(BF16)
    a_lo = (a - a_hi.astype(F32)).astype(BF16)
    for lo in range(0, ADA_TN, ADA_TC):
        w = w_ref[0, :, lo:lo + ADA_TC]
        w_hi = w.astype(BF16)
        w_lo = (w - w_hi.astype(F32)).astype(BF16)
        o_ref[0, :, lo:lo + ADA_TC] = (
            _dot(a_hi, w_hi) + _dot(a_hi, w_lo) + _dot(a_lo, w_hi) + b_ref[0, :, lo:lo + ADA_TC])


def _adaln(cond, ada_w, ada_b):
    n_out = 6 * D_MODEL
    return pl.pallas_call(
        _ada_kernel,
        grid=(DEPTH, n_out // ADA_TN),
        in_specs=[
            pl.BlockSpec((COND_ROWS, D_MODEL), lambda i, j: (0, 0)),
            pl.BlockSpec((1, D_MODEL, ADA_TN), lambda i, j: (i, 0, j)),
            pl.BlockSpec((1, 1, ADA_TN), lambda i, j: (i, 0, j)),
        ],
        out_specs=pl.BlockSpec((1, COND_ROWS, ADA_TN), lambda i, j: (i, 0, j)),
        out_shape=jax.ShapeDtypeStruct((DEPTH, COND_ROWS, n_out), F32),
        compiler_params=_params(2),
        name="adaln",
    )(cond, ada_w, ada_b.reshape(DEPTH, 1, n_out))


GMLP_TM = 512
GMLP_TC = 1024


def _gmlp_gate_half(h, win_s, v_s):
    ssq = jnp.zeros((GMLP_TM, 1), F32)
    for c in range(0, A_HALF, GMLP_TC):
        cols = slice(c, c + GMLP_TC)
        v_c = _gelu(_dot(h, win_s[:, A_HALF + c:A_HALF + c + GMLP_TC]))
        ssq = ssq + jnp.sum(v_c * v_c, axis=-1, keepdims=True)
        v_s[:, cols] = v_c
    return lax.rsqrt(ssq * (1.0 / A_HALF) + EPS)


def _gmlp_mix(h, inv_rms, vg_ref, ws_ref, bs_ref, win_s, wout_s, v_s):
    acc = jnp.zeros((GMLP_TM, D_MODEL), F32)
    for c in range(0, A_HALF, GMLP_TC):
        cols = slice(c, c + GMLP_TC)
        v_n = (v_s[:, cols] * inv_rms * vg_ref[:, cols]).astype(BF16)
        u_c = _gelu(_dot(h, win_s[:, cols]))
        gated_cols = []
        for g in range(GMLP_TC // A_GROUP_W):
            gcols = slice(g * A_GROUP_W, (g + 1) * A_GROUP_W)
            w_g = ws_ref[c // A_GROUP_W + g].astype(BF16)
            b_g = bs_ref[c // A_GROUP_W + g]
            gated = []
            for n in range(GMLP_TM // CHUNK):
                rows = slice(n * CHUNK, (n + 1) * CHUNK)
                s = _dot(w_g, v_n[rows, gcols]) + b_g
                gated.append((u_c[rows, gcols] * s).astype(BF16))
            gated_cols.append(jnp.concatenate(gated, axis=0))
        acc = acc + _dot(jnp.concatenate(gated_cols, axis=1), wout_s[cols, :])
    return acc


def _gmlp_body(x, mod_ref, g_ref, vg_ref, ws_ref, bs_ref, o_ref, win_s, wout_s, v_s):
    mod = mod_ref[0]
    h = (_rms(x, g_ref[...]) * (1.0 + mod[1:2]) + mod[0:1]).astype(BF16)
    inv_rms = _gmlp_gate_half(h, win_s, v_s)
    o_ref[...] = x + mod[2:3] * _gmlp_mix(h, inv_rms, vg_ref, ws_ref, bs_ref, win_s, wout_s, v_s)


def _gmlp_kernel(x_ref, mod_ref, g_ref, win_ref, vg_ref, ws_ref, bs_ref, wout_ref, o_ref,
                 win_s, wout_s, v_s):
    step = pl.program_id(0)

    @pl.when(step < LOAD_STEPS)
    def _():
        _load_chunk(step, win_ref, win_s)
        _load_chunk(step, wout_ref, wout_s)

    @pl.when(step >= LOAD_STEPS)
    def _():
        _gmlp_body(x_ref[...], mod_ref, g_ref, vg_ref, ws_ref, bs_ref, o_ref, win_s, wout_s, v_s)


def _gmlp_split_in_kernel(xp_ref, xs_ref, mod_ref, g_ref, win_ref, vg_ref, ws_ref, bs_ref, wout_ref,
                          o_ref, win_s, wout_s, v_s):
    step = pl.program_id(0)

    @pl.when(step < LOAD_STEPS)
    def _():
        _load_chunk(step, win_ref, win_s)
        _load_chunk(step, wout_ref, wout_s)

    @pl.when(step >= LOAD_STEPS)
    def _():
        x = jnp.where(_is_prompt_tile(step, GMLP_TM), xp_ref[...], xs_ref[...])
        _gmlp_body(x, mod_ref, g_ref, vg_ref, ws_ref, bs_ref, o_ref, win_s, wout_s, v_s)


def _gmlp(xs, mods, layer, norm_g, j, w_in, v_gain, ws, bs, w_out):
    tm = GMLP_TM
    split_in = isinstance(xs, tuple)
    x_specs = [_prompt_tok_spec(tm), _sample_tok_spec(tm)] if split_in else [_tok_spec(tm)]
    x_args = list(xs) if split_in else [xs]
    return pl.pallas_call(
        _gmlp_split_in_kernel if split_in else _gmlp_kernel,
        grid=(LOAD_STEPS + N_TOK // tm,),
        in_specs=x_specs + [
            _mod_spec(layer, tm),
            _layer_spec(2 * layer, (1, D_MODEL)),
            _chunk_spec(j, D_MODEL, 2 * A_HALF),
            _layer_spec(j, (1, A_HALF)),
            _layer_spec(j, (A_GROUPS, CHUNK, CHUNK)),
            _layer_spec(j, (A_GROUPS, CHUNK, 1)),
            _chunk_spec(j, A_HALF, D_MODEL),
        ],
        out_specs=_tok_spec(tm),
        out_shape=jax.ShapeDtypeStruct((N_TOK, D_MODEL), F32),
        scratch_shapes=[
            pltpu.VMEM((D_MODEL, 2 * A_HALF), BF16),
            pltpu.VMEM((A_HALF, D_MODEL), BF16),
            pltpu.VMEM((tm, A_HALF), F32),
        ],
        compiler_params=_params(),
        name="gmlp",
    )(*x_args, mods, norm_g, w_in, v_gain, ws, bs, w_out)


FFN_TM = 512
FFN_TC = 1024
FFN_CW = D_FF // LOAD_STEPS
FFN_TILE0_STEP = LOAD_STEPS - 1


def _ffn_hidden_in(x, mod, g_ref):
    return (_rms(x, g_ref[...]) * (1.0 + mod[4:5]) + mod[3:4]).astype(BF16)


def _ffn_body(x, mod_ref, g_ref, w1_s, w2_s):
    mod = mod_ref[0]
    h = _ffn_hidden_in(x, mod, g_ref)
    acc = jnp.zeros((FFN_TM, D_MODEL), F32)
    for c in range(D_FF // FFN_TC):
        cols = slice(c * FFN_TC, (c + 1) * FFN_TC)
        hid = jnp.square(jnp.maximum(_dot(h, w1_s[:, cols]), 0.0)).astype(BF16)
        acc = acc + _dot(hid, w2_s[cols, :])
    return x + mod[5:6] * acc


def _ffn_kernel(*refs, proj, split_out):
    refs = list(refs)
    x_ref, mod_ref = refs[:2]
    del refs[:2]
    if proj:
        ap_ref, as_ref, wo_ref = refs[:3]
        del refs[:3]
    g_ref, w1_ref, w2_ref = refs[:3]
    del refs[:3]
    n_out = 2 if split_out else 1
    out_refs = refs[:n_out]
    del refs[:n_out]
    if proj:
        wo_s, x0_s = refs[:2]
        del refs[:2]
    w1_s, w2_s, h0_s, acc0_s = refs
    step = pl.program_id(0)

    def tile_input():
        x = x_ref[...]
        if proj:
            is_prompt = _is_prompt_tile(step, FFN_TM, FFN_TILE0_STEP)
            a = jnp.where(is_prompt, ap_ref[...], as_ref[...])
            x = x + mod_ref[0][2:3] * _dot(a, wo_s[...])
        return x

    for k in range(LOAD_STEPS):
        @pl.when(step == k)
        def _(k=k):
            units = slice(k * FFN_CW, (k + 1) * FFN_CW)
            w1_k = w1_ref[...].astype(BF16)
            w2_k = w2_ref[...].astype(BF16)
            w1_s[:, units] = w1_k
            w2_s[units, :] = w2_k
            mod = mod_ref[0]
            if k == 0:
                if proj:
                    wo_s[...] = wo_ref[...].astype(BF16)
                x0 = tile_input()
                if proj:
                    x0_s[...] = x0
                h0_s[...] = _ffn_hidden_in(x0, mod, g_ref)
            hid = jnp.square(jnp.maximum(_dot(h0_s[...], w1_k), 0.0)).astype(BF16)
            part = _dot(hid, w2_k)
            if k == 0:
                acc0_s[...] = part
            elif k < LOAD_STEPS - 1:
                acc0_s[...] = acc0_s[...] + part
            else:
                x0 = x0_s[...] if proj else x_ref[...]
                out_refs[0][...] = x0 + mod[5:6] * (acc0_s[...] + part)

    @pl.when(step >= LOAD_STEPS)
    def _():
        y = _ffn_body(tile_input(), mod_ref, g_ref, w1_s, w2_s)
        if not split_out:
            out_refs[0][...] = y
        else:
            is_prompt = _is_prompt_tile(step, FFN_TM, FFN_TILE0_STEP)

            @pl.when(is_prompt)
            def _():
                out_refs[0][...] = y

            @pl.when(jnp.logical_not(is_prompt))
            def _():
                out_refs[1][...] = y


def _ffn(x, mods, layer, norm_g, w1, w2, attn=None, split_out=False):
    tm, t0 = FFN_TM, FFN_TILE0_STEP
    proj = attn is not None
    if split_out:
        out_specs = [_prompt_tok_spec(tm, t0), _sample_tok_spec(tm, t0)]
        out_shape = [jax.ShapeDtypeStruct((P_TOK, D_MODEL), F32), jax.ShapeDtypeStruct((S_TOK, D_MODEL), F32)]
    else:
        out_specs = _tok_spec(tm, t0)
        out_shape = jax.ShapeDtypeStruct((N_TOK, D_MODEL), F32)
    in_specs = [_tok_spec(tm, t0), _mod_spec(layer, tm, t0)]
    args = [x, mods]
    scratch = []
    if proj:
        attn_prompt, attn_sample, j, w_o = attn
        in_specs += [_prompt_tok_spec(tm, t0), _sample_tok_spec(tm, t0), _layer_spec(j, (D_MODEL, D_MODEL))]
        args += [attn_prompt, attn_sample, w_o]
        scratch += [pltpu.VMEM((D_MODEL, D_MODEL), BF16), pltpu.VMEM((tm, D_MODEL), F32)]
    last = LOAD_STEPS - 1
    in_specs += [
        _layer_spec(2 * layer + 1, (1, D_MODEL)),
        pl.BlockSpec((None, D_MODEL, FFN_CW), lambda s: (layer, 0, jnp.minimum(s, last))),
        pl.BlockSpec((None, FFN_CW, D_MODEL), lambda s: (layer, jnp.minimum(s, last), 0)),
    ]
    args += [norm_g, w1, w2]
    scratch += [
        pltpu.VMEM((D_MODEL, D_FF), BF16),
        pltpu.VMEM((D_FF, D_MODEL), BF16),
        pltpu.VMEM((tm, D_MODEL), BF16),
        pltpu.VMEM((tm, D_MODEL), F32),
    ]
    return pl.pallas_call(
        functools.partial(_ffn_kernel, proj=proj, split_out=split_out),
        grid=(t0 + N_TOK // tm,),
        in_specs=in_specs,
        out_specs=out_specs,
        out_shape=out_shape,
        scratch_shapes=scratch,
        compiler_params=_params(),
        name="attn_proj_ffn" if proj else "ffn",
    )(*args)


CONV_TM = 1024
CONV_TC = 1024


def _conv_kernel(x_ref, mod_ref, g_ref, win_ref, cw_ref, cb_ref, wout_ref, o_ref, win_s, wout_s):
    step = pl.program_id(0)

    @pl.when(step < LOAD_STEPS)
    def _():
        _load_chunk(step, win_ref, win_s)
        _load_chunk(step, wout_ref, wout_s)

    @pl.when(step >= LOAD_STEPS)
    def _():
        seq_len = jnp.where(_is_prompt_tile(step, CONV_TM), SEQ, DEC_SEQ)
        pos = lax.broadcasted_iota(jnp.int32, (CONV_TM, 1), 0) & (seq_len - 1)
        has_prev = pos != 0
        has_next = pos != seq_len - 1
        x = x_ref[...]
        mod = mod_ref[0]
        h = (_rms(x, g_ref[...]) * (1.0 + mod[1:2]) + mod[0:1]).astype(BF16)
        cw = cw_ref[...]
        cb = cb_ref[...]
        acc = jnp.zeros((CONV_TM, D_MODEL), F32)
        for c in range(D_MODEL // CONV_TC):
            lo = c * CONV_TC
            cols = slice(lo, lo + CONV_TC)
            bg = _dot(h, win_s[:, lo:lo + CONV_TC])
            cg = _dot(h, win_s[:, D_MODEL + lo:D_MODEL + lo + CONV_TC])
            xt = _dot(h, win_s[:, 2 * D_MODEL + lo:2 * D_MODEL + lo + CONV_TC])
            z = cg * xt
            z_prev = jnp.where(has_prev, pltpu.roll(z, 1, axis=0), 0.0)
            z_next = jnp.where(has_next, pltpu.roll(z, CONV_TM - 1, axis=0), 0.0)
            zc = cw[0:1, cols] * z_prev + cw[1:2, cols] * z + cw[2:3, cols] * z_next + cb[:, cols]
            acc = acc + _dot((bg * zc).astype(BF16), wout_s[cols, :])
        o_ref[...] = x + mod[2:3] * acc


def _conv(x, mods, layer, norm_g, j, w_in, conv_w, conv_b, w_out):
    tm = CONV_TM
    return pl.pallas_call(
        _conv_kernel,
        grid=(LOAD_STEPS + N_TOK // tm,),
        in_specs=[
            _tok_spec(tm),
            _mod_spec(layer, tm),
            _layer_spec(2 * layer, (1, D_MODEL)),
            _chunk_spec(j, D_MODEL, 3 * D_MODEL),
            _layer_spec(j, (3, D_MODEL)),
            _layer_spec(j, (1, D_MODEL)),
            _chunk_spec(j, D_MODEL, D_MODEL),
        ],
        out_specs=_tok_spec(tm),
        out_shape=jax.ShapeDtypeStruct((N_TOK, D_MODEL), F32),
        scratch_shapes=[
            pltpu.VMEM((D_MODEL, 3 * D_MODEL), BF16),
            pltpu.VMEM((D_MODEL, D_MODEL), BF16),
        ],
        compiler_params=_params(),
        name="sconv",
    )(x, mods, norm_g, w_in, conv_w, conv_b, w_out)


QKV_TM = 512


def _qkv_kernel(x_ref, mod_ref, g_ref, w_ref, gq_ref, gk_ref, hm_ref,
                q_ref, k_ref, v_ref, kc_ref, vc_ref, w_s):
    step = pl.program_id(0)

    @pl.when(step < LOAD_STEPS)
    def _():
        _load_chunk(step, w_ref, w_s)

    @pl.when(step >= LOAD_STEPS)
    def _():
        x = x_ref[...]
        mod = mod_ref[0]
        h = (_rms(x, g_ref[...]) * (1.0 + mod[1:2]) + mod[0:1]).astype(BF16)
        head_mean = hm_ref[...]

        def head_norm(y, gain):
            sq = (y * y).astype(BF16)
            ms = jnp.concatenate(
                [_dot(sq[:, c:c + MXU_TILE], head_mean) for c in range(0, D_MODEL, MXU_TILE)], axis=1)
            return y * lax.rsqrt(ms + EPS) * gain

        q = head_norm(_dot(h, w_s[:, :D_MODEL]), gq_ref[...])
        k = head_norm(_dot(h, w_s[:, D_MODEL:2 * D_MODEL]), gk_ref[...])
        v = _dot(h, w_s[:, 2 * D_MODEL:])
        q_ref[...] = q.astype(BF16)
        k_ref[...] = k.astype(BF16)
        v_ref[...] = v.astype(BF16)

        @pl.when(_is_prompt_tile(step, QKV_TM))
        def _():
            kc_ref[...] = pltpu.einshape("m(hd)->mhd", k, h=N_HEADS)
            vc_ref[...] = pltpu.einshape("m(hd)->mhd", v, h=N_HEADS)


def _qkv(x, mods, layer, norm_g, j, w_qkv, q_gain, k_gain):
    tm = QKV_TM
    head_mean = jnp.asarray(
        np.kron(np.eye(MXU_TILE // HEAD_DIM), np.full((HEAD_DIM, HEAD_DIM), 1.0 / HEAD_DIM)), BF16)
    last_prompt_tile = P_TOK // tm - 1
    cache_spec = pl.BlockSpec(
        (tm, N_HEADS, HEAD_DIM), lambda s: (jnp.clip(s - LOAD_STEPS, 0, last_prompt_tile), 0, 0))
    return pl.pallas_call(
        _qkv_kernel,
        grid=(LOAD_STEPS + N_TOK // tm,),
        in_specs=[
            _tok_spec(tm),
            _mod_spec(layer, tm),
            _layer_spec(2 * layer, (1, D_MODEL)),
            _chunk_spec(j, D_MODEL, 3 * D_MODEL),
            _const_spec((1, D_MODEL)),
            _const_spec((1, D_MODEL)),
            _const_spec((MXU_TILE, MXU_TILE)),
        ],
        out_specs=[_tok_spec(tm), _tok_spec(tm), _tok_spec(tm), cache_spec, cache_spec],
        out_shape=[
            jax.ShapeDtypeStruct((N_TOK, D_MODEL), BF16),
            jax.ShapeDtypeStruct((N_TOK, D_MODEL), BF16),
            jax.ShapeDtypeStruct((N_TOK, D_MODEL), BF16),
            jax.ShapeDtypeStruct((P_TOK, N_HEADS, HEAD_DIM), F32),
            jax.ShapeDtypeStruct((P_TOK, N_HEADS, HEAD_DIM), F32),
        ],
        scratch_shapes=[pltpu.VMEM((D_MODEL, 3 * D_MODEL), BF16)],
        compiler_params=_params(),
        name="qkv",
    )(x, mods, norm_g, w_qkv,
      jnp.tile(q_gain[j], N_HEADS).reshape(1, D_MODEL), jnp.tile(k_gain[j], N_HEADS).reshape(1, D_MODEL),
      head_mean)


NBR_GROUP_ROWS = GRID_ROWS // 2
NBR_KEY_ROWS = NBR_GROUP_ROWS + WIN_ROWS // 2
NBR_Q = NBR_GROUP_ROWS * GRID_W
NBR_KEYS = NBR_KEY_ROWS * GRID_W
MASKED_TILE = RPB_ROWS
N_BIAS_TILES = RPB_ROWS + 1


def _band_start(r):
    return min(max(r - WIN_ROWS // 2, 0), GRID_ROWS - WIN_ROWS)


def _key_row_start(r):
    return 0 if r < NBR_GROUP_ROWS else GRID_ROWS - NBR_KEY_ROWS


def _bias_tile_index(r, i):
    key_row = _key_row_start(r) + i
    in_window = _band_start(r) <= key_row < _band_start(r) + WIN_ROWS
    return key_row - r + WIN_ROWS - 1 if in_window else MASKED_TILE


def _build_bias_tiles(rpb_ref, tiles_s, pair):
    qc = lax.broadcasted_iota(jnp.int32, (GRID_W, HEAD_PAIR), 0)
    kc = lax.broadcasted_iota(jnp.int32, (GRID_W, HEAD_PAIR), 1) & (GRID_W - 1)
    rel_c = jnp.clip(kc - qc + (WIN_COLS - 1), 0, RPB_COLS - 1)
    col_start = jnp.clip(qc - WIN_COLS // 2, 0, GRID_W - WIN_COLS)
    col_ok = (kc >= col_start) & (kc < col_start + WIN_COLS)
    for half in range(2):
        base = (2 * pair + half) * (RPB_ROWS * RPB_COLS)
        for rel_r in range(RPB_ROWS):
            tile = jnp.zeros((GRID_W, HEAD_PAIR), F32)
            for c in range(RPB_COLS):
                tile = jnp.where(rel_c == c, rpb_ref[base + rel_r * RPB_COLS + c], tile)
            tiles_s[half, rel_r] = jnp.where(col_ok, tile, MASK_VALUE)
        tiles_s[half, MASKED_TILE] = jnp.full((GRID_W, HEAD_PAIR), MASK_VALUE, F32)


def _head_lane_mask(half):
    lane = lax.broadcasted_iota(jnp.int32, (1, HEAD_PAIR), 1)
    return (lane >= HEAD_DIM) if half else (lane < HEAD_DIM)


def _scaled_head_queries(q2, sel):
    return jnp.where(sel, q2 * ATT_SCALE, jnp.zeros_like(q2))


def _values_and_ones(v2, sel):
    return jnp.where(sel, v2, jnp.ones_like(v2))


def _normalise(o):
    return o * (1.0 / pltpu.roll(o, HEAD_DIM, axis=1))


CTX_SEQS = 4


def _ctx_attn_kernel(q_ref, k_ref, v_ref, o_ref):
    for b in range(CTX_SEQS):
        rows = slice(b * SEQ, (b + 1) * SEQ)
        for p in range(N_PAIRS):
            cols = slice(p * HEAD_PAIR, (p + 1) * HEAD_PAIR)
            q2 = q_ref[rows, cols]
            k2 = k_ref[rows, cols]
            v2 = v_ref[rows, cols]
            out = None
            for half in range(2):
                sel = _head_lane_mask(half)
                s = _dot_nt(_scaled_head_queries(q2, sel), k2)
                e = jnp.exp(s - jnp.max(s, axis=-1, keepdims=True))
                o = _dot(e.astype(BF16), v2) / jnp.sum(e, axis=-1, keepdims=True)
                out = o if out is None else jnp.where(sel, o, out)
            o_ref[rows, cols] = out.astype(BF16)


def _ctx_attention(q, k, v):
    spec = pl.BlockSpec((CTX_SEQS * SEQ, D_MODEL), lambda b: (b, 0))
    return pl.pallas_call(
        _ctx_attn_kernel,
        grid=(BATCH // CTX_SEQS,),
        in_specs=[spec, spec, spec],
        out_specs=spec,
        out_shape=jax.ShapeDtypeStruct((P_TOK, D_MODEL), BF16),
        compiler_params=_params(),
        name="ctx_attention",
    )(q, k, v)


NBR_SOFTMAX_ROWS = 32


def _nbr_attn_kernel(rpb_ref, q_ref, k_ref, v_ref, ck_ref, cv_ref, o_ref, tiles_s):
    @pl.when(pl.program_id(1) == 0)
    def _():
        _build_bias_tiles(rpb_ref, tiles_s, pl.program_id(0))

    first_key_row = _head_lane_mask(0)
    ck2 = ck_ref[0]
    cv2 = cv_ref[0]
    blocks = [(g, half) for g in range(GRID_ROWS // NBR_GROUP_ROWS) for half in range(2)]

    def bias_rows(half, r, q_lo):
        rows = slice(q_lo, q_lo + NBR_SOFTMAX_ROWS)
        parts = []
        for i in range(0, NBR_KEY_ROWS, 2):
            left, right = _bias_tile_index(r, i), _bias_tile_index(r, i + 1)
            part = tiles_s[half, left, rows, :]
            if right != left:
                part = jnp.where(first_key_row, part, tiles_s[half, right, rows, :])
            parts.append(part)
        return jnp.concatenate(parts, axis=1)

    def scores(g, half):
        r0 = g * NBR_GROUP_ROWS
        key_lo = _key_row_start(r0) * GRID_W
        qm = _scaled_head_queries(q_ref[r0 * GRID_W:r0 * GRID_W + NBR_Q, :], _head_lane_mask(half))
        return _dot_nt(qm, k_ref[key_lo:key_lo + NBR_KEYS, :]), _dot_nt(qm, ck2)

    def attend(g, half, s_loc, s_ctx):
        r0 = g * NBR_GROUP_ROWS
        key_lo = _key_row_start(r0) * GRID_W
        sel = _head_lane_mask(half)
        e_loc, e_ctx = [], []
        for lo in range(0, NBR_Q, NBR_SOFTMAX_ROWS):
            hi = lo + NBR_SOFTMAX_ROWS
            r, q_lo = r0 + lo // GRID_W, lo % GRID_W
            sl = s_loc[lo:hi] + bias_rows(half, r, q_lo)
            sc = s_ctx[lo:hi]
            m = jnp.maximum(jnp.max(sl, axis=-1, keepdims=True), jnp.max(sc, axis=-1, keepdims=True))
            e_loc.append(jnp.exp(sl - m).astype(BF16))
            e_ctx.append(jnp.exp(sc - m).astype(BF16))
        o = _dot(jnp.concatenate(e_loc, axis=0), _values_and_ones(v_ref[key_lo:key_lo + NBR_KEYS, :], sel))
        o = o + _dot(jnp.concatenate(e_ctx, axis=0), _values_and_ones(cv2, sel))
        return _normalise(o)

    pending = scores(*blocks[0])
    outs = {}
    for i, (g, half) in enumerate(blocks):
        s_loc, s_ctx = pending
        if i + 1 < len(blocks):
            pending = scores(*blocks[i + 1])
        outs[(g, half)] = attend(g, half, s_loc, s_ctx)
    for g in range(GRID_ROWS // NBR_GROUP_ROWS):
        rows = slice(g * NBR_Q, (g + 1) * NBR_Q)
        o_ref[rows, :] = jnp.where(_head_lane_mask(0), outs[(g, 0)], outs[(g, 1)]).astype(BF16)


def _nbr_attention(rpb, q, k, v, ck, cv):
    first = P_TOK // DEC_SEQ
    tok = pl.BlockSpec((DEC_SEQ, HEAD_PAIR), lambda p, b: (first + b, p))
    ctx = pl.BlockSpec((1, PAST_LEN, HEAD_PAIR), lambda p, b: (b, 0, p))
    return pl.pallas_call(
        _nbr_attn_kernel,
        grid=(N_PAIRS, DEC_BATCH),
        in_specs=[pl.BlockSpec(memory_space=pltpu.SMEM), tok, tok, tok, ctx, ctx],
        out_specs=pl.BlockSpec((DEC_SEQ, HEAD_PAIR), lambda p, b: (b, p)),
        out_shape=jax.ShapeDtypeStruct((S_TOK, D_MODEL), BF16),
        scratch_shapes=[pltpu.VMEM((2, N_BIAS_TILES, GRID_W, HEAD_PAIR), F32)],
        compiler_params=_params(2),
        name="nbr_attention",
    )(rpb.reshape(N_HEADS * RPB_ROWS * RPB_COLS), q, k, v, ck, cv)


def kernel(x_prompt, x_sample, cache_k, cache_v, c, c_ctx, norm_g, ada_w, ada_b, a_w_in, a_v_gain, a_ws, a_bs, a_w_out, b_w_qkv, b_q_gain, b_k_gain, b_rpb, b_w_o, c_w_in, c_conv_w, c_conv_b, c_w_out, ff_w1, ff_w2):
    n_a = a_w_in.shape[0]
    cond = jnp.concatenate(
        [c, c_ctx[None, :], jnp.zeros((COND_ROWS - DEC_BATCH - 1, D_MODEL), F32)], axis=0)
    mods = _adaln(cond, ada_w, ada_b).reshape(DEPTH, COND_ROWS, 6, D_MODEL)
    norm_g = norm_g.reshape(2 * DEPTH, 1, D_MODEL)
    a_v_gain = a_v_gain.reshape(n_a, 1, A_HALF)
    a_bs = a_bs.reshape(n_a, A_GROUPS, CHUNK, 1)
    c_conv_b = c_conv_b.reshape(-1, 1, D_MODEL)
    x = (x_prompt.reshape(P_TOK, D_MODEL), x_sample.reshape(S_TOK, D_MODEL))
    new_k, new_v = [], []
    for i in range(DEPTH):
        kind, j = i % 3, i // 3
        attn = None
        if kind == 0:
            x = _gmlp(x, mods, i, norm_g, j, a_w_in, a_v_gain, a_ws, a_bs, a_w_out)
        elif kind == 1:
            q, k, v, k_new, v_new = _qkv(x, mods, i, norm_g, j, b_w_qkv, b_q_gain, b_k_gain)
            new_k.append(k_new.reshape(BATCH, SEQ, N_HEADS, HEAD_DIM))
            new_v.append(v_new.reshape(BATCH, SEQ, N_HEADS, HEAD_DIM))
            o_prompt = _ctx_attention(q, k, v)
            o_sample = _nbr_attention(
                b_rpb[j], q, k, v,
                cache_k[:, j].reshape(DEC_BATCH, PAST_LEN, D_MODEL).astype(BF16),
                cache_v[:, j].reshape(DEC_BATCH, PAST_LEN, D_MODEL).astype(BF16))
            attn = (o_prompt, o_sample, j, b_w_o)
        else:
            x = _conv(x, mods, i, norm_g, j, c_w_in, c_conv_w, c_conv_b, c_w_out)
        x = _ffn(x, mods, i, norm_g, ff_w1, ff_w2, attn=attn, split_out=(i == DEPTH - 1))
    y_prompt = x[0].reshape(BATCH, SEQ, D_MODEL)
    y_sample = x[1].reshape(DEC_BATCH, DEC_SEQ, D_MODEL)
    return (y_prompt, y_sample, jnp.stack(new_k, axis=1), jnp.stack(new_v, axis=1))
```

```python
import functools

import jax
import jax.numpy as jnp
import numpy as np
from jax import lax
from jax.experimental import pallas as pl
from jax.experimental.pallas import tpu as pltpu

D_MODEL = 1024
BATCH = 16
SEQ = 256
DEPTH = 4
DEC_BATCH = 8
DEC_SEQ = 1024
PAST_LEN = 512
GRID_W = 64
GRID_ROWS = DEC_SEQ // GRID_W
CHUNK = 128
A_HALF = 2 * D_MODEL
A_GROUPS = 8
A_GROUP_W = A_HALF // A_GROUPS
N_HEADS = 16
HEAD_DIM = D_MODEL // N_HEADS
WIN_ROWS = 8
WIN_COLS = 16
RPB_ROWS = 2 * WIN_ROWS - 1
RPB_COLS = 2 * WIN_COLS - 1
D_FF = 4 * D_MODEL
EPS = 1e-6
ATT_SCALE = HEAD_DIM ** -0.5
MASK_VALUE = -1e30

P_TOK = BATCH * SEQ
S_TOK = DEC_BATCH * DEC_SEQ
N_TOK = P_TOK + S_TOK
COND_ROWS = 16
CTX_ROW = DEC_BATCH
LOAD_STEPS = 8
HEAD_PAIR = 2 * HEAD_DIM
N_PAIRS = N_HEADS // 2
MXU_TILE = 256

F32 = jnp.float32
BF16 = jnp.bfloat16
VMEM_LIMIT = 56 * 1024 * 1024


def _params(n_axes=1, vmem=VMEM_LIMIT):
    return pltpu.CompilerParams(dimension_semantics=("arbitrary",) * n_axes, vmem_limit_bytes=vmem)


def _dot(a, b):
    return jnp.dot(a, b, preferred_element_type=F32)


def _dot_nt(a, b):
    return lax.dot_general(a, b, (((1,), (1,)), ((), ())), preferred_element_type=F32)


def _rms(x, g):
    return x * lax.rsqrt(jnp.mean(x * x, axis=-1, keepdims=True) + EPS) * g


def _gelu(x):
    return 0.5 * x * (1.0 + jnp.tanh(0.7978845608028654 * (x + 0.044715 * (x * x * x))))


def _tok_spec(tm, tile0_step=LOAD_STEPS):
    return pl.BlockSpec((tm, D_MODEL), lambda s: (jnp.maximum(s - tile0_step, 0), 0))


def _prompt_tok_spec(tm, tile0_step=LOAD_STEPS):
    last = P_TOK // tm - 1
    return pl.BlockSpec((tm, D_MODEL), lambda s: (jnp.clip(s - tile0_step, 0, last), 0))


def _sample_tok_spec(tm, tile0_step=LOAD_STEPS):
    first, last = P_TOK // tm, S_TOK // tm - 1
    return pl.BlockSpec((tm, D_MODEL), lambda s: (jnp.clip(s - tile0_step - first, 0, last), 0))


def _is_prompt_tile(step, tm, tile0_step=LOAD_STEPS):
    return step - tile0_step < P_TOK // tm


def _mod_spec(layer, tm, tile0_step=LOAD_STEPS):
    n_prompt_tiles = P_TOK // tm
    tiles_per_seq = DEC_SEQ // tm

    def index(s):
        t = jnp.maximum(s - tile0_step, 0)
        row = jnp.where(t < n_prompt_tiles, CTX_ROW, jnp.maximum(t - n_prompt_tiles, 0) // tiles_per_seq)
        return (layer, row, 0, 0)

    return pl.BlockSpec((None, 1, 6, D_MODEL), index)


def _chunk_spec(layer, rows, cols):
    return pl.BlockSpec((None, rows // LOAD_STEPS, cols),
                        lambda s: (layer, jnp.minimum(s, LOAD_STEPS - 1), 0))


def _layer_spec(layer, shape):
    zeros = (0,) * len(shape)
    return pl.BlockSpec((None,) + tuple(shape), lambda s: (layer,) + zeros)


def _const_spec(shape):
    zeros = (0,) * len(shape)
    return pl.BlockSpec(shape, lambda s: zeros)


def _load_chunk(step, w_ref, w_scr):
    rows = w_ref.shape[0]
    off = pl.multiple_of(step * rows, rows)
    w_scr[pl.ds(off, rows), :] = w_ref[...].astype(BF16)


ADA_TK = 256
ADA_TC = 1024


def _ada_kernel(c_ref, w_ref, b_ref, o_ref):
    n_out = o_ref.shape[-1]

    @pl.when(pl.program_id(1) == 0)
    def _():
        o_ref[0] = jnp.broadcast_to(b_ref[0], (COND_ROWS, n_out))

    c = c_ref[0]
    a = c * (1.0 / (1.0 + jnp.exp(-c)))
    a_hi = a.astype(BF16)
    a_lo = (a - a_hi.astype(F32)).astype(BF16)
    for lo in range(0, n_out, ADA_TC):
        cols = slice(lo, lo + ADA_TC)
        w = w_ref[0, :, cols]
        w_hi = w.astype(BF16)
        w_lo = (w - w_hi.astype(F32)).astype(BF16)
        o_ref[0, :, cols] = o_ref[0, :, cols] + (_dot(a_hi, w_hi) + _dot(a_hi, w_lo) + _dot(a_lo, w_hi))


def _adaln(cond, ada_w, ada_b):
    n_out = 6 * D_MODEL
    return pl.pallas_call(
        _ada_kernel,
        grid=(DEPTH, D_MODEL // ADA_TK),
        in_specs=[
            pl.BlockSpec((1, COND_ROWS, ADA_TK), lambda i, k: (k, 0, 0)),
            pl.BlockSpec((1, ADA_TK, n_out), lambda i, k: (i, k, 0)),
            pl.BlockSpec((1, 1, n_out), lambda i, k: (i, 0, 0)),
        ],
        out_specs=pl.BlockSpec((1, COND_ROWS, n_out), lambda i, k: (i, 0, 0)),
        out_shape=jax.ShapeDtypeStruct((DEPTH, COND_ROWS, n_out), F32),
        compiler_params=_params(2),
        name="adaln",
    )(cond, ada_w, ada_b.reshape(DEPTH, 1, n_out))


GMLP_TM = 512
GMLP_TC = 1024


def _gmlp_gate_half(h, win_s, v_s):
    ssq = jnp.zeros((GMLP_TM, 1), F32)
    for c in range(0, A_HALF, GMLP_TC):
        cols = slice(c, c + GMLP_TC)
        v_c = _gelu(_dot(h, win_s[:, A_HALF + c:A_HALF + c + GMLP_TC]))
        ssq = ssq + jnp.sum(v_c * v_c, axis=-1, keepdims=True)
        v_s[:, cols] = v_c
    return lax.rsqrt(ssq * (1.0 / A_HALF) + EPS)


def _gmlp_mix(h, inv_rms, vg_ref, ws_ref, bs_ref, win_s, wout_s, v_s):
    acc = jnp.zeros((GMLP_TM, D_MODEL), F32)
    for c in range(0, A_HALF, GMLP_TC):
        cols = slice(c, c + GMLP_TC)
        v_n = (v_s[:, cols] * inv_rms * vg_ref[:, cols]).astype(BF16)
        u_c = _gelu(_dot(h, win_s[:, cols]))
        gated_cols = []
        for g in range(GMLP_TC // A_GROUP_W):
            gcols = slice(g * A_GROUP_W, (g + 1) * A_GROUP_W)
            w_g = ws_ref[c // A_GROUP_W + g].astype(BF16)
            b_g = bs_ref[c // A_GROUP_W + g]
            gated = []
            for n in range(GMLP_TM // CHUNK):
                rows = slice(n * CHUNK, (n + 1) * CHUNK)
                s = _dot(w_g, v_n[rows, gcols]) + b_g
                gated.append((u_c[rows, gcols] * s).astype(BF16))
            gated_cols.append(jnp.concatenate(gated, axis=0))
        acc = acc + _dot(jnp.concatenate(gated_cols, axis=1), wout_s[cols, :])
    return acc


def _gmlp_body(x, mod_ref, g_ref, vg_ref, ws_ref, bs_ref, o_ref, win_s, wout_s, v_s):
    mod = mod_ref[0]
    h = (_rms(x, g_ref[...]) * (1.0 + mod[1:2]) + mod[0:1]).astype(BF16)
    inv_rms = _gmlp_gate_half(h, win_s, v_s)
    o_ref[...] = x + mod[2:3] * _gmlp_mix(h, inv_rms, vg_ref, ws_ref, bs_ref, win_s, wout_s, v_s)


def _gmlp_kernel(x_ref, mod_ref, g_ref, win_ref, vg_ref, ws_ref, bs_ref, wout_ref, o_ref,
                 win_s, wout_s, v_s):
    step = pl.program_id(0)

    @pl.when(step < LOAD_STEPS)
    def _():
        _load_chunk(step, win_ref, win_s)
        _load_chunk(step, wout_ref, wout_s)

    @pl.when(step >= LOAD_STEPS)
    def _():
        _gmlp_body(x_ref[...], mod_ref, g_ref, vg_ref, ws_ref, bs_ref, o_ref, win_s, wout_s, v_s)


def _gmlp_split_in_kernel(xp_ref, xs_ref, mod_ref, g_ref, win_ref, vg_ref, ws_ref, bs_ref, wout_ref,
                          o_ref, win_s, wout_s, v_s):
    step = pl.program_id(0)

    @pl.when(step < LOAD_STEPS)
    def _():
        _load_chunk(step, win_ref, win_s)
        _load_chunk(step, wout_ref, wout_s)

    @pl.when(step >= LOAD_STEPS)
    def _():
        x = jnp.where(_is_prompt_tile(step, GMLP_TM), xp_ref[...], xs_ref[...])
        _gmlp_body(x, mod_ref, g_ref, vg_ref, ws_ref, bs_ref, o_ref, win_s, wout_s, v_s)


def _gmlp(xs, mods, layer, norm_g, j, w_in, v_gain, ws, bs, w_out):
    tm = GMLP_TM
    split_in = isinstance(xs, tuple)
    x_specs = [_prompt_tok_spec(tm), _sample_tok_spec(tm)] if split_in else [_tok_spec(tm)]
    x_args = list(xs) if split_in else [xs]
    return pl.pallas_call(
        _gmlp_split_in_kernel if split_in else _gmlp_kernel,
        grid=(LOAD_STEPS + N_TOK // tm,),
        in_specs=x_specs + [
            _mod_spec(layer, tm),
            _layer_spec(2 * layer, (1, D_MODEL)),
            _chunk_spec(j, D_MODEL, 2 * A_HALF),
            _layer_spec(j, (1, A_HALF)),
            _layer_spec(j, (A_GROUPS, CHUNK, CHUNK)),
            _layer_spec(j, (A_GROUPS, CHUNK, 1)),
            _chunk_spec(j, A_HALF, D_MODEL),
        ],
        out_specs=_tok_spec(tm),
        out_shape=jax.ShapeDtypeStruct((N_TOK, D_MODEL), F32),
        scratch_shapes=[
            pltpu.VMEM((D_MODEL, 2 * A_HALF), BF16),
            pltpu.VMEM((A_HALF, D_MODEL), BF16),
            pltpu.VMEM((tm, A_HALF), F32),
        ],
        compiler_params=_params(),
        name="gmlp",
    )(*x_args, mods, norm_g, w_in, v_gain, ws, bs, w_out)


FFN_TM = 512
FFN_TC = 1024
FFN_CW = D_FF // LOAD_STEPS
FFN_TILE0_STEP = LOAD_STEPS - 1


def _ffn_hidden_in(x, mod, g_ref):
    return (_rms(x, g_ref[...]) * (1.0 + mod[4:5]) + mod[3:4]).astype(BF16)


def _ffn_body(x, mod_ref, g_ref, w1_s, w2_s):
    mod = mod_ref[0]
    h = _ffn_hidden_in(x, mod, g_ref)
    acc = jnp.zeros((FFN_TM, D_MODEL), F32)
    for c in range(D_FF // FFN_TC):
        cols = slice(c * FFN_TC, (c + 1) * FFN_TC)
        hid = jnp.square(jnp.maximum(_dot(h, w1_s[:, cols]), 0.0)).astype(BF16)
        acc = acc + _dot(hid, w2_s[cols, :])
    return x + mod[5:6] * acc


def _ffn_kernel(*refs, proj, split_out):
    refs = list(refs)
    x_ref, mod_ref = refs[:2]
    del refs[:2]
    if proj:
        ap_ref, as_ref, wo_ref = refs[:3]
        del refs[:3]
    g_ref, w1_ref, w2_ref = refs[:3]
    del refs[:3]
    n_out = 2 if split_out else 1
    out_refs = refs[:n_out]
    del refs[:n_out]
    if proj:
        wo_s, x0_s = refs[:2]
        del refs[:2]
    w1_s, w2_s, h0_s, acc0_s = refs
    step = pl.program_id(0)

    def tile_input():
        x = x_ref[...]
        if proj:
            is_prompt = _is_prompt_tile(step, FFN_TM, FFN_TILE0_STEP)
            a = jnp.where(is_prompt, ap_ref[...], as_ref[...])
            x = x + mod_ref[0][2:3] * _dot(a, wo_s[...])
        return x

    for k in range(LOAD_STEPS):
        @pl.when(step == k)
        def _(k=k):
            units = slice(k * FFN_CW, (k + 1) * FFN_CW)
            w1_k = w1_ref[...].astype(BF16)
            w2_k = w2_ref[...].astype(BF16)
            w1_s[:, units] = w1_k
            w2_s[units, :] = w2_k
            mod = mod_ref[0]
            if k == 0:
                if proj:
                    wo_s[...] = wo_ref[...].astype(BF16)
                x0 = tile_input()
                if proj:
                    x0_s[...] = x0
                h0_s[...] = _ffn_hidden_in(x0, mod, g_ref)
            hid = jnp.square(jnp.maximum(_dot(h0_s[...], w1_k), 0.0)).astype(BF16)
            part = _dot(hid, w2_k)
            if k == 0:
                acc0_s[...] = part
            elif k < LOAD_STEPS - 1:
                acc0_s[...] = acc0_s[...] + part
            else:
                x0 = x0_s[...] if proj else x_ref[...]
                out_refs[0][...] = x0 + mod[5:6] * (acc0_s[...] + part)

    @pl.when(step >= LOAD_STEPS)
    def _():
        y = _ffn_body(tile_input(), mod_ref, g_ref, w1_s, w2_s)
        if not split_out:
            out_refs[0][...] = y
        else:
            is_prompt = _is_prompt_tile(step, FFN_TM, FFN_TILE0_STEP)

            @pl.when(is_prompt)
            def _():
                out_refs[0][...] = y

            @pl.when(jnp.logical_not(is_prompt))
            def _():
                out_refs[1][...] = y


def _ffn(x, mods, layer, norm_g, w1, w2, attn=None, split_out=False):
    tm, t0 = FFN_TM, FFN_TILE0_STEP
    proj = attn is not None
    if split_out:
        out_specs = [_prompt_tok_spec(tm, t0), _sample_tok_spec(tm, t0)]
        out_shape = [jax.ShapeDtypeStruct((P_TOK, D_MODEL), F32), jax.ShapeDtypeStruct((S_TOK, D_MODEL), F32)]
    else:
        out_specs = _tok_spec(tm, t0)
        out_shape = jax.ShapeDtypeStruct((N_TOK, D_MODEL), F32)
    in_specs = [_tok_spec(tm, t0), _mod_spec(layer, tm, t0)]
    args = [x, mods]
    scratch = []
    if proj:
        attn_prompt, attn_sample, j, w_o = attn
        in_specs += [_prompt_tok_spec(tm, t0), _sample_tok_spec(tm, t0), _layer_spec(j, (D_MODEL, D_MODEL))]
        args += [attn_prompt, attn_sample, w_o]
        scratch += [pltpu.VMEM((D_MODEL, D_MODEL), BF16), pltpu.VMEM((tm, D_MODEL), F32)]
    last = LOAD_STEPS - 1
    in_specs += [
        _layer_spec(2 * layer + 1, (1, D_MODEL)),
        pl.BlockSpec((None, D_MODEL, FFN_CW), lambda s: (layer, 0, jnp.minimum(s, last))),
        pl.BlockSpec((None, FFN_CW, D_MODEL), lambda s: (layer, jnp.minimum(s, last), 0)),
    ]
    args += [norm_g, w1, w2]
    scratch += [
        pltpu.VMEM((D_MODEL, D_FF), BF16),
        pltpu.VMEM((D_FF, D_MODEL), BF16),
        pltpu.VMEM((tm, D_MODEL), BF16),
        pltpu.VMEM((tm, D_MODEL), F32),
    ]
    return pl.pallas_call(
        functools.partial(_ffn_kernel, proj=proj, split_out=split_out),
        grid=(t0 + N_TOK // tm,),
        in_specs=in_specs,
        out_specs=out_specs,
        out_shape=out_shape,
        scratch_shapes=scratch,
        compiler_params=_params(),
        name="attn_proj_ffn" if proj else "ffn",
    )(*args)


CONV_TM = 1024
CONV_TC = 1024


def _conv_kernel(x_ref, mod_ref, g_ref, win_ref, cw_ref, cb_ref, wout_ref, o_ref, win_s, wout_s):
    step = pl.program_id(0)

    @pl.when(step < LOAD_STEPS)
    def _():
        _load_chunk(step, win_ref, win_s)
        _load_chunk(step, wout_ref, wout_s)

    @pl.when(step >= LOAD_STEPS)
    def _():
        seq_len = jnp.where(_is_prompt_tile(step, CONV_TM), SEQ, DEC_SEQ)
        pos = lax.broadcasted_iota(jnp.int32, (CONV_TM, 1), 0) & (seq_len - 1)
        has_prev = pos != 0
        has_next = pos != seq_len - 1
        x = x_ref[...]
        mod = mod_ref[0]
        h = (_rms(x, g_ref[...]) * (1.0 + mod[1:2]) + mod[0:1]).astype(BF16)
        cw = cw_ref[...]
        cb = cb_ref[...]
        acc = jnp.zeros((CONV_TM, D_MODEL), F32)
        for c in range(D_MODEL // CONV_TC):
            lo = c * CONV_TC
            cols = slice(lo, lo + CONV_TC)
            bg = _dot(h, win_s[:, lo:lo + CONV_TC])
            cg = _dot(h, win_s[:, D_MODEL + lo:D_MODEL + lo + CONV_TC])
            xt = _dot(h, win_s[:, 2 * D_MODEL + lo:2 * D_MODEL + lo + CONV_TC])
            z = cg * xt
            z_prev = jnp.where(has_prev, pltpu.roll(z, 1, axis=0), 0.0)
            z_next = jnp.where(has_next, pltpu.roll(z, CONV_TM - 1, axis=0), 0.0)
            zc = cw[0:1, cols] * z_prev + cw[1:2, cols] * z + cw[2:3, cols] * z_next + cb[:, cols]
            acc = acc + _dot((bg * zc).astype(BF16), wout_s[cols, :])
        o_ref[...] = x + mod[2:3] * acc


def _conv(x, mods, layer, norm_g, j, w_in, conv_w, conv_b, w_out):
    tm = CONV_TM
    return pl.pallas_call(
        _conv_kernel,
        grid=(LOAD_STEPS + N_TOK // tm,),
        in_specs=[
            _tok_spec(tm),
            _mod_spec(layer, tm),
            _layer_spec(2 * layer, (1, D_MODEL)),
            _chunk_spec(j, D_MODEL, 3 * D_MODEL),
            _layer_spec(j, (3, D_MODEL)),
            _layer_spec(j, (1, D_MODEL)),
            _chunk_spec(j, D_MODEL, D_MODEL),
        ],
        out_specs=_tok_spec(tm),
        out_shape=jax.ShapeDtypeStruct((N_TOK, D_MODEL), F32),
        scratch_shapes=[
            pltpu.VMEM((D_MODEL, 3 * D_MODEL), BF16),
            pltpu.VMEM((D_MODEL, D_MODEL), BF16),
        ],
        compiler_params=_params(),
        name="sconv",
    )(x, mods, norm_g, w_in, conv_w, conv_b, w_out)


QKV_TM = 512


def _qkv_kernel(x_ref, mod_ref, g_ref, w_ref, gq_ref, gk_ref, hm_ref,
                q_ref, k_ref, v_ref, kc_ref, vc_ref, w_s):
    step = pl.program_id(0)

    @pl.when(step < LOAD_STEPS)
    def _():
        _load_chunk(step, w_ref, w_s)

    @pl.when(step >= LOAD_STEPS)
    def _():
        x = x_ref[...]
        mod = mod_ref[0]
        h = (_rms(x, g_ref[...]) * (1.0 + mod[1:2]) + mod[0:1]).astype(BF16)
        head_mean = hm_ref[...]

        def head_norm(y, gain):
            sq = (y * y).astype(BF16)
            ms = jnp.concatenate(
                [_dot(sq[:, c:c + MXU_TILE], head_mean) for c in range(0, D_MODEL, MXU_TILE)], axis=1)
            return y * lax.rsqrt(ms + EPS) * gain

        q = head_norm(_dot(h, w_s[:, :D_MODEL]), gq_ref[...])
        k = head_norm(_dot(h, w_s[:, D_MODEL:2 * D_MODEL]), gk_ref[...])
        v = _dot(h, w_s[:, 2 * D_MODEL:])
        q_ref[...] = q.astype(BF16)
        k_ref[...] = k.astype(BF16)
        v_ref[...] = v.astype(BF16)

        @pl.when(_is_prompt_tile(step, QKV_TM))
        def _():
            kc_ref[...] = pltpu.einshape("m(hd)->mhd", k, h=N_HEADS)
            vc_ref[...] = pltpu.einshape("m(hd)->mhd", v, h=N_HEADS)


def _qkv(x, mods, layer, norm_g, j, w_qkv, q_gain, k_gain):
    tm = QKV_TM
    head_mean = jnp.asarray(
        np.kron(np.eye(MXU_TILE // HEAD_DIM), np.full((HEAD_DIM, HEAD_DIM), 1.0 / HEAD_DIM)), BF16)
    last_prompt_tile = P_TOK // tm - 1
    cache_spec = pl.BlockSpec(
        (tm, N_HEADS, HEAD_DIM), lambda s: (jnp.clip(s - LOAD_STEPS, 0, last_prompt_tile), 0, 0))
    return pl.pallas_call(
        _qkv_kernel,
        grid=(LOAD_STEPS + N_TOK // tm,),
        in_specs=[
            _tok_spec(tm),
            _mod_spec(layer, tm),
            _layer_spec(2 * layer, (1, D_MODEL)),
            _chunk_spec(j, D_MODEL, 3 * D_MODEL),
            _const_spec((1, D_MODEL)),
            _const_spec((1, D_MODEL)),
            _const_spec((MXU_TILE, MXU_TILE)),
        ],
        out_specs=[_tok_spec(tm), _tok_spec(tm), _tok_spec(tm), cache_spec, cache_spec],
        out_shape=[
            jax.ShapeDtypeStruct((N_TOK, D_MODEL), BF16),
            jax.ShapeDtypeStruct((N_TOK, D_MODEL), BF16),
            jax.ShapeDtypeStruct((N_TOK, D_MODEL), BF16),
            jax.ShapeDtypeStruct((P_TOK, N_HEADS, HEAD_DIM), F32),
            jax.ShapeDtypeStruct((P_TOK, N_HEADS, HEAD_DIM), F32),
        ],
        scratch_shapes=[pltpu.VMEM((D_MODEL, 3 * D_MODEL), BF16)],
        compiler_params=_params(),
        name="qkv",
    )(x, mods, norm_g, w_qkv,
      jnp.tile(q_gain[j], N_HEADS).reshape(1, D_MODEL), jnp.tile(k_gain[j], N_HEADS).reshape(1, D_MODEL),
      head_mean)


NBR_GROUP_ROWS = GRID_ROWS // 2
NBR_KEY_ROWS = NBR_GROUP_ROWS + WIN_ROWS // 2
NBR_Q = NBR_GROUP_ROWS * GRID_W
NBR_KEYS = NBR_KEY_ROWS * GRID_W
MASKED_TILE = RPB_ROWS
N_BIAS_TILES = RPB_ROWS + 1


def _band_start(r):
    return min(max(r - WIN_ROWS // 2, 0), GRID_ROWS - WIN_ROWS)


def _key_row_start(r):
    return 0 if r < NBR_GROUP_ROWS else GRID_ROWS - NBR_KEY_ROWS


def _bias_tile_index(r, i):
    key_row = _key_row_start(r) + i
    in_window = _band_start(r) <= key_row < _band_start(r) + WIN_ROWS
    return key_row - r + WIN_ROWS - 1 if in_window else MASKED_TILE


def _build_bias_tiles(rpb_ref, tiles_s, pair):
    half_w = GRID_W // 2
    lane = lax.broadcasted_iota(jnp.int32, (half_w, HEAD_PAIR), 1)
    low = lane < GRID_W
    qc = lax.broadcasted_iota(jnp.int32, (half_w, HEAD_PAIR), 0) + jnp.where(low, 0, half_w)
    kc = lane & (GRID_W - 1)
    rel_c = jnp.clip(kc - qc + (WIN_COLS - 1), 0, RPB_COLS - 1)
    col_start = jnp.clip(qc - WIN_COLS // 2, 0, GRID_W - WIN_COLS)
    col_ok = (kc >= col_start) & (kc < col_start + WIN_COLS)
    for half in range(2):
        base = (2 * pair + half) * (RPB_ROWS * RPB_COLS)
        for rel_r in range(RPB_ROWS):
            packed = jnp.zeros((half_w, HEAD_PAIR), F32)
            for c in range(RPB_COLS):
                packed = jnp.where(rel_c == c, rpb_ref[base + rel_r * RPB_COLS + c], packed)
            packed = jnp.where(col_ok, packed, MASK_VALUE)
            swapped = pltpu.roll(packed, GRID_W, axis=1)
            tiles_s[half, rel_r, :half_w] = jnp.where(low, packed, swapped)
            tiles_s[half, rel_r, half_w:] = jnp.where(low, swapped, packed)
        tiles_s[half, MASKED_TILE] = jnp.full((GRID_W, HEAD_PAIR), MASK_VALUE, F32)


def _head_lane_mask(half):
    lane = lax.broadcasted_iota(jnp.int32, (1, HEAD_PAIR), 1)
    return (lane >= HEAD_DIM) if half else (lane < HEAD_DIM)


def _scaled_head_queries(q2, sel):
    return jnp.where(sel, q2 * ATT_SCALE, jnp.zeros_like(q2))


def _values_and_ones(v2, sel):
    return jnp.where(sel, v2, jnp.ones_like(v2))


def _normalise(o):
    return o * (1.0 / pltpu.roll(o, HEAD_DIM, axis=1))


CTX_SEQS = 4


def _ctx_attn_kernel(q_ref, k_ref, v_ref, o_ref):
    for b in range(CTX_SEQS):
        rows = slice(b * SEQ, (b + 1) * SEQ)
        for p in range(N_PAIRS):
            cols = slice(p * HEAD_PAIR, (p + 1) * HEAD_PAIR)
            q2 = q_ref[rows, cols]
            k2 = k_ref[rows, cols]
            v2 = v_ref[rows, cols]
            out = None
            for half in range(2):
                sel = _head_lane_mask(half)
                s = _dot_nt(_scaled_head_queries(q2, sel), k2)
                e = jnp.exp(s - jnp.max(s, axis=-1, keepdims=True))
                o = _dot(e.astype(BF16), v2) / jnp.sum(e, axis=-1, keepdims=True)
                out = o if out is None else jnp.where(sel, o, out)
            o_ref[rows, cols] = out.astype(BF16)


def _ctx_attention(q, k, v):
    spec = pl.BlockSpec((CTX_SEQS * SEQ, D_MODEL), lambda b: (b, 0))
    return pl.pallas_call(
        _ctx_attn_kernel,
        grid=(BATCH // CTX_SEQS,),
        in_specs=[spec, spec, spec],
        out_specs=spec,
        out_shape=jax.ShapeDtypeStruct((P_TOK, D_MODEL), BF16),
        compiler_params=_params(),
        name="ctx_attention",
    )(q, k, v)


NBR_SOFTMAX_ROWS = 32


def _nbr_attn_kernel(rpb_ref, q_ref, k_ref, v_ref, ck_ref, cv_ref, o_ref, tiles_s):
    @pl.when(pl.program_id(1) == 0)
    def _():
        _build_bias_tiles(rpb_ref, tiles_s, pl.program_id(0))

    first_key_row = _head_lane_mask(0)
    ck2 = ck_ref[0].astype(BF16)
    cv2 = cv_ref[0].astype(BF16)
    blocks = [(g, half) for g in range(GRID_ROWS // NBR_GROUP_ROWS) for half in range(2)]

    def bias_rows(half, r, q_lo):
        rows = slice(q_lo, q_lo + NBR_SOFTMAX_ROWS)
        parts = []
        for i in range(0, NBR_KEY_ROWS, 2):
            left, right = _bias_tile_index(r, i), _bias_tile_index(r, i + 1)
            part = tiles_s[half, left, rows, :]
            if right != left:
                part = jnp.where(first_key_row, part, tiles_s[half, right, rows, :])
            parts.append(part)
        return jnp.concatenate(parts, axis=1)

    def scores(g, half):
        r0 = g * NBR_GROUP_ROWS
        key_lo = _key_row_start(r0) * GRID_W
        qm = _scaled_head_queries(q_ref[r0 * GRID_W:r0 * GRID_W + NBR_Q, :], _head_lane_mask(half))
        return _dot_nt(qm, k_ref[key_lo:key_lo + NBR_KEYS, :]), _dot_nt(qm, ck2)

    def attend(g, half, s_loc, s_ctx):
        r0 = g * NBR_GROUP_ROWS
        key_lo = _key_row_start(r0) * GRID_W
        sel = _head_lane_mask(half)
        e_loc, e_ctx = [], []
        for lo in range(0, NBR_Q, NBR_SOFTMAX_ROWS):
            hi = lo + NBR_SOFTMAX_ROWS
            r, q_lo = r0 + lo // GRID_W, lo % GRID_W
            sl = s_loc[lo:hi] + bias_rows(half, r, q_lo)
            sc = s_ctx[lo:hi]
            m = jnp.maximum(jnp.max(sl, axis=-1, keepdims=True), jnp.max(sc, axis=-1, keepdims=True))
            e_loc.append(jnp.exp(sl - m).astype(BF16))
            e_ctx.append(jnp.exp(sc - m).astype(BF16))
        o = _dot(jnp.concatenate(e_loc, axis=0), _values_and_ones(v_ref[key_lo:key_lo + NBR_KEYS, :], sel))
        o = o + _dot(jnp.concatenate(e_ctx, axis=0), _values_and_ones(cv2, sel))
        return _normalise(o)

    pending = scores(*blocks[0])
    outs = {}
    for i, (g, half) in enumerate(blocks):
        s_loc, s_ctx = pending
        if i + 1 < len(blocks):
            pending = scores(*blocks[i + 1])
        outs[(g, half)] = attend(g, half, s_loc, s_ctx)
    for g in range(GRID_ROWS // NBR_GROUP_ROWS):
        rows = slice(g * NBR_Q, (g + 1) * NBR_Q)
        o_ref[rows, :] = jnp.where(_head_lane_mask(0), outs[(g, 0)], outs[(g, 1)]).astype(BF16)


def _nbr_attention(rpb, q, k, v, ck, cv):
    first = P_TOK // DEC_SEQ
    tok = pl.BlockSpec((DEC_SEQ, HEAD_PAIR), lambda p, b: (first + b, p))
    ctx = pl.BlockSpec((1, PAST_LEN, HEAD_PAIR), lambda p, b: (b, 0, p))
    return pl.pallas_call(
        _nbr_attn_kernel,
        grid=(N_PAIRS, DEC_BATCH),
        in_specs=[pl.BlockSpec(memory_space=pltpu.SMEM), tok, tok, tok, ctx, ctx],
        out_specs=pl.BlockSpec((DEC_SEQ, HEAD_PAIR), lambda p, b: (b, p)),
        out_shape=jax.ShapeDtypeStruct((S_TOK, D_MODEL), BF16),
        scratch_shapes=[pltpu.VMEM((2, N_BIAS_TILES, GRID_W, HEAD_PAIR), F32)],
        compiler_params=_params(2),
        name="nbr_attention",
    )(rpb.reshape(N_HEADS * RPB_ROWS * RPB_COLS), q, k, v, ck, cv)


def kernel(x_prompt, x_sample, cache_k, cache_v, c, c_ctx, norm_g, ada_w, ada_b, a_w_in, a_v_gain, a_ws, a_bs, a_w_out, b_w_qkv, b_q_gain, b_k_gain, b_rpb, b_w_o, c_w_in, c_conv_w, c_conv_b, c_w_out, ff_w1, ff_w2):
    n_a = a_w_in.shape[0]
    cond = jnp.concatenate(
        [c, c_ctx[None, :], jnp.zeros((COND_ROWS - DEC_BATCH - 1, D_MODEL), F32)], axis=0)
    cond = cond.reshape(COND_ROWS, D_MODEL // ADA_TK, ADA_TK).transpose(1, 0, 2)
    mods = _adaln(cond, ada_w, ada_b).reshape(DEPTH, COND_ROWS, 6, D_MODEL)
    norm_g = norm_g.reshape(2 * DEPTH, 1, D_MODEL)
    a_v_gain = a_v_gain.reshape(n_a, 1, A_HALF)
    a_bs = a_bs.reshape(n_a, A_GROUPS, CHUNK, 1)
    c_conv_b = c_conv_b.reshape(-1, 1, D_MODEL)
    x = (x_prompt.reshape(P_TOK, D_MODEL), x_sample.reshape(S_TOK, D_MODEL))
    new_k, new_v = [], []
    for i in range(DEPTH):
        kind, j = i % 3, i // 3
        attn = None
        if kind == 0:
            x = _gmlp(x, mods, i, norm_g, j, a_w_in, a_v_gain, a_ws, a_bs, a_w_out)
        elif kind == 1:
            q, k, v, k_new, v_new = _qkv(x, mods, i, norm_g, j, b_w_qkv, b_q_gain, b_k_gain)
            new_k.append(k_new.reshape(BATCH, SEQ, N_HEADS, HEAD_DIM))
            new_v.append(v_new.reshape(BATCH, SEQ, N_HEADS, HEAD_DIM))
            o_prompt = _ctx_attention(q, k, v)
            o_sample = _nbr_attention(
                b_rpb[j], q, k, v,
                cache_k[:, j].reshape(DEC_BATCH, PAST_LEN, D_MODEL),
                cache_v[:, j].reshape(DEC_BATCH, PAST_LEN, D_MODEL))
            attn = (o_prompt, o_sample, j, b_w_o)
        else:
            x = _conv(x, mods, i, norm_g, j, c_w_in, c_conv_w, c_conv_b, c_w_out)
        x = _ffn(x, mods, i, norm_g, ff_w1, ff_w2, attn=attn, split_out=(i == DEPTH - 1))
    y_prompt = x[0].reshape(BATCH, SEQ, D_MODEL)
    y_sample = x[1].reshape(DEC_BATCH, DEC_SEQ, D_MODEL)
    return (y_prompt, y_sample, jnp.stack(new_k, axis=1), jnp.stack(new_v, axis=1))
```
